```python
import jax
import jax.numpy as jnp
from jax import lax
import numpy as np

D_MODEL = 1024
BATCH = 4
SEQ = 8192
DEPTH = 2

GRID_W = 64
CTX_LEN = 256
D_MIX = D_MODEL
HG_HEAD_DIM = 128
HG_WIDTH = D_MIX // 2
HG_HEADS = HG_WIDTH // HG_HEAD_DIM
HG_CHUNK = 64
MLA_WIDTH = D_MIX - HG_WIDTH
MLA_V = 64
MLA_HEADS = MLA_WIDTH // MLA_V
MLA_NOPE = 64
MLA_ROPE = 32
Q_LORA = 256
KV_LORA = 128
ROPE_BASE = 10000.0
Q_BLOCK = 128
N_EXPERTS = 64
TOP_K = 8
EXPERT_HIDDEN = 256
SHARED_HIDDEN = 256
ROUTED_SCALE = 2.5
DISPATCH_BLOCK = 128
N_MOD = 6
LN_EPS = 1e-6
RMS_EPS = 1e-6
DEEPNORM_ALPHA = (2 * DEPTH) ** 0.25
DEEPNORM_BETA = (8 * DEPTH) ** -0.25
N_IN = 5 * HG_WIDTH + Q_LORA + KV_LORA + MLA_ROPE

kernel_name = "hybrid_hgrn2_mla_moe_dit_block"


def _normalize(x):
    xf = x.astype(jnp.float32)
    mu = jnp.mean(xf, axis=-1, keepdims=True)
    var = jnp.mean(jnp.square(xf - mu), axis=-1, keepdims=True)
    return (xf - mu) * lax.rsqrt(var + LN_EPS)


def layer_norm(x, w, b):
    return (_normalize(x) * w.astype(jnp.float32) + b.astype(jnp.float32)).astype(x.dtype)


def modulate(x, shift, scale):
    y = _normalize(x) * (1.0 + scale.astype(jnp.float32)) + shift.astype(jnp.float32)
    return y.astype(x.dtype)


def rms_norm(x, w):
    xf = x.astype(jnp.float32)
    y = xf * lax.rsqrt(jnp.mean(jnp.square(xf), axis=-1, keepdims=True) + RMS_EPS)
    return (y * w.astype(jnp.float32)).astype(x.dtype)


def axial_rope(n_tok):
    pos = jnp.arange(n_tok, dtype=jnp.int32)
    row = (pos // GRID_W).astype(jnp.float32)
    col = (pos % GRID_W).astype(jnp.float32)
    n_freq = MLA_ROPE // 4
    inv = ROPE_BASE ** (-jnp.arange(n_freq, dtype=jnp.float32) / n_freq)
    ang = jnp.concatenate([row[:, None] * inv, col[:, None] * inv], axis=-1)
    return jnp.cos(ang), jnp.sin(ang)


def apply_rope(x, cos, sin):
    x1, x2 = jnp.split(x.astype(jnp.float32), 2, axis=-1)
    return jnp.concatenate([x1 * cos - x2 * sin, x2 * cos + x1 * sin], axis=-1).astype(x.dtype)


def seg_flip(a, n_ctx):
    return jnp.concatenate([jnp.flip(a[:, :n_ctx], axis=1), jnp.flip(a[:, n_ctx:], axis=1)], axis=1)


def gla_chunk_scan(q, k, v, log_f):
    bsz, n_tok, n_heads, dk = q.shape
    dv = v.shape[-1]
    n_chunks = n_tok // HG_CHUNK

    def chunks(a):
        return a.astype(jnp.float32).reshape(bsz, n_chunks, HG_CHUNK, n_heads, a.shape[-1]).transpose(1, 0, 3, 2, 4)

    lower = jnp.tril(jnp.ones((HG_CHUNK, HG_CHUNK), dtype=bool))[:, :, None]

    def step(state, inp):
        qc, kc, vc, gc = inp
        b = jnp.cumsum(gc, axis=-2)
        b_last = b[..., -1:, :]
        rel = jnp.where(lower, b[..., :, None, :] - b[..., None, :, :], -jnp.inf)
        scores = jnp.sum(qc[..., :, None, :] * kc[..., None, :, :] * jnp.exp(rel), axis=-1)
        o = scores @ vc + jnp.einsum('bhcd,bhdv->bhcv', qc * jnp.exp(b), state)
        state = state * jnp.swapaxes(jnp.exp(b_last), -1, -2) + jnp.einsum('bhcd,bhcv->bhdv', kc * jnp.exp(b_last - b), vc)
        return state, o

    s0 = jnp.zeros((bsz, n_heads, dk, dv), jnp.float32)
    _, o = lax.scan(step, s0, (chunks(q), chunks(k), chunks(v), chunks(log_f)))
    return o.transpose(1, 0, 3, 2, 4).reshape(bsz, n_tok, n_heads, dv)


def hgrn2_mixer(f_fw, f_bw, i_in, q_in, g_in, lb, norm_w, n_ctx):
    bsz, n_tok, _ = i_in.shape

    def heads(a):
        return a.reshape(bsz, n_tok, HG_HEADS, HG_HEAD_DIM)

    q = heads(jax.nn.silu(q_in) * HG_HEAD_DIM ** -0.5)
    v = heads(i_in)

    def forget(z, lb_dir):
        f = lb_dir + (1.0 - lb_dir) * jax.nn.sigmoid(z.astype(jnp.float32))
        return heads(1.0 - f), heads(jnp.log(f))

    k_fw, lf_fw = forget(f_fw, lb[:HG_WIDTH])
    k_bw, lf_bw = forget(f_bw, lb[HG_WIDTH:])
    o_fw = gla_chunk_scan(q, k_fw, v, lf_fw)
    o_bw = seg_flip(gla_chunk_scan(seg_flip(q, n_ctx), seg_flip(k_bw, n_ctx), seg_flip(v, n_ctx), seg_flip(lf_bw, n_ctx)), n_ctx)
    o = rms_norm(o_fw + o_bw, norm_w).reshape(bsz, n_tok, HG_WIDTH)
    return (o * jax.nn.silu(g_in.astype(jnp.float32))).astype(i_in.dtype)


def mla_mixer(cq, ckv, kpe, q_norm_w, w_uq, kv_norm_w, w_ukv, n_ctx, with_ctx):
    bsz, n_tok, _ = cq.shape
    n_lat = n_tok - n_ctx
    q = (rms_norm(cq, q_norm_w) @ w_uq).reshape(bsz, n_tok, MLA_HEADS, MLA_NOPE + MLA_ROPE)
    kv = (rms_norm(ckv, kv_norm_w) @ w_ukv).reshape(bsz, n_tok, MLA_HEADS, MLA_NOPE + MLA_V)
    q_nope, q_pe = q[..., :MLA_NOPE], q[..., MLA_NOPE:]
    k_nope, v = kv[..., :MLA_NOPE], kv[..., MLA_NOPE:]
    cos, sin = axial_rope(n_lat)
    qn_x, qp_x = q_nope[:, n_ctx:], q_pe[:, n_ctx:]
    qr_x = apply_rope(qp_x, cos[:, None, :], sin[:, None, :])
    kn_x, v_x = k_nope[:, n_ctx:], v[:, n_ctx:]
    kr_x = apply_rope(kpe[:, n_ctx:], cos, sin)
    kn_c, v_c, kp_c = k_nope[:, :n_ctx], v[:, :n_ctx], kpe[:, :n_ctx]
    scale = (MLA_NOPE + MLA_ROPE) ** -0.5
    n_blk = n_lat // Q_BLOCK

    def to_blocks(a):
        return jnp.moveaxis(a.reshape(bsz, n_blk, Q_BLOCK, *a.shape[2:]), 1, 0)

    def attend_block(blk):
        qn, qr, qp = blk
        s_lat = jnp.einsum('bqhd,bkhd->bhqk', qn, kn_x) + jnp.einsum('bqhd,bkd->bhqk', qr, kr_x)
        s_ctx = jnp.einsum('bqhd,bkhd->bhqk', qn, kn_c) + jnp.einsum('bqhd,bkd->bhqk', qp, kp_c)
        p = jax.nn.softmax(jnp.concatenate([s_lat, s_ctx], axis=-1).astype(jnp.float32) * scale, axis=-1).astype(v.dtype)
        return jnp.einsum('bhqk,bkhv->bqhv', p[..., :n_lat], v_x) + jnp.einsum('bhqk,bkhv->bqhv', p[..., n_lat:], v_c)

    o_x = lax.map(attend_block, (to_blocks(qn_x), to_blocks(qr_x), to_blocks(qp_x)))
    o_x = jnp.moveaxis(o_x, 0, 1).reshape(bsz, n_lat, MLA_WIDTH)
    if not with_ctx:
        return o_x, None
    s_c = jnp.einsum('bqhd,bkhd->bhqk', q_nope[:, :n_ctx], kn_c) + jnp.einsum('bqhd,bkd->bhqk', q_pe[:, :n_ctx], kp_c)
    p_c = jax.nn.softmax(s_c.astype(jnp.float32) * scale, axis=-1).astype(v.dtype)
    o_c = jnp.einsum('bhqk,bkhv->bqhv', p_c, v_c).reshape(bsz, n_ctx, MLA_WIDTH)
    return o_x, o_c


def token_mixer(hx, hc, w_in, lb, hg_norm_w, q_norm_w, w_uq, kv_norm_w, w_ukv, w_out, with_ctx):
    n_ctx = hc.shape[1]
    h = jnp.concatenate([hc, hx], axis=1)
    proj = h @ w_in
    sizes = (HG_WIDTH,) * 5 + (Q_LORA, KV_LORA, MLA_ROPE)
    f_fw, f_bw, i_in, q_in, g_in, cq, ckv, kpe = jnp.split(proj, np.cumsum(sizes)[:-1].tolist(), axis=-1)
    o_hg = hgrn2_mixer(f_fw, f_bw, i_in, q_in, g_in, lb, hg_norm_w, n_ctx)
    o_mla_x, o_mla_c = mla_mixer(cq, ckv, kpe, q_norm_w, w_uq, kv_norm_w, w_ukv, n_ctx, with_ctx)
    out_x = jnp.concatenate([o_hg[:, n_ctx:], o_mla_x], axis=-1) @ w_out
    if not with_ctx:
        return out_x, None
    out_c = jnp.concatenate([o_hg[:, :n_ctx], o_mla_c], axis=-1) @ w_out
    return out_x, out_c


def moe_ffn(h, router_w, router_bias, w1, w3, w2, sw1, sw3, sw2):
    n_tok, d = h.shape
    scores = jax.nn.sigmoid(jnp.dot(h, router_w).astype(jnp.float32))
    _, top_idx = lax.top_k(scores + router_bias.astype(jnp.float32), TOP_K)
    top_s = jnp.take_along_axis(scores, top_idx, axis=-1)
    top_w = ROUTED_SCALE * top_s / jnp.sum(top_s, axis=-1, keepdims=True)
    n_assign = n_tok * TOP_K
    e_flat = top_idx.reshape(n_assign)
    tok_flat = jnp.repeat(jnp.arange(n_tok, dtype=jnp.int32), TOP_K)
    order = jnp.argsort(e_flat)
    e_sorted = e_flat[order]
    counts = jnp.zeros((N_EXPERTS,), jnp.int32).at[e_flat].add(1)
    padded = (counts + DISPATCH_BLOCK - 1) // DISPATCH_BLOCK * DISPATCH_BLOCK
    pad_end = jnp.cumsum(padded)
    pad_start = pad_end - padded
    start = jnp.cumsum(counts) - counts
    dest = pad_start[e_sorted] + jnp.arange(n_assign, dtype=jnp.int32) - start[e_sorted]
    n_blocks = -(-(n_assign + N_EXPERTS * (DISPATCH_BLOCK - 1)) // DISPATCH_BLOCK)
    n_slots = n_blocks * DISPATCH_BLOCK
    slot_tok = jnp.zeros((n_slots,), jnp.int32).at[dest].set(tok_flat[order])
    slot_w = jnp.zeros((n_slots,), jnp.float32).at[dest].set(top_w.reshape(n_assign)[order])
    block_expert = jnp.minimum(jnp.searchsorted(pad_end, jnp.arange(n_blocks, dtype=jnp.int32) * DISPATCH_BLOCK, side='right'), N_EXPERTS - 1)

    def expert_block(acc, blk):
        idx, wts, e = blk
        xb = h[idx]
        hid = jax.nn.silu(xb @ w1[e]) * (xb @ w3[e])
        out = (hid @ w2[e]).astype(jnp.float32) * wts[:, None]
        return acc.at[idx].add(out), None

    routed, _ = lax.scan(expert_block, jnp.zeros((n_tok, d), jnp.float32),
                         (slot_tok.reshape(n_blocks, DISPATCH_BLOCK), slot_w.reshape(n_blocks, DISPATCH_BLOCK), block_expert))
    shared = (jax.nn.silu(h @ sw1) * (h @ sw3)) @ sw2
    return (routed + shared.astype(jnp.float32)).astype(h.dtype)


def setup_inputs(seed: int = 0) -> dict:
    key = jax.random.key(seed)
    ks = jax.random.split(key, 26)
    D = D_MODEL

    def nrm(k, shape, scale):
        return jax.random.normal(k, shape, jnp.float32) * scale

    return {
        "x": nrm(ks[0], (BATCH, SEQ, D), 1.0),
        "c": nrm(ks[1], (BATCH, D), 1.0),
        "ctx": nrm(ks[2], (BATCH, CTX_LEN, D), 1.0),
        "c_ctx": nrm(ks[3], (D,), 1.0),
        "w_mod": nrm(ks[4], (DEPTH, D, N_MOD * D), D ** -0.5),
        "b_mod": nrm(ks[5], (DEPTH, N_MOD * D), 0.02),
        "w_in": nrm(ks[6], (DEPTH, D, N_IN), D ** -0.5),
        "hg_lb": nrm(ks[7], (DEPTH, 2 * HG_WIDTH), 1.0),
        "hg_norm_w": 1.0 + nrm(ks[8], (DEPTH, HG_HEAD_DIM), 0.02),
        "q_norm_w": 1.0 + nrm(ks[9], (DEPTH, Q_LORA), 0.02),
        "w_uq": nrm(ks[10], (DEPTH, Q_LORA, MLA_HEADS * (MLA_NOPE + MLA_ROPE)), Q_LORA ** -0.5),
        "kv_norm_w": 1.0 + nrm(ks[11], (DEPTH, KV_LORA), 0.02),
        "w_ukv": nrm(ks[12], (DEPTH, KV_LORA, MLA_HEADS * (MLA_NOPE + MLA_V)), KV_LORA ** -0.5),
        "w_out": nrm(ks[13], (DEPTH, D_MIX, D), D_MIX ** -0.5 * DEEPNORM_BETA),
        "ln1_w": 1.0 + nrm(ks[14], (DEPTH, D), 0.02),
        "ln1_b": nrm(ks[15], (DEPTH, D), 0.02),
        "router_w": nrm(ks[16], (DEPTH, D, N_EXPERTS), D ** -0.5),
        "router_bias": nrm(ks[17], (DEPTH, N_EXPERTS), 0.01),
        "moe_w1": nrm(ks[18], (DEPTH, N_EXPERTS, D, EXPERT_HIDDEN), D ** -0.5),
        "moe_w3": nrm(ks[19], (DEPTH, N_EXPERTS, D, EXPERT_HIDDEN), D ** -0.5),
        "moe_w2": nrm(ks[20], (DEPTH, N_EXPERTS, EXPERT_HIDDEN, D), EXPERT_HIDDEN ** -0.5 * DEEPNORM_BETA),
        "shared_w1": nrm(ks[21], (DEPTH, D, SHARED_HIDDEN), D ** -0.5),
        "shared_w3": nrm(ks[22], (DEPTH, D, SHARED_HIDDEN), D ** -0.5),
        "shared_w2": nrm(ks[23], (DEPTH, SHARED_HIDDEN, D), SHARED_HIDDEN ** -0.5 * DEEPNORM_BETA),
        "ln2_w": 1.0 + nrm(ks[24], (DEPTH, D), 0.02),
        "ln2_b": nrm(ks[25], (DEPTH, D), 0.02),
    }


def reference(x, c, ctx, c_ctx, w_mod, b_mod, w_in, hg_lb, hg_norm_w, q_norm_w, w_uq, kv_norm_w, w_ukv,
              w_out, ln1_w, ln1_b, router_w, router_bias, moe_w1, moe_w3, moe_w2, shared_w1, shared_w3,
              shared_w2, ln2_w, ln2_b):
    bsz, n_lat, d = x.shape
    n_ctx = ctx.shape[1]
    lb_all = jnp.cumsum(jax.nn.softmax(hg_lb.astype(jnp.float32), axis=0), axis=0)
    lb_all = lb_all - lb_all[:1]
    cx = ctx
    for l in range(DEPTH):
        with_ctx = l < DEPTH - 1
        mod_x = jnp.dot(jax.nn.silu(c), w_mod[l]) + b_mod[l]
        mod_c = jnp.dot(jax.nn.silu(c_ctx), w_mod[l]) + b_mod[l]
        sh_a, sc_a, g_a, sh_f, sc_f, g_f = jnp.split(mod_x[:, None, :], N_MOD, axis=-1)
        csh_a, csc_a, cg_a, csh_f, csc_f, cg_f = jnp.split(mod_c, N_MOD, axis=-1)
        hx = modulate(x, sh_a, sc_a)
        hc = modulate(cx, csh_a, csc_a)
        mx, mc = token_mixer(hx, hc, w_in[l], lb_all[l], hg_norm_w[l], q_norm_w[l], w_uq[l], kv_norm_w[l],
                             w_ukv[l], w_out[l], with_ctx)
        x = layer_norm(DEEPNORM_ALPHA * x + g_a * mx, ln1_w[l], ln1_b[l])
        hx = modulate(x, sh_f, sc_f)
        if with_ctx:
            cx = layer_norm(DEEPNORM_ALPHA * cx + cg_a * mc, ln1_w[l], ln1_b[l])
            hc = modulate(cx, csh_f, csc_f)
            tokens = jnp.concatenate([hc.reshape(bsz * n_ctx, d), hx.reshape(bsz * n_lat, d)], axis=0)
        else:
            tokens = hx.reshape(bsz * n_lat, d)
        ff = moe_ffn(tokens, router_w[l], router_bias[l], moe_w1[l], moe_w3[l], moe_w2[l],
                     shared_w1[l], shared_w3[l], shared_w2[l])
        if with_ctx:
            fc = ff[:bsz * n_ctx].reshape(bsz, n_ctx, d)
            fx = ff[bsz * n_ctx:].reshape(bsz, n_lat, d)
            cx = layer_norm(DEEPNORM_ALPHA * cx + cg_f * fc, ln2_w[l], ln2_b[l])
        else:
            fx = ff.reshape(bsz, n_lat, d)
        x = layer_norm(DEEPNORM_ALPHA * x + g_f * fx, ln2_w[l], ln2_b[l])
    return x
```

```python
import functools

import jax
import jax.numpy as jnp
from jax import lax
from jax.experimental import pallas as pl
from jax.experimental.pallas import tpu as pltpu

F32 = jnp.float32
BF16 = jnp.bfloat16
U32 = jnp.uint32
I32 = jnp.int32
HIGHEST = lax.Precision.HIGHEST

HG_HEAD_DIM = 128
MLA_V = 64
MLA_NOPE = 64
MLA_ROPE = 32
GRID_W = 64
ROPE_BASE = 10000.0
TOP_K = 8
ROUTED_SCALE = 2.5
N_MOD = 6
LN_EPS = 1e-6
RMS_EPS = 1e-6

LANE = 128
TOK_TILE = 256
GLA_CHUNK = 64
GLA_SUB = 16
ATT_TQ = 256
ATT_TK = 512
ROW_BLOCK = 256
DISPATCH_TILE = 256
COMBINE_TILE = 128
VMEM_LIMIT = 56 * 1024 * 1024
MASKED = -1e30


def _cparams(sem):
    return pltpu.CompilerParams(dimension_semantics=sem, vmem_limit_bytes=VMEM_LIMIT)


def _sigmoid(z):
    return 1.0 / (1.0 + jnp.exp(-z))


def _silu(z):
    return z * _sigmoid(z)


def _normalize(x):
    mu = jnp.mean(x, axis=-1, keepdims=True)
    xc = x - mu
    var = jnp.mean(xc * xc, axis=-1, keepdims=True)
    return xc * lax.rsqrt(var + LN_EPS)


def _pack_bf16_pair(lo, hi):
    lo_u = lax.bitcast_convert_type(lo.astype(BF16).astype(F32), U32) >> 16
    hi_u = lax.bitcast_convert_type(hi.astype(BF16).astype(F32), U32) & jnp.uint32(0xFFFF0000)
    return hi_u | lo_u


def _unpack_bf16_pair(u):
    lo = lax.bitcast_convert_type(u << 16, F32)
    hi = lax.bitcast_convert_type(u & jnp.uint32(0xFFFF0000), F32)
    return lo, hi


def _mod_kernel(c_ref, w_ref, b_ref, o_ref):
    c = c_ref[...]
    o_ref[0] = jnp.dot(_silu(c), w_ref[0], precision=HIGHEST, preferred_element_type=F32) + b_ref[0]


def _modulation(c_rows, w_mod, b_mod):
    depth, d, n = w_mod.shape
    tn = 1536
    return pl.pallas_call(
        _mod_kernel,
        out_shape=jax.ShapeDtypeStruct((depth, 8, n), F32),
        grid=(depth, n // tn),
        in_specs=[
            pl.BlockSpec((8, d), lambda l, j: (0, 0)),
            pl.BlockSpec((1, d, tn), lambda l, j: (l, 0, j)),
            pl.BlockSpec((1, 1, tn), lambda l, j: (l, 0, j)),
        ],
        out_specs=pl.BlockSpec((1, 8, tn), lambda l, j: (l, 0, j)),
        compiler_params=_cparams(("arbitrary", "arbitrary")),
        name="modulation",
    )(c_rows, w_mod, b_mod.reshape(depth, 1, n))


def _inproj_kernel(x_ref, mod_ref, w_ref, lb_ref, qnw_ref, kvnw_ref, wqa_ref, wqb_ref, wk_ref, wv_ref,
                   cos_ref, sin_ref,
                   lff_ref, lfb_ref, kf_ref, kb_ref, q_ref, v_ref, sg_ref, qr_ref, qu_ref, kk_ref, vv_ref,
                   *, d, hw, q_lora, kv_lora, n_heads):
    x = x_ref[0]
    m = mod_ref[0]
    shift = m[:, 0:d]
    scale = m[:, d:2 * d]
    h = (_normalize(x) * (1.0 + scale) + shift).astype(BF16)
    proj = jnp.dot(h, w_ref[...], preferred_element_type=F32)

    lb = lb_ref[...]

    def forget(z, lbd):
        f = lbd + (1.0 - lbd) * _sigmoid(z)
        return jnp.log(f), 1.0 - f

    lf, kd = forget(proj[:, 0:hw], lb[:, 0:hw])
    lff_ref[0] = lf
    kf_ref[0] = kd.astype(BF16)
    lf, kd = forget(proj[:, hw:2 * hw], lb[:, hw:2 * hw])
    lfb_ref[0] = lf
    kb_ref[0] = kd.astype(BF16)
    v_ref[0] = proj[:, 2 * hw:3 * hw].astype(BF16)
    q_ref[0] = (_silu(proj[:, 3 * hw:4 * hw]) * (HG_HEAD_DIM ** -0.5)).astype(BF16)
    sg_ref[0] = _silu(proj[:, 4 * hw:5 * hw]).astype(BF16)

    o = 5 * hw
    cq = proj[:, o:o + q_lora]
    ckv = proj[:, o + q_lora:o + q_lora + kv_lora]
    o2 = o + q_lora + kv_lora
    kpe_a = proj[:, o2:o2 + LANE]
    kpe_b = proj[:, o2 + LANE:o2 + 2 * LANE]

    cqn = (cq * lax.rsqrt(jnp.mean(cq * cq, axis=-1, keepdims=True) + RMS_EPS) * qnw_ref[...]).astype(BF16)
    ckvn = (ckv * lax.rsqrt(jnp.mean(ckv * ckv, axis=-1, keepdims=True) + RMS_EPS) * kvnw_ref[...]).astype(BF16)

    cos = cos_ref[...]
    sin = sin_ref[...]
    cos_h = jnp.concatenate([cos] * n_heads, axis=1)
    sin_h = jnp.concatenate([sin] * n_heads, axis=1)
    att_scale = (MLA_NOPE + MLA_ROPE) ** -0.5
    qa = jnp.dot(cqn, wqa_ref[...], preferred_element_type=F32)
    qb = jnp.dot(cqn, wqb_ref[...], preferred_element_type=F32)
    qu_ref[0] = (qa * att_scale).astype(BF16)
    qr_ref[0] = ((qa * cos_h + qb * sin_h) * att_scale).astype(BF16)

    kr = kpe_a * cos + kpe_b * sin
    kk = jnp.dot(ckvn, wk_ref[...], preferred_element_type=F32) + jnp.concatenate([kr] * n_heads, axis=1)
    kk_ref[0] = kk.astype(BF16)
    vv = jnp.dot(ckvn, wv_ref[...], preferred_element_type=F32)
    lane = lax.broadcasted_iota(I32, vv.shape, 1)
    vv = jnp.where((lane % LANE) == MLA_V, 1.0, vv)
    vv_ref[0] = vv.astype(BF16)


def _inproj(xc, mod3, layer, n_batch, nct, w_ext, lb, qnw, kvnw, wqa, wqb, wk, wv, cos_t, sin_t, n_heads):
    b, t, d = xc.shape
    hw = d // 2
    q_lora = wqa.shape[0]
    kv_lora = wk.shape[0]
    hp = n_heads * LANE
    tm = TOK_TILE
    full = lambda shape: pl.BlockSpec(shape, lambda bi, ti: tuple(0 for _ in shape))
    tok = lambda w: pl.BlockSpec((1, tm, w), lambda bi, ti: (bi, ti, 0))
    out_shape = (
        jax.ShapeDtypeStruct((b, t, hw), F32), jax.ShapeDtypeStruct((b, t, hw), F32),
        jax.ShapeDtypeStruct((b, t, hw), BF16), jax.ShapeDtypeStruct((b, t, hw), BF16),
        jax.ShapeDtypeStruct((b, t, hw), BF16), jax.ShapeDtypeStruct((b, t, hw), BF16),
        jax.ShapeDtypeStruct((b, t, hw), BF16),
        jax.ShapeDtypeStruct((b, t, hp), BF16), jax.ShapeDtypeStruct((b, t, hp), BF16),
        jax.ShapeDtypeStruct((b, t, hp), BF16), jax.ShapeDtypeStruct((b, t, hp), BF16),
    )
    return pl.pallas_call(
        functools.partial(_inproj_kernel, d=d, hw=hw, q_lora=q_lora, kv_lora=kv_lora, n_heads=n_heads),
        out_shape=out_shape,
        grid=(b, t // tm),
        in_specs=[
            tok(d),
            pl.BlockSpec((1, 1, N_MOD * d), lambda bi, ti: (layer * 8 + jnp.where(ti < nct, n_batch, bi), 0, 0)),
            full(w_ext.shape), full(lb.shape), full(qnw.shape), full(kvnw.shape),
            full(wqa.shape), full(wqb.shape), full(wk.shape), full(wv.shape),
            pl.BlockSpec((tm, LANE), lambda bi, ti: (ti, 0)),
            pl.BlockSpec((tm, LANE), lambda bi, ti: (ti, 0)),
        ],
        out_specs=tuple([tok(hw)] * 7 + [tok(hp)] * 4),
        compiler_params=_cparams(("arbitrary", "arbitrary")),
        name="inproj",
    )(xc, mod3, w_ext, lb, qnw, kvnw, wqa, wqb, wk, wv, cos_t, sin_t)


def _gla_kernel(*refs, reverse, n_chunks, n_heads, final):
    if final:
        lf_ref, k_ref, q_ref, v_ref, ofw_ref, sg_ref, nw_ref, o_ref, st_ref = refs
    else:
        lf_ref, k_ref, q_ref, v_ref, o_ref, st_ref = refs
    c, s = GLA_CHUNK, GLA_SUB
    n_sub = c // s

    @pl.when(pl.program_id(1) == 0)
    def _():
        st_ref[...] = jnp.zeros_like(st_ref)

    row = lax.broadcasted_iota(I32, (c, c), 0)
    col = lax.broadcasted_iota(I32, (c, c), 1)
    tri = jnp.where((row <= col) if reverse else (row >= col), 1.0, 0.0).astype(F32)
    sub_row = lax.broadcasted_iota(I32, (s, HG_HEAD_DIM), 0)

    def chunk_body(i, carry):
        ci = (n_chunks - 1 - i) if reverse else i
        r0 = pl.multiple_of(ci * c, c)
        rows = pl.ds(r0, c)
        for h in range(n_heads):
            sl = slice(h * HG_HEAD_DIM, (h + 1) * HG_HEAD_DIM)
            lf = lf_ref[0, rows, sl]
            k = k_ref[0, rows, sl].astype(F32)
            q = q_ref[0, rows, sl].astype(F32)
            v = v_ref[0, rows, sl].astype(F32)
            vb = v.astype(BF16)
            bc = jnp.dot(tri, lf, precision=HIGHEST, preferred_element_type=F32)
            tot = bc[0:1] if reverse else bc[c - 1:c]
            st = st_ref[h]
            qhat = (q * jnp.exp(bc)).astype(BF16)
            o_state = lax.dot_general(qhat, st.astype(BF16), (((1,), (1,)), ((), ())),
                                      preferred_element_type=F32)
            khat = (k * jnp.exp(tot - bc)).astype(BF16)
            st_ref[h] = st * jnp.exp(tot) + jnp.dot(v.T.astype(BF16), khat, preferred_element_type=F32)

            pieces = []
            for si in range(n_sub):
                rs = slice(si * s, (si + 1) * s)
                bs, qs, ks, vs = bc[rs], q[rs], k[rs], v[rs]
                acc = o_state[rs]
                if reverse:
                    src = slice((si + 1) * s, c) if si < n_sub - 1 else None
                    beta = bc[(si + 1) * s:(si + 1) * s + 1] if si < n_sub - 1 else None
                else:
                    src = slice(0, si * s) if si > 0 else None
                    beta = bc[si * s - 1:si * s] if si > 0 else None
                if src is not None:
                    qi = (qs * jnp.exp(bs - beta)).astype(BF16)
                    ksrc = (k[src] * jnp.exp(beta - bc[src])).astype(BF16)
                    a = lax.dot_general(qi, ksrc, (((1,), (1,)), ((), ())), preferred_element_type=F32)
                    acc = acc + jnp.dot(a.astype(BF16), vb[src], preferred_element_type=F32)
                for j in range(s):
                    mask = (sub_row <= j) if reverse else (sub_row >= j)
                    dlt = jnp.where(mask, bs - bs[j:j + 1], MASKED)
                    w = jnp.exp(dlt) * qs * ks[j:j + 1]
                    acc = acc + jnp.sum(w, axis=-1, keepdims=True) * vs[j:j + 1]
                pieces.append(acc)
            o = jnp.concatenate(pieces, axis=0)
            if final:
                o = o + ofw_ref[0, rows, sl]
                o = o * lax.rsqrt(jnp.mean(o * o, axis=-1, keepdims=True) + RMS_EPS) * nw_ref[...]
                o_ref[0, rows, sl] = (o * sg_ref[0, rows, sl].astype(F32)).astype(o_ref.dtype)
            else:
                o_ref[0, rows, sl] = o
        return carry

    lax.fori_loop(0, n_chunks, chunk_body, 0)


def _gla(lf, k, q, v, nct_blocks, reverse, extra=None):
    b, t, hw = lf.shape
    n_heads = hw // HG_HEAD_DIM
    blk = TOK_TILE
    nb = t // blk

    def blk_index(g):
        if not reverse:
            return g
        return jnp.where(g < nct_blocks, nct_blocks - 1 - g, nb - 1 - (g - nct_blocks))

    spec = pl.BlockSpec((1, blk, hw), lambda bi, g: (bi, blk_index(g), 0))
    in_specs = [spec, spec, spec, spec]
    args = [lf, k, q, v]
    final = extra is not None
    if final:
        ofw, sg, nw = extra
        in_specs += [spec, spec, pl.BlockSpec(nw.shape, lambda bi, g: (0, 0))]
        args += [ofw, sg, nw]
    return pl.pallas_call(
        functools.partial(_gla_kernel, reverse=reverse, n_chunks=blk // GLA_CHUNK, n_heads=n_heads, final=final),
        out_shape=jax.ShapeDtypeStruct((b, t, hw), BF16 if final else F32),
        grid=(b, nb),
        in_specs=in_specs,
        out_specs=spec,
        scratch_shapes=[pltpu.VMEM((n_heads, HG_HEAD_DIM, HG_HEAD_DIM), F32)],
        compiler_params=_cparams(("arbitrary", "arbitrary")),
        name="gla_bwd" if reverse else "gla_fwd",
    )(*args)


def _attn_kernel(qu_ref, qr_ref, k_ref, v_ref, o_ref, *, n_ctx, n_lat):
    qi = pl.program_id(2)
    nt = (((1,), (1,)), ((), ()))
    s = lax.dot_general(qu_ref[0], k_ref[0, 0:n_ctx, :], nt, preferred_element_type=F32)
    m = jnp.max(s, axis=-1, keepdims=True)
    p = jnp.exp(s - m)
    acc = jnp.dot(p.astype(BF16), v_ref[0, 0:n_ctx, :], preferred_element_type=F32)
    qr = qr_ref[0]

    def body(j, carry):
        m, acc = carry
        r0 = pl.multiple_of(n_ctx + j * ATT_TK, ATT_TQ)
        s = lax.dot_general(qr, k_ref[0, pl.ds(r0, ATT_TK), :], nt, preferred_element_type=F32)
        m_new = jnp.maximum(m, jnp.max(s, axis=-1, keepdims=True))
        alpha = jnp.exp(m - m_new)
        p = jnp.exp(s - m_new)
        acc = acc * alpha + jnp.dot(p.astype(BF16), v_ref[0, pl.ds(r0, ATT_TK), :], preferred_element_type=F32)
        return m_new, acc

    n_it = jnp.where(qi < n_ctx // ATT_TQ, 0, n_lat // ATT_TK)
    m, acc = lax.fori_loop(0, n_it, body, (m, acc))
    o_ref[0] = (acc * (1.0 / acc[:, MLA_V:MLA_V + 1])).astype(o_ref.dtype)


def _attention(qu, qr, kk, vv, n_ctx, n_heads):
    b, t, hp = qu.shape
    n_lat = t - n_ctx
    qspec = pl.BlockSpec((1, ATT_TQ, LANE), lambda bi, h, qi: (bi, qi, h))
    kspec = pl.BlockSpec((1, t, LANE), lambda bi, h, qi: (bi, 0, h))
    return pl.pallas_call(
        functools.partial(_attn_kernel, n_ctx=n_ctx, n_lat=n_lat),
        out_shape=jax.ShapeDtypeStruct((b, t, hp), BF16),
        grid=(b, n_heads, t // ATT_TQ),
        in_specs=[qspec, qspec, kspec, kspec],
        out_specs=qspec,
        compiler_params=_cparams(("arbitrary", "arbitrary", "arbitrary")),
        name="mla_attention",
    )(qu, qr, kk, vv)


def _postmix_kernel(x_ref, ohg_ref, omla_ref, mod_ref, wohg_ref, womla_ref, ln1w_ref, ln1b_ref,
                    rw_ref, rb_ref, sw1_ref, sw3_ref, sw2_ref,
                    x1_ref, h2p_ref, sh_ref, idx_ref, wts_ref, pos_ref, cnt_ref, cnt_sc,
                    *, d, alpha, n_experts):
    @pl.when((pl.program_id(0) == 0) & (pl.program_id(1) == 0))
    def _():
        cnt_sc[...] = jnp.zeros_like(cnt_sc)

    m = mod_ref[0]
    g_a = m[:, 2 * d:3 * d]
    sh_f = m[:, 3 * d:4 * d]
    sc_f = m[:, 4 * d:5 * d]
    mix = (jnp.dot(ohg_ref[0], wohg_ref[...], preferred_element_type=F32)
           + jnp.dot(omla_ref[0], womla_ref[...], preferred_element_type=F32))
    x1 = _normalize(alpha * x_ref[0] + g_a * mix) * ln1w_ref[...] + ln1b_ref[...]
    x1_ref[0] = x1
    h2 = _normalize(x1) * (1.0 + sc_f) + sh_f
    hw = d // 2
    h2p_ref[0] = _pack_bf16_pair(h2[:, 0:hw], h2[:, hw:d])
    h2b = h2.astype(BF16)

    a1 = jnp.dot(h2b, sw1_ref[...], preferred_element_type=F32)
    a3 = jnp.dot(h2b, sw3_ref[...], preferred_element_type=F32)
    hid = (_silu(a1) * a3).astype(BF16)
    sh_ref[0] = jnp.dot(hid, sw2_ref[...], preferred_element_type=F32).astype(sh_ref.dtype)

    logits = jnp.dot(h2, rw_ref[...], precision=HIGHEST, preferred_element_type=F32)
    scores = _sigmoid(logits)
    sel = scores + rb_ref[...]
    tm = scores.shape[0]
    lane = lax.broadcasted_iota(I32, (tm, n_experts), 1)
    masks, tops, idxs = [], [], []
    for _ in range(TOP_K):
        mx = jnp.max(sel, axis=-1, keepdims=True)
        ik = jnp.min(jnp.where(sel == mx, lane, n_experts), axis=-1, keepdims=True)
        oh = lane == ik
        masks.append(oh)
        idxs.append(ik)
        tops.append(jnp.sum(jnp.where(oh, scores, 0.0), axis=-1, keepdims=True))
        sel = jnp.where(oh, -jnp.inf, sel)
    tsum = tops[0]
    for tk in tops[1:]:
        tsum = tsum + tk
    inv = ROUTED_SCALE / tsum

    ohf = jnp.zeros((tm, n_experts), F32)
    for oh in masks:
        ohf = jnp.where(oh, 1.0, ohf)
    r = lax.broadcasted_iota(I32, (tm, tm), 0)
    cc = lax.broadcasted_iota(I32, (tm, tm), 1)
    strict = jnp.where(r > cc, 1.0, 0.0).astype(BF16)
    before = jnp.dot(strict, ohf.astype(BF16), preferred_element_type=F32) + cnt_sc[...]
    cnt_new = cnt_sc[...] + jnp.sum(ohf, axis=0, keepdims=True)
    cnt_sc[...] = cnt_new
    cnt_ref[...] = cnt_new.astype(I32)

    lane_k = lax.broadcasted_iota(I32, (tm, TOP_K), 1)
    idx_o = jnp.zeros((tm, TOP_K), I32)
    wts_o = jnp.zeros((tm, TOP_K), F32)
    pos_o = jnp.zeros((tm, TOP_K), I32)
    for kk in range(TOP_K):
        pk = jnp.sum(jnp.where(masks[kk], before, 0.0), axis=-1, keepdims=True)
        hit = lane_k == kk
        idx_o = jnp.where(hit, idxs[kk], idx_o)
        wts_o = jnp.where(hit, tops[kk] * inv, wts_o)
        pos_o = jnp.where(hit, pk.astype(I32), pos_o)
    idx_ref[0] = idx_o
    wts_ref[0] = wts_o
    pos_ref[0] = pos_o


def _postmix(xc, ohg, omla, mod3, layer, n_batch, nct, wohg, womla, ln1w, ln1b, rw, rb, sw1, sw3, sw2, alpha):
    b, t, d = xc.shape
    tm = TOK_TILE
    n_experts = rw.shape[1]
    full = lambda a: pl.BlockSpec(a.shape, lambda bi, ti: tuple(0 for _ in a.shape))
    tok = lambda w: pl.BlockSpec((1, tm, w), lambda bi, ti: (bi, ti, 0))
    out_shape = (
        jax.ShapeDtypeStruct((b, t, d), F32),
        jax.ShapeDtypeStruct((b, t, d // 2), U32),
        jax.ShapeDtypeStruct((b, t, d), BF16),
        jax.ShapeDtypeStruct((b, t, TOP_K), I32),
        jax.ShapeDtypeStruct((b, t, TOP_K), F32),
        jax.ShapeDtypeStruct((b, t, TOP_K), I32),
        jax.ShapeDtypeStruct((1, n_experts), I32),
    )
    return pl.pallas_call(
        functools.partial(_postmix_kernel, d=d, alpha=alpha, n_experts=n_experts),
        out_shape=out_shape,
        grid=(b, t // tm),
        in_specs=[
            tok(d), tok(ohg.shape[2]), tok(omla.shape[2]),
            pl.BlockSpec((1, 1, N_MOD * d), lambda bi, ti: (layer * 8 + jnp.where(ti < nct, n_batch, bi), 0, 0)),
            full(wohg), full(womla), full(ln1w), full(ln1b), full(rw), full(rb), full(sw1), full(sw3), full(sw2),
        ],
        out_specs=(tok(d), tok(d // 2), tok(d), tok(TOP_K), tok(TOP_K), tok(TOP_K),
                   pl.BlockSpec((1, n_experts), lambda bi, ti: (0, 0))),
        scratch_shapes=[pltpu.VMEM((1, n_experts), F32)],
        compiler_params=_cparams(("arbitrary", "arbitrary")),
        name="postmix_router",
    )(xc, ohg, omla, mod3, wohg, womla, ln1w, ln1b, rw, rb, sw1, sw3, sw2)


def _row_copy(src_ref, src_row, dst_ref, dst_row, sem):
    return pltpu.make_async_copy(src_ref.at[pl.ds(src_row, 1)], dst_ref.at[pl.ds(dst_row, 1)], sem)


def _dispatch_kernel(dest_ref, h_ref, xs_in_ref, xs_ref, sem):
    del xs_in_ref
    base = pl.program_id(0) * DISPATCH_TILE
    n = DISPATCH_TILE * TOP_K

    def issue(r, c):
        _row_copy(h_ref, base + r // TOP_K, xs_ref, dest_ref[0, 0, r], sem).start()
        return c

    lax.fori_loop(0, n, issue, 0, unroll=8)

    def drain(r, c):
        _row_copy(h_ref, 0, xs_ref, 0, sem).wait()
        return c

    lax.fori_loop(0, n, drain, 0, unroll=8)


def _dispatch(dest, h2p, n_slots):
    n, w = h2p.shape
    steps = n // DISPATCH_TILE
    dest3 = dest.reshape(steps, 1, DISPATCH_TILE * TOP_K)
    xs0 = jnp.zeros((n_slots, w), U32)
    return pl.pallas_call(
        _dispatch_kernel,
        out_shape=jax.ShapeDtypeStruct((n_slots, w), U32),
        grid=(steps,),
        in_specs=[
            pl.BlockSpec((1, 1, DISPATCH_TILE * TOP_K), lambda i: (i, 0, 0), memory_space=pltpu.SMEM),
            pl.BlockSpec(memory_space=pl.ANY),
            pl.BlockSpec(memory_space=pl.ANY),
        ],
        out_specs=pl.BlockSpec(memory_space=pl.ANY),
        scratch_shapes=[pltpu.SemaphoreType.DMA],
        input_output_aliases={2: 0},
        compiler_params=_cparams(("arbitrary",)),
        name="moe_dispatch",
    )(dest3, h2p, xs0)


def _expert_kernel(be_ref, nused_ref, xs_ref, w1_ref, w3_ref, w2_ref, ys_ref, w1b, w3b, w2b):
    i = pl.program_id(0)
    used = i < nused_ref[0]

    @pl.when(used)
    def _():
        changed = (i == 0) | (be_ref[i] != be_ref[jnp.maximum(i - 1, 0)])

        @pl.when(changed)
        def _():
            w1b[...] = w1_ref[0, 0].astype(BF16)
            w3b[...] = w3_ref[0, 0].astype(BF16)
            w2b[...] = w2_ref[0, 0].astype(BF16)

        lo, hi = _unpack_bf16_pair(xs_ref[...])
        x = jnp.concatenate([lo.astype(BF16), hi.astype(BF16)], axis=1)
        h1 = jnp.dot(x, w1b[...], preferred_element_type=F32)
        h3 = jnp.dot(x, w3b[...], preferred_element_type=F32)
        hid = (_silu(h1) * h3).astype(BF16)
        out = jnp.dot(hid, w2b[...], preferred_element_type=F32)
        half = out.shape[1] // 2
        ys_ref[...] = _pack_bf16_pair(out[:, 0:half], out[:, half:])

    @pl.when(jnp.logical_not(used))
    def _():
        ys_ref[...] = jnp.zeros_like(ys_ref)


def _experts(xs, block_expert, n_used, w1, w3, w2, layer):
    n_slots, w = xs.shape
    n_blocks = n_slots // ROW_BLOCK
    _, _, d, hid = w1.shape
    grid_spec = pltpu.PrefetchScalarGridSpec(
        num_scalar_prefetch=2,
        grid=(n_blocks,),
        in_specs=[
            pl.BlockSpec((ROW_BLOCK, w), lambda i, be, nu: (i, 0)),
            pl.BlockSpec((1, 1, d, hid), lambda i, be, nu: (layer, be[i], 0, 0)),
            pl.BlockSpec((1, 1, d, hid), lambda i, be, nu: (layer, be[i], 0, 0)),
            pl.BlockSpec((1, 1, hid, d), lambda i, be, nu: (layer, be[i], 0, 0)),
        ],
        out_specs=pl.BlockSpec((ROW_BLOCK, w), lambda i, be, nu: (i, 0)),
        scratch_shapes=[pltpu.VMEM((d, hid), BF16), pltpu.VMEM((d, hid), BF16), pltpu.VMEM((hid, d), BF16)],
    )
    return pl.pallas_call(
        _expert_kernel,
        out_shape=jax.ShapeDtypeStruct((n_slots, w), U32),
        grid_spec=grid_spec,
        compiler_params=_cparams(("arbitrary",)),
        name="moe_experts",
    )(block_expert, n_used, xs, w1, w3, w2)


def _combine_kernel(dest_ref, wts_ref, ys_ref, sh_ref, x1_ref, mod_ref, lnw_ref, lnb_ref, o_ref, buf, sem,
                    *, d, alpha):
    n = COMBINE_TILE * TOP_K

    def issue(r, c):
        pltpu.make_async_copy(ys_ref.at[pl.ds(dest_ref[0, 0, r], 1)],
                              buf.at[r % TOP_K, pl.ds(r // TOP_K, 1)], sem).start()
        return c

    lax.fori_loop(0, n, issue, 0, unroll=8)

    def drain(r, c):
        pltpu.make_async_copy(ys_ref.at[pl.ds(0, 1)], buf.at[0, pl.ds(0, 1)], sem).wait()
        return c

    lax.fori_loop(0, n, drain, 0, unroll=8)

    hw = d // 2
    sh = sh_ref[0].astype(F32)
    acc_lo = sh[:, 0:hw]
    acc_hi = sh[:, hw:d]
    wts = wts_ref[0]
    for kk in range(TOP_K):
        lo, hi = _unpack_bf16_pair(buf[kk])
        wk = wts[:, kk:kk + 1]
        acc_lo = acc_lo + wk * lo
        acc_hi = acc_hi + wk * hi
    ff = jnp.concatenate([acc_lo, acc_hi], axis=1)
    g_f = mod_ref[0][:, 5 * d:6 * d]
    o_ref[0] = _normalize(alpha * x1_ref[0] + g_f * ff) * lnw_ref[...] + lnb_ref[...]


def _combine(dest, wts, ys, shared, x1, mod3, layer, n_batch, nct_tiles, lnw, lnb, alpha):
    b, t, d = x1.shape
    tm = COMBINE_TILE
    tpb = t // tm
    dest3 = dest.reshape(b * tpb, 1, tm * TOP_K)
    tok = lambda w: pl.BlockSpec((1, tm, w), lambda bi, ti: (bi, ti, 0))
    full = lambda a: pl.BlockSpec(a.shape, lambda bi, ti: tuple(0 for _ in a.shape))
    return pl.pallas_call(
        functools.partial(_combine_kernel, d=d, alpha=alpha),
        out_shape=jax.ShapeDtypeStruct((b, t, d), F32),
        grid=(b, tpb),
        in_specs=[
            pl.BlockSpec((1, 1, tm * TOP_K), lambda bi, ti: (bi * tpb + ti, 0, 0), memory_space=pltpu.SMEM),
            tok(TOP_K),
            pl.BlockSpec(memory_space=pl.ANY),
            tok(d), tok(d),
            pl.BlockSpec((1, 1, N_MOD * d), lambda bi, ti: (layer * 8 + jnp.where(ti < nct_tiles, n_batch, bi), 0, 0)),
            full(lnw), full(lnb),
        ],
        out_specs=tok(d),
        scratch_shapes=[pltpu.VMEM((TOP_K, tm, d // 2), U32), pltpu.SemaphoreType.DMA],
        compiler_params=_cparams(("arbitrary", "arbitrary")),
        name="moe_combine",
    )(dest3, wts, ys, shared, x1, mod3, lnw, lnb)


def _rope_tables(n_ctx, n_lat):
    pos = jnp.arange(n_lat, dtype=I32)
    rowp = (pos // GRID_W).astype(F32)
    colp = (pos % GRID_W).astype(F32)
    n_freq = MLA_ROPE // 4
    inv = ROPE_BASE ** (-jnp.arange(n_freq, dtype=F32) / n_freq)
    ang = jnp.concatenate([rowp[:, None] * inv, colp[:, None] * inv], axis=-1)
    ang = jnp.concatenate([jnp.zeros((n_ctx, MLA_ROPE // 2), F32), ang], axis=0)
    t = n_ctx + n_lat
    ones = jnp.ones((t, MLA_NOPE), F32)
    zeros_tail = jnp.zeros((t, LANE - MLA_NOPE - MLA_ROPE), F32)
    cos_t = jnp.concatenate([ones, jnp.cos(ang), jnp.cos(ang), zeros_tail], axis=1)
    sin_t = jnp.concatenate([jnp.zeros((t, MLA_NOPE), F32), jnp.sin(ang), jnp.sin(ang), zeros_tail], axis=1)
    return cos_t, sin_t


def _rope_swap(w):
    half = MLA_ROPE // 2
    return jnp.concatenate([-w[..., half:], w[..., :half]], axis=-1)


def _prep_layer_weights(w_in_l, w_uq_l, w_ukv_l, w_out_l, hw, q_lora, kv_lora, n_heads):
    d = w_in_l.shape[0]
    base = 5 * hw + q_lora + kv_lora
    kpe_w = w_in_l[:, base:base + MLA_ROPE]
    z_nope = jnp.zeros((d, MLA_NOPE), F32)
    z_tail = jnp.zeros((d, LANE - MLA_NOPE - MLA_ROPE), F32)
    w_ext = jnp.concatenate([w_in_l[:, :base], z_nope, kpe_w, z_tail, z_nope, _rope_swap(kpe_w), z_tail],
                            axis=1).astype(BF16)
    wq = w_uq_l.reshape(q_lora, n_heads, MLA_NOPE + MLA_ROPE)
    zq = jnp.zeros((q_lora, n_heads, LANE - MLA_NOPE - MLA_ROPE), F32)
    wqa = jnp.concatenate([wq, zq], axis=-1).reshape(q_lora, n_heads * LANE).astype(BF16)
    wqb = jnp.concatenate([jnp.zeros((q_lora, n_heads, MLA_NOPE), F32), _rope_swap(wq[..., MLA_NOPE:]), zq],
                          axis=-1).reshape(q_lora, n_heads * LANE).astype(BF16)
    wkv = w_ukv_l.reshape(kv_lora, n_heads, MLA_NOPE + MLA_V)
    wk = jnp.concatenate([wkv[..., :MLA_NOPE], jnp.zeros((kv_lora, n_heads, LANE - MLA_NOPE), F32)],
                         axis=-1).reshape(kv_lora, n_heads * LANE).astype(BF16)
    wv = jnp.concatenate([wkv[..., MLA_NOPE:], jnp.zeros((kv_lora, n_heads, LANE - MLA_V), F32)],
                         axis=-1).reshape(kv_lora, n_heads * LANE).astype(BF16)
    wohg = w_out_l[:hw].astype(BF16)
    wom = w_out_l[hw:].reshape(n_heads, MLA_V, d)
    womla = jnp.concatenate([wom, jnp.zeros((n_heads, LANE - MLA_V, d), F32)], axis=1)
    womla = womla.reshape(n_heads * LANE, d).astype(BF16)
    return w_ext, wqa, wqb, wk, wv, wohg, womla


def kernel(x, c, ctx, c_ctx, w_mod, b_mod, w_in, hg_lb, hg_norm_w, q_norm_w, w_uq, kv_norm_w, w_ukv, w_out,
           ln1_w, ln1_b, router_w, router_bias, moe_w1, moe_w3, moe_w2, shared_w1, shared_w3, shared_w2,
           ln2_w, ln2_b):
    bsz, n_lat, d = x.shape
    n_ctx = ctx.shape[1]
    depth = w_mod.shape[0]
    t = n_ctx + n_lat
    hw = d // 2
    q_lora = w_uq.shape[1]
    kv_lora = w_ukv.shape[1]
    n_heads = (d - hw) // MLA_V
    n_experts = router_w.shape[2]
    alpha = float((2 * depth) ** 0.25)
    assert n_ctx % TOK_TILE == 0 and n_lat % ATT_TK == 0 and bsz < 8
    assert n_ctx % ATT_TQ == 0 and (bsz * t) % DISPATCH_TILE == 0 and t % COMBINE_TILE == 0

    lb_all = jnp.cumsum(jax.nn.softmax(hg_lb.astype(F32), axis=0), axis=0)
    lb_all = lb_all - lb_all[:1]

    c_rows = jnp.concatenate([c, c_ctx[None, :], jnp.zeros((8 - bsz - 1, d), F32)], axis=0)
    mod = _modulation(c_rows, w_mod, b_mod)
    mod3 = mod.reshape(depth * 8, 1, N_MOD * d)

    cos_t, sin_t = _rope_tables(n_ctx, n_lat)
    xc = jnp.concatenate([ctx, x], axis=1)
    nct = n_ctx // TOK_TILE

    n_tok = bsz * t
    n_assign = n_tok * TOP_K
    n_blocks = -(-(n_assign + n_experts * (ROW_BLOCK - 1)) // ROW_BLOCK)
    n_slots = n_blocks * ROW_BLOCK

    for l in range(depth):
        w_ext, wqa, wqb, wk, wv, wohg, womla = _prep_layer_weights(
            w_in[l], w_uq[l], w_ukv[l], w_out[l], hw, q_lora, kv_lora, n_heads)
        (lf_fw, lf_bw, k_fw, k_bw, q_hg, v_hg, sg, q_rot, q_unrot, kk, vv) = _inproj(
            xc, mod3, l, bsz, nct, w_ext, lb_all[l][None, :], q_norm_w[l][None, :], kv_norm_w[l][None, :],
            wqa, wqb, wk, wv, cos_t, sin_t, n_heads)
        o_fw = _gla(lf_fw, k_fw, q_hg, v_hg, nct, reverse=False)
        o_hg = _gla(lf_bw, k_bw, q_hg, v_hg, nct, reverse=True, extra=(o_fw, sg, hg_norm_w[l][None, :]))
        o_mla = _attention(q_unrot, q_rot, kk, vv, n_ctx, n_heads)
        x1, h2p, shared, idx, wts, pos, counts = _postmix(
            xc, o_hg, o_mla, mod3, l, bsz, nct, wohg, womla, ln1_w[l][None, :], ln1_b[l][None, :],
            router_w[l], router_bias[l][None, :], shared_w1[l].astype(BF16), shared_w3[l].astype(BF16),
            shared_w2[l].astype(BF16), alpha)

        counts = counts[0]
        padded = (counts + ROW_BLOCK - 1) // ROW_BLOCK * ROW_BLOCK
        pad_end = jnp.cumsum(padded)
        pad_start = pad_end - padded
        dest = (pad_start[idx.reshape(n_tok, TOP_K)] + pos.reshape(n_tok, TOP_K)).astype(I32)
        block_expert = jnp.minimum(
            jnp.searchsorted(pad_end, jnp.arange(n_blocks, dtype=I32) * ROW_BLOCK, side="right"),
            n_experts - 1).astype(I32)
        n_used = (pad_end[-1:] // ROW_BLOCK).astype(I32)

        xs = _dispatch(dest, h2p.reshape(n_tok, hw), n_slots)
        ys = _experts(xs, block_expert, n_used, moe_w1, moe_w3, moe_w2, l)
        xc = _combine(dest, wts, ys, shared, x1, mod3, l, bsz, n_ctx // COMBINE_TILE,
                      ln2_w[l][None, :], ln2_b[l][None, :], alpha)
    return xc[:, n_ctx:, :]
```

```python
import functools

import jax
import jax.numpy as jnp
from jax import lax
from jax.experimental import pallas as pl
from jax.experimental.pallas import tpu as pltpu

F32 = jnp.float32
BF16 = jnp.bfloat16
U32 = jnp.uint32
I32 = jnp.int32
HIGHEST = lax.Precision.HIGHEST

HG_HEAD_DIM = 128
MLA_V = 64
MLA_NOPE = 64
MLA_ROPE = 32
GRID_W = 64
ROPE_BASE = 10000.0
TOP_K = 8
ROUTED_SCALE = 2.5
N_MOD = 6
LN_EPS = 1e-6
RMS_EPS = 1e-6

LANE = 128
TOK_TILE = 256
GLA_CHUNK = 64
GLA_SUB = 16
ATT_TQ = 256
ATT_TK = 1024
LOG2_E = 1.4426950408889634
ROW_BLOCK = 256
DISPATCH_TILE = 256
COMBINE_TILE = 128
VMEM_LIMIT = 56 * 1024 * 1024
MASKED = -1e30


def _cparams(sem):
    return pltpu.CompilerParams(dimension_semantics=sem, vmem_limit_bytes=VMEM_LIMIT)


def _sigmoid(z):
    return 1.0 / (1.0 + jnp.exp(-z))


def _silu(z):
    return z * _sigmoid(z)


def _normalize(x):
    mu = jnp.mean(x, axis=-1, keepdims=True)
    xc = x - mu
    var = jnp.mean(xc * xc, axis=-1, keepdims=True)
    return xc * lax.rsqrt(var + LN_EPS)


def _pack_bf16_pair(lo, hi):
    lo_u = lax.bitcast_convert_type(lo.astype(BF16).astype(F32), U32) >> 16
    hi_u = lax.bitcast_convert_type(hi.astype(BF16).astype(F32), U32) & jnp.uint32(0xFFFF0000)
    return hi_u | lo_u


def _unpack_bf16_pair(u):
    lo = lax.bitcast_convert_type(u << 16, F32)
    hi = lax.bitcast_convert_type(u & jnp.uint32(0xFFFF0000), F32)
    return lo, hi


def _mod_kernel(c_ref, w_ref, b_ref, o_ref):
    c = c_ref[...]
    o_ref[0] = jnp.dot(_silu(c), w_ref[0], precision=HIGHEST, preferred_element_type=F32) + b_ref[0]


def _modulation(c_rows, w_mod, b_mod):
    depth, d, n = w_mod.shape
    tn = 1536
    return pl.pallas_call(
        _mod_kernel,
        out_shape=jax.ShapeDtypeStruct((depth, 8, n), F32),
        grid=(depth, n // tn),
        in_specs=[
            pl.BlockSpec((8, d), lambda l, j: (0, 0)),
            pl.BlockSpec((1, d, tn), lambda l, j: (l, 0, j)),
            pl.BlockSpec((1, 1, tn), lambda l, j: (l, 0, j)),
        ],
        out_specs=pl.BlockSpec((1, 8, tn), lambda l, j: (l, 0, j)),
        compiler_params=_cparams(("arbitrary", "arbitrary")),
        name="modulation",
    )(c_rows, w_mod, b_mod.reshape(depth, 1, n))


def _inproj_kernel(x_ref, mod_ref, w_ref, lb_ref, qnw_ref, kvnw_ref, wqa_ref, wqb_ref, wk_ref, wv_ref,
                   cos_ref, sin_ref,
                   lff_ref, lfb_ref, kf_ref, kb_ref, q_ref, v_ref, sg_ref, qr_ref, qu_ref, kk_ref, vv_ref,
                   *, d, hw, q_lora, kv_lora, n_heads):
    x = x_ref[0]
    m = mod_ref[0]
    shift = m[:, 0:d]
    scale = m[:, d:2 * d]
    h = (_normalize(x) * (1.0 + scale) + shift).astype(BF16)
    proj = jnp.dot(h, w_ref[...], preferred_element_type=F32)

    lb = lb_ref[...]

    def forget(z, lbd):
        f = lbd + (1.0 - lbd) * _sigmoid(z)
        return jnp.log(f), 1.0 - f

    lf, kd = forget(proj[:, 0:hw], lb[:, 0:hw])
    lff_ref[0] = lf
    kf_ref[0] = kd.astype(BF16)
    lf, kd = forget(proj[:, hw:2 * hw], lb[:, hw:2 * hw])
    lfb_ref[0] = lf
    kb_ref[0] = kd.astype(BF16)
    v_ref[0] = proj[:, 2 * hw:3 * hw].astype(BF16)
    q_ref[0] = (_silu(proj[:, 3 * hw:4 * hw]) * (HG_HEAD_DIM ** -0.5)).astype(BF16)
    sg_ref[0] = _silu(proj[:, 4 * hw:5 * hw]).astype(BF16)

    o = 5 * hw
    cq = proj[:, o:o + q_lora]
    ckv = proj[:, o + q_lora:o + q_lora + kv_lora]
    o2 = o + q_lora + kv_lora
    kpe_a = proj[:, o2:o2 + LANE]
    kpe_b = proj[:, o2 + LANE:o2 + 2 * LANE]

    cqn = (cq * lax.rsqrt(jnp.mean(cq * cq, axis=-1, keepdims=True) + RMS_EPS) * qnw_ref[...]).astype(BF16)
    ckvn = (ckv * lax.rsqrt(jnp.mean(ckv * ckv, axis=-1, keepdims=True) + RMS_EPS) * kvnw_ref[...]).astype(BF16)

    cos = cos_ref[...]
    sin = sin_ref[...]
    cos_h = jnp.concatenate([cos] * n_heads, axis=1)
    sin_h = jnp.concatenate([sin] * n_heads, axis=1)
    att_scale = (MLA_NOPE + MLA_ROPE) ** -0.5 * LOG2_E
    qa = jnp.dot(cqn, wqa_ref[...], preferred_element_type=F32)
    qb = jnp.dot(cqn, wqb_ref[...], preferred_element_type=F32)
    qu_ref[0] = (qa * att_scale).astype(BF16)
    qr_ref[0] = ((qa * cos_h + qb * sin_h) * att_scale).astype(BF16)

    kr = kpe_a * cos + kpe_b * sin
    kk = jnp.dot(ckvn, wk_ref[...], preferred_element_type=F32) + jnp.concatenate([kr] * n_heads, axis=1)
    kk_ref[0] = kk.astype(BF16)
    vv = jnp.dot(ckvn, wv_ref[...], preferred_element_type=F32)
    lane = lax.broadcasted_iota(I32, vv.shape, 1)
    vv = jnp.where((lane % LANE) == MLA_V, 1.0, vv)
    vv_ref[0] = vv.astype(BF16)


def _inproj(xc, mod3, layer, n_batch, nct, w_ext, lb, qnw, kvnw, wqa, wqb, wk, wv, cos_t, sin_t, n_heads):
    b, t, d = xc.shape
    hw = d // 2
    q_lora = wqa.shape[0]
    kv_lora = wk.shape[0]
    hp = n_heads * LANE
    tm = TOK_TILE
    full = lambda shape: pl.BlockSpec(shape, lambda bi, ti: tuple(0 for _ in shape))
    tok = lambda w: pl.BlockSpec((1, tm, w), lambda bi, ti: (bi, ti, 0))
    out_shape = (
        jax.ShapeDtypeStruct((b, t, hw), F32), jax.ShapeDtypeStruct((b, t, hw), F32),
        jax.ShapeDtypeStruct((b, t, hw), BF16), jax.ShapeDtypeStruct((b, t, hw), BF16),
        jax.ShapeDtypeStruct((b, t, hw), BF16), jax.ShapeDtypeStruct((b, t, hw), BF16),
        jax.ShapeDtypeStruct((b, t, hw), BF16),
        jax.ShapeDtypeStruct((b, t, hp), BF16), jax.ShapeDtypeStruct((b, t, hp), BF16),
        jax.ShapeDtypeStruct((b, t, hp), BF16), jax.ShapeDtypeStruct((b, t, hp), BF16),
    )
    return pl.pallas_call(
        functools.partial(_inproj_kernel, d=d, hw=hw, q_lora=q_lora, kv_lora=kv_lora, n_heads=n_heads),
        out_shape=out_shape,
        grid=(b, t // tm),
        in_specs=[
            tok(d),
            pl.BlockSpec((1, 1, N_MOD * d), lambda bi, ti: (layer * 8 + jnp.where(ti < nct, n_batch, bi), 0, 0)),
            full(w_ext.shape), full(lb.shape), full(qnw.shape), full(kvnw.shape),
            full(wqa.shape), full(wqb.shape), full(wk.shape), full(wv.shape),
            pl.BlockSpec((tm, LANE), lambda bi, ti: (ti, 0)),
            pl.BlockSpec((tm, LANE), lambda bi, ti: (ti, 0)),
        ],
        out_specs=tuple([tok(hw)] * 7 + [tok(hp)] * 4),
        compiler_params=_cparams(("arbitrary", "arbitrary")),
        name="inproj",
    )(xc, mod3, w_ext, lb, qnw, kvnw, wqa, wqb, wk, wv, cos_t, sin_t)


def _gla_kernel(*refs, reverse, n_chunks, n_heads, final):
    if final:
        lf_ref, k_ref, q_ref, v_ref, ofw_ref, sg_ref, nw_ref, o_ref, st_ref = refs
    else:
        lf_ref, k_ref, q_ref, v_ref, o_ref, st_ref = refs
    c, s = GLA_CHUNK, GLA_SUB
    n_sub = c // s

    @pl.when(pl.program_id(1) == 0)
    def _():
        st_ref[...] = jnp.zeros_like(st_ref)

    row = lax.broadcasted_iota(I32, (c, c), 0)
    col = lax.broadcasted_iota(I32, (c, c), 1)
    tri = jnp.where((row <= col) if reverse else (row >= col), 1.0, 0.0).astype(F32)
    sub_row = lax.broadcasted_iota(I32, (s, HG_HEAD_DIM), 0)

    def chunk_body(i, carry):
        ci = (n_chunks - 1 - i) if reverse else i
        r0 = pl.multiple_of(ci * c, c)
        rows = pl.ds(r0, c)
        for h in range(n_heads):
            sl = slice(h * HG_HEAD_DIM, (h + 1) * HG_HEAD_DIM)
            lf = lf_ref[0, rows, sl]
            k = k_ref[0, rows, sl].astype(F32)
            q = q_ref[0, rows, sl].astype(F32)
            v = v_ref[0, rows, sl].astype(F32)
            vb = v.astype(BF16)
            bc = jnp.dot(tri, lf, precision=HIGHEST, preferred_element_type=F32)
            tot = bc[0:1] if reverse else bc[c - 1:c]
            st = st_ref[h]
            qhat = (q * jnp.exp(bc)).astype(BF16)
            o_state = lax.dot_general(qhat, st.astype(BF16), (((1,), (1,)), ((), ())),
                                      preferred_element_type=F32)
            khat = (k * jnp.exp(tot - bc)).astype(BF16)
            st_ref[h] = st * jnp.exp(tot) + jnp.dot(v.T.astype(BF16), khat, preferred_element_type=F32)

            pieces = []
            for si in range(n_sub):
                rs = slice(si * s, (si + 1) * s)
                bs, qs, ks, vs = bc[rs], q[rs], k[rs], v[rs]
                acc = o_state[rs]
                if reverse:
                    src = slice((si + 1) * s, c) if si < n_sub - 1 else None
                    beta = bc[(si + 1) * s:(si + 1) * s + 1] if si < n_sub - 1 else None
                else:
                    src = slice(0, si * s) if si > 0 else None
                    beta = bc[si * s - 1:si * s] if si > 0 else None
                if src is not None:
                    qi = (qs * jnp.exp(bs - beta)).astype(BF16)
                    ksrc = (k[src] * jnp.exp(beta - bc[src])).astype(BF16)
                    a = lax.dot_general(qi, ksrc, (((1,), (1,)), ((), ())), preferred_element_type=F32)
                    acc = acc + jnp.dot(a.astype(BF16), vb[src], preferred_element_type=F32)
                for j in range(s):
                    mask = (sub_row <= j) if reverse else (sub_row >= j)
                    dlt = jnp.where(mask, bs - bs[j:j + 1], MASKED)
                    w = jnp.exp(dlt) * qs * ks[j:j + 1]
                    acc = acc + jnp.sum(w, axis=-1, keepdims=True) * vs[j:j + 1]
                pieces.append(acc)
            o = jnp.concatenate(pieces, axis=0)
            if final:
                o = o + ofw_ref[0, rows, sl]
                o = o * lax.rsqrt(jnp.mean(o * o, axis=-1, keepdims=True) + RMS_EPS) * nw_ref[...]
                o_ref[0, rows, sl] = (o * sg_ref[0, rows, sl].astype(F32)).astype(o_ref.dtype)
            else:
                o_ref[0, rows, sl] = o
        return carry

    lax.fori_loop(0, n_chunks, chunk_body, 0)


def _gla(lf, k, q, v, nct_blocks, reverse, extra=None):
    b, t, hw = lf.shape
    n_heads = hw // HG_HEAD_DIM
    blk = TOK_TILE
    nb = t // blk

    def blk_index(g):
        if not reverse:
            return g
        return jnp.where(g < nct_blocks, nct_blocks - 1 - g, nb - 1 - (g - nct_blocks))

    spec = pl.BlockSpec((1, blk, hw), lambda bi, g: (bi, blk_index(g), 0))
    in_specs = [spec, spec, spec, spec]
    args = [lf, k, q, v]
    final = extra is not None
    if final:
        ofw, sg, nw = extra
        in_specs += [spec, spec, pl.BlockSpec(nw.shape, lambda bi, g: (0, 0))]
        args += [ofw, sg, nw]
    return pl.pallas_call(
        functools.partial(_gla_kernel, reverse=reverse, n_chunks=blk // GLA_CHUNK, n_heads=n_heads, final=final),
        out_shape=jax.ShapeDtypeStruct((b, t, hw), BF16 if final else F32),
        grid=(b, nb),
        in_specs=in_specs,
        out_specs=spec,
        scratch_shapes=[pltpu.VMEM((n_heads, HG_HEAD_DIM, HG_HEAD_DIM), F32)],
        compiler_params=_cparams(("arbitrary", "arbitrary")),
        name="gla_bwd" if reverse else "gla_fwd",
    )(*args)


def _attn_kernel(qu_ref, qr_ref, k_ref, v_ref, o_ref, *, n_ctx, n_lat):
    is_ctx = pl.program_id(2) < n_ctx // ATT_TQ
    nt = (((1,), (1,)), ((), ()))

    def ctx_scores():
        return lax.dot_general(qu_ref[0], k_ref[0, 0:n_ctx, :], nt, preferred_element_type=F32)

    def finish(acc):
        o_ref[0] = (acc * (1.0 / acc[:, MLA_V:MLA_V + 1])).astype(o_ref.dtype)

    @pl.when(is_ctx)
    def _():
        s = ctx_scores()
        p = jnp.exp2(s - jnp.max(s, axis=-1, keepdims=True))
        finish(jnp.dot(p.astype(BF16), v_ref[0, 0:n_ctx, :], preferred_element_type=F32))

    @pl.when(jnp.logical_not(is_ctx))
    def _():
        qr = qr_ref[0]
        s_c = ctx_scores()
        m = jnp.max(s_c, axis=-1, keepdims=True)
        s_l = []
        for j in range(n_lat // ATT_TK):
            rows = slice(n_ctx + j * ATT_TK, n_ctx + (j + 1) * ATT_TK)
            s = lax.dot_general(qr, k_ref[0, rows, :], nt, preferred_element_type=F32)
            m = jnp.maximum(m, jnp.max(s, axis=-1, keepdims=True))
            s_l.append(s)
        acc = jnp.dot(jnp.exp2(s_c - m).astype(BF16), v_ref[0, 0:n_ctx, :], preferred_element_type=F32)
        for j, s in enumerate(s_l):
            rows = slice(n_ctx + j * ATT_TK, n_ctx + (j + 1) * ATT_TK)
            acc = acc + jnp.dot(jnp.exp2(s - m).astype(BF16), v_ref[0, rows, :], preferred_element_type=F32)
        finish(acc)


def _attention(qu, qr, kk, vv, n_ctx, n_heads):
    b, t, hp = qu.shape
    n_lat = t - n_ctx
    qspec = pl.BlockSpec((1, ATT_TQ, LANE), lambda bi, h, qi: (bi, qi, h))
    kspec = pl.BlockSpec((1, t, LANE), lambda bi, h, qi: (bi, 0, h))
    return pl.pallas_call(
        functools.partial(_attn_kernel, n_ctx=n_ctx, n_lat=n_lat),
        out_shape=jax.ShapeDtypeStruct((b, t, hp), BF16),
        grid=(b, n_heads, t // ATT_TQ),
        in_specs=[qspec, qspec, kspec, kspec],
        out_specs=qspec,
        compiler_params=_cparams(("arbitrary", "arbitrary", "arbitrary")),
        name="mla_attention",
    )(qu, qr, kk, vv)


def _postmix_kernel(x_ref, ohg_ref, omla_ref, mod_ref, wohg_ref, womla_ref, ln1w_ref, ln1b_ref,
                    rw_ref, rb_ref, sw1_ref, sw3_ref, sw2_ref,
                    x1_ref, h2p_ref, sh_ref, idx_ref, wts_ref, pos_ref, cnt_ref, cnt_sc,
                    *, d, alpha, n_experts):
    @pl.when((pl.program_id(0) == 0) & (pl.program_id(1) == 0))
    def _():
        cnt_sc[...] = jnp.zeros_like(cnt_sc)

    m = mod_ref[0]
    g_a = m[:, 2 * d:3 * d]
    sh_f = m[:, 3 * d:4 * d]
    sc_f = m[:, 4 * d:5 * d]
    mix = (jnp.dot(ohg_ref[0], wohg_ref[...], preferred_element_type=F32)
           + jnp.dot(omla_ref[0], womla_ref[...], preferred_element_type=F32))
    x1 = _normalize(alpha * x_ref[0] + g_a * mix) * ln1w_ref[...] + ln1b_ref[...]
    x1_ref[0] = x1
    h2 = _normalize(x1) * (1.0 + sc_f) + sh_f
    hw = d // 2
    h2p_ref[0] = _pack_bf16_pair(h2[:, 0:hw], h2[:, hw:d])
    h2b = h2.astype(BF16)

    a1 = jnp.dot(h2b, sw1_ref[...], preferred_element_type=F32)
    a3 = jnp.dot(h2b, sw3_ref[...], preferred_element_type=F32)
    hid = (_silu(a1) * a3).astype(BF16)
    sh_ref[0] = jnp.dot(hid, sw2_ref[...], preferred_element_type=F32).astype(sh_ref.dtype)

    logits = jnp.dot(h2, rw_ref[...], precision=HIGHEST, preferred_element_type=F32)
    scores = _sigmoid(logits)
    sel = scores + rb_ref[...]
    tm = scores.shape[0]
    lane = lax.broadcasted_iota(I32, (tm, n_experts), 1)
    masks, tops, idxs = [], [], []
    for _ in range(TOP_K):
        mx = jnp.max(sel, axis=-1, keepdims=True)
        ik = jnp.min(jnp.where(sel == mx, lane, n_experts), axis=-1, keepdims=True)
        oh = lane == ik
        masks.append(oh)
        idxs.append(ik)
        tops.append(jnp.sum(jnp.where(oh, scores, 0.0), axis=-1, keepdims=True))
        sel = jnp.where(oh, -jnp.inf, sel)
    tsum = tops[0]
    for tk in tops[1:]:
        tsum = tsum + tk
    inv = ROUTED_SCALE / tsum

    ohf = jnp.zeros((tm, n_experts), F32)
    for oh in masks:
        ohf = jnp.where(oh, 1.0, ohf)
    r = lax.broadcasted_iota(I32, (tm, tm), 0)
    cc = lax.broadcasted_iota(I32, (tm, tm), 1)
    strict = jnp.where(r > cc, 1.0, 0.0).astype(BF16)
    before = jnp.dot(strict, ohf.astype(BF16), preferred_element_type=F32) + cnt_sc[...]
    cnt_new = cnt_sc[...] + jnp.sum(ohf, axis=0, keepdims=True)
    cnt_sc[...] = cnt_new
    cnt_ref[...] = cnt_new.astype(I32)

    lane_k = lax.broadcasted_iota(I32, (tm, TOP_K), 1)
    idx_o = jnp.zeros((tm, TOP_K), I32)
    wts_o = jnp.zeros((tm, TOP_K), F32)
    pos_o = jnp.zeros((tm, TOP_K), I32)
    for kk in range(TOP_K):
        pk = jnp.sum(jnp.where(masks[kk], before, 0.0), axis=-1, keepdims=True)
        hit = lane_k == kk
        idx_o = jnp.where(hit, idxs[kk], idx_o)
        wts_o = jnp.where(hit, tops[kk] * inv, wts_o)
        pos_o = jnp.where(hit, pk.astype(I32), pos_o)
    idx_ref[0] = idx_o
    wts_ref[0] = wts_o
    pos_ref[0] = pos_o


def _postmix(xc, ohg, omla, mod3, layer, n_batch, nct, wohg, womla, ln1w, ln1b, rw, rb, sw1, sw3, sw2, alpha):
    b, t, d = xc.shape
    tm = TOK_TILE
    n_experts = rw.shape[1]
    full = lambda a: pl.BlockSpec(a.shape, lambda bi, ti: tuple(0 for _ in a.shape))
    tok = lambda w: pl.BlockSpec((1, tm, w), lambda bi, ti: (bi, ti, 0))
    out_shape = (
        jax.ShapeDtypeStruct((b, t, d), F32),
        jax.ShapeDtypeStruct((b, t, d // 2), U32),
        jax.ShapeDtypeStruct((b, t, d), BF16),
        jax.ShapeDtypeStruct((b, t, TOP_K), I32),
        jax.ShapeDtypeStruct((b, t, TOP_K), F32),
        jax.ShapeDtypeStruct((b, t, TOP_K), I32),
        jax.ShapeDtypeStruct((1, n_experts), I32),
    )
    return pl.pallas_call(
        functools.partial(_postmix_kernel, d=d, alpha=alpha, n_experts=n_experts),
        out_shape=out_shape,
        grid=(b, t // tm),
        in_specs=[
            tok(d), tok(ohg.shape[2]), tok(omla.shape[2]),
            pl.BlockSpec((1, 1, N_MOD * d), lambda bi, ti: (layer * 8 + jnp.where(ti < nct, n_batch, bi), 0, 0)),
            full(wohg), full(womla), full(ln1w), full(ln1b), full(rw), full(rb), full(sw1), full(sw3), full(sw2),
        ],
        out_specs=(tok(d), tok(d // 2), tok(d), tok(TOP_K), tok(TOP_K), tok(TOP_K),
                   pl.BlockSpec((1, n_experts), lambda bi, ti: (0, 0))),
        scratch_shapes=[pltpu.VMEM((1, n_experts), F32)],
        compiler_params=_cparams(("arbitrary", "arbitrary")),
        name="postmix_router",
    )(xc, ohg, omla, mod3, wohg, womla, ln1w, ln1b, rw, rb, sw1, sw3, sw2)


def _row_copy(src_ref, src_row, dst_ref, dst_row, sem):
    return pltpu.make_async_copy(src_ref.at[pl.ds(src_row, 1)], dst_ref.at[pl.ds(dst_row, 1)], sem)


def _dispatch_kernel(dest_ref, h_ref, xs_in_ref, xs_ref, sem):
    del xs_in_ref

    def issue(t, c):
        for kk in range(TOP_K):
            _row_copy(h_ref, t, xs_ref, dest_ref[0, 0, t * TOP_K + kk], sem).start()
        return c

    lax.fori_loop(0, DISPATCH_TILE, issue, 0, unroll=2)

    def drain(t, c):
        for kk in range(TOP_K):
            _row_copy(h_ref, 0, xs_ref, 0, sem).wait()
        return c

    lax.fori_loop(0, DISPATCH_TILE, drain, 0, unroll=2)


def _dispatch(dest, h2p, n_slots):
    n, w = h2p.shape
    steps = n // DISPATCH_TILE
    dest3 = dest.reshape(steps, 1, DISPATCH_TILE * TOP_K)
    xs0 = jnp.zeros((n_slots, w), U32)
    return pl.pallas_call(
        _dispatch_kernel,
        out_shape=jax.ShapeDtypeStruct((n_slots, w), U32),
        grid=(steps,),
        in_specs=[
            pl.BlockSpec((1, 1, DISPATCH_TILE * TOP_K), lambda i: (i, 0, 0), memory_space=pltpu.SMEM),
            pl.BlockSpec((DISPATCH_TILE, w), lambda i: (i, 0)),
            pl.BlockSpec(memory_space=pl.ANY),
        ],
        out_specs=pl.BlockSpec(memory_space=pl.ANY),
        scratch_shapes=[pltpu.SemaphoreType.DMA],
        input_output_aliases={2: 0},
        compiler_params=_cparams(("arbitrary",)),
        name="moe_dispatch",
    )(dest3, h2p, xs0)


def _expert_kernel(be_ref, nused_ref, xs_ref, w1_ref, w3_ref, w2_ref, ys_ref, w1b, w3b, w2b):
    i = pl.program_id(0)
    used = i < nused_ref[0]

    @pl.when(used)
    def _():
        changed = (i == 0) | (be_ref[i] != be_ref[jnp.maximum(i - 1, 0)])

        @pl.when(changed)
        def _():
            w1b[...] = w1_ref[0, 0].astype(BF16)
            w3b[...] = w3_ref[0, 0].astype(BF16)
            w2b[...] = w2_ref[0, 0].astype(BF16)

        lo, hi = _unpack_bf16_pair(xs_ref[...])
        x = jnp.concatenate([lo.astype(BF16), hi.astype(BF16)], axis=1)
        h1 = jnp.dot(x, w1b[...], preferred_element_type=F32)
        h3 = jnp.dot(x, w3b[...], preferred_element_type=F32)
        hid = (_silu(h1) * h3).astype(BF16)
        out = jnp.dot(hid, w2b[...], preferred_element_type=F32)
        half = out.shape[1] // 2
        ys_ref[...] = _pack_bf16_pair(out[:, 0:half], out[:, half:])

    @pl.when(jnp.logical_not(used))
    def _():
        ys_ref[...] = jnp.zeros_like(ys_ref)


def _experts(xs, block_expert, n_used, w1, w3, w2, layer):
    n_slots, w = xs.shape
    n_blocks = n_slots // ROW_BLOCK
    _, _, d, hid = w1.shape
    grid_spec = pltpu.PrefetchScalarGridSpec(
        num_scalar_prefetch=2,
        grid=(n_blocks,),
        in_specs=[
            pl.BlockSpec((ROW_BLOCK, w), lambda i, be, nu: (i, 0)),
            pl.BlockSpec((1, 1, d, hid), lambda i, be, nu: (layer, be[i], 0, 0)),
            pl.BlockSpec((1, 1, d, hid), lambda i, be, nu: (layer, be[i], 0, 0)),
            pl.BlockSpec((1, 1, hid, d), lambda i, be, nu: (layer, be[i], 0, 0)),
        ],
        out_specs=pl.BlockSpec((ROW_BLOCK, w), lambda i, be, nu: (i, 0)),
        scratch_shapes=[pltpu.VMEM((d, hid), BF16), pltpu.VMEM((d, hid), BF16), pltpu.VMEM((hid, d), BF16)],
    )
    return pl.pallas_call(
        _expert_kernel,
        out_shape=jax.ShapeDtypeStruct((n_slots, w), U32),
        grid_spec=grid_spec,
        compiler_params=_cparams(("arbitrary",)),
        name="moe_experts",
    )(block_expert, n_used, xs, w1, w3, w2)


def _combine_kernel(dest_ref, wts_ref, ys_ref, sh_ref, x1_ref, mod_ref, lnw_ref, lnb_ref, o_ref, buf, sem,
                    *, d, alpha):
    def issue(t, c):
        for kk in range(TOP_K):
            pltpu.make_async_copy(ys_ref.at[pl.ds(dest_ref[0, 0, t * TOP_K + kk], 1)],
                                  buf.at[kk, pl.ds(t, 1)], sem).start()
        return c

    lax.fori_loop(0, COMBINE_TILE, issue, 0, unroll=2)

    def drain(t, c):
        for kk in range(TOP_K):
            pltpu.make_async_copy(ys_ref.at[pl.ds(0, 1)], buf.at[0, pl.ds(0, 1)], sem).wait()
        return c

    lax.fori_loop(0, COMBINE_TILE, drain, 0, unroll=2)

    hw = d // 2
    sh = sh_ref[0].astype(F32)
    acc_lo = sh[:, 0:hw]
    acc_hi = sh[:, hw:d]
    wts = wts_ref[0]
    for kk in range(TOP_K):
        lo, hi = _unpack_bf16_pair(buf[kk])
        wk = wts[:, kk:kk + 1]
        acc_lo = acc_lo + wk * lo
        acc_hi = acc_hi + wk * hi
    ff = jnp.concatenate([acc_lo, acc_hi], axis=1)
    g_f = mod_ref[0][:, 5 * d:6 * d]
    o_ref[0] = _normalize(alpha * x1_ref[0] + g_f * ff) * lnw_ref[...] + lnb_ref[...]


def _combine(dest, wts, ys, shared, x1, mod3, layer, n_batch, nct_tiles, lnw, lnb, alpha):
    b, t, d = x1.shape
    tm = COMBINE_TILE
    tpb = t // tm
    dest3 = dest.reshape(b * tpb, 1, tm * TOP_K)
    tok = lambda w: pl.BlockSpec((1, tm, w), lambda bi, ti: (bi, ti, 0))
    full = lambda a: pl.BlockSpec(a.shape, lambda bi, ti: tuple(0 for _ in a.shape))
    return pl.pallas_call(
        functools.partial(_combine_kernel, d=d, alpha=alpha),
        out_shape=jax.ShapeDtypeStruct((b, t, d), F32),
        grid=(b, tpb),
        in_specs=[
            pl.BlockSpec((1, 1, tm * TOP_K), lambda bi, ti: (bi * tpb + ti, 0, 0), memory_space=pltpu.SMEM),
            tok(TOP_K),
            pl.BlockSpec(memory_space=pl.ANY),
            tok(d), tok(d),
            pl.BlockSpec((1, 1, N_MOD * d), lambda bi, ti: (layer * 8 + jnp.where(ti < nct_tiles, n_batch, bi), 0, 0)),
            full(lnw), full(lnb),
        ],
        out_specs=tok(d),
        scratch_shapes=[pltpu.VMEM((TOP_K, tm, d // 2), U32), pltpu.SemaphoreType.DMA],
        compiler_params=_cparams(("arbitrary", "arbitrary")),
        name="moe_combine",
    )(dest3, wts, ys, shared, x1, mod3, lnw, lnb)


def _rope_tables(n_ctx, n_lat):
    pos = jnp.arange(n_lat, dtype=I32)
    rowp = (pos // GRID_W).astype(F32)
    colp = (pos % GRID_W).astype(F32)
    n_freq = MLA_ROPE // 4
    inv = ROPE_BASE ** (-jnp.arange(n_freq, dtype=F32) / n_freq)
    ang = jnp.concatenate([rowp[:, None] * inv, colp[:, None] * inv], axis=-1)
    ang = jnp.concatenate([jnp.zeros((n_ctx, MLA_ROPE // 2), F32), ang], axis=0)
    t = n_ctx + n_lat
    ones = jnp.ones((t, MLA_NOPE), F32)
    zeros_tail = jnp.zeros((t, LANE - MLA_NOPE - MLA_ROPE), F32)
    cos_t = jnp.concatenate([ones, jnp.cos(ang), jnp.cos(ang), zeros_tail], axis=1)
    sin_t = jnp.concatenate([jnp.zeros((t, MLA_NOPE), F32), jnp.sin(ang), jnp.sin(ang), zeros_tail], axis=1)
    return cos_t, sin_t


def _rope_swap(w):
    half = MLA_ROPE // 2
    return jnp.concatenate([-w[..., half:], w[..., :half]], axis=-1)


def _prep_layer_weights(w_in_l, w_uq_l, w_ukv_l, w_out_l, hw, q_lora, kv_lora, n_heads):
    d = w_in_l.shape[0]
    base = 5 * hw + q_lora + kv_lora
    kpe_w = w_in_l[:, base:base + MLA_ROPE]
    z_nope = jnp.zeros((d, MLA_NOPE), F32)
    z_tail = jnp.zeros((d, LANE - MLA_NOPE - MLA_ROPE), F32)
    w_ext = jnp.concatenate([w_in_l[:, :base], z_nope, kpe_w, z_tail, z_nope, _rope_swap(kpe_w), z_tail],
                            axis=1).astype(BF16)
    wq = w_uq_l.reshape(q_lora, n_heads, MLA_NOPE + MLA_ROPE)
    zq = jnp.zeros((q_lora, n_heads, LANE - MLA_NOPE - MLA_ROPE), F32)
    wqa = jnp.concatenate([wq, zq], axis=-1).reshape(q_lora, n_heads * LANE).astype(BF16)
    wqb = jnp.concatenate([jnp.zeros((q_lora, n_heads, MLA_NOPE), F32), _rope_swap(wq[..., MLA_NOPE:]), zq],
                          axis=-1).reshape(q_lora, n_heads * LANE).astype(BF16)
    wkv = w_ukv_l.reshape(kv_lora, n_heads, MLA_NOPE + MLA_V)
    wk = jnp.concatenate([wkv[..., :MLA_NOPE], jnp.zeros((kv_lora, n_heads, LANE - MLA_NOPE), F32)],
                         axis=-1).reshape(kv_lora, n_heads * LANE).astype(BF16)
    wv = jnp.concatenate([wkv[..., MLA_NOPE:], jnp.zeros((kv_lora, n_heads, LANE - MLA_V), F32)],
                         axis=-1).reshape(kv_lora, n_heads * LANE).astype(BF16)
    wohg = w_out_l[:hw].astype(BF16)
    wom = w_out_l[hw:].reshape(n_heads, MLA_V, d)
    womla = jnp.concatenate([wom, jnp.zeros((n_heads, LANE - MLA_V, d), F32)], axis=1)
    womla = womla.reshape(n_heads * LANE, d).astype(BF16)
    return w_ext, wqa, wqb, wk, wv, wohg, womla


def kernel(x, c, ctx, c_ctx, w_mod, b_mod, w_in, hg_lb, hg_norm_w, q_norm_w, w_uq, kv_norm_w, w_ukv, w_out,
           ln1_w, ln1_b, router_w, router_bias, moe_w1, moe_w3, moe_w2, shared_w1, shared_w3, shared_w2,
           ln2_w, ln2_b):
    bsz, n_lat, d = x.shape
    n_ctx = ctx.shape[1]
    depth = w_mod.shape[0]
    t = n_ctx + n_lat
    hw = d // 2
    q_lora = w_uq.shape[1]
    kv_lora = w_ukv.shape[1]
    n_heads = (d - hw) // MLA_V
    n_experts = router_w.shape[2]
    alpha = float((2 * depth) ** 0.25)
    assert n_ctx % TOK_TILE == 0 and n_lat % ATT_TK == 0 and bsz < 8
    assert n_ctx % ATT_TQ == 0 and (bsz * t) % DISPATCH_TILE == 0 and t % COMBINE_TILE == 0

    lb_all = jnp.cumsum(jax.nn.softmax(hg_lb.astype(F32), axis=0), axis=0)
    lb_all = lb_all - lb_all[:1]

    c_rows = jnp.concatenate([c, c_ctx[None, :], jnp.zeros((8 - bsz - 1, d), F32)], axis=0)
    mod = _modulation(c_rows, w_mod, b_mod)
    mod3 = mod.reshape(depth * 8, 1, N_MOD * d)

    cos_t, sin_t = _rope_tables(n_ctx, n_lat)
    xc = jnp.concatenate([ctx, x], axis=1)
    nct = n_ctx // TOK_TILE

    n_tok = bsz * t
    n_assign = n_tok * TOP_K
    n_blocks = -(-(n_assign + n_experts * (ROW_BLOCK - 1)) // ROW_BLOCK)
    n_slots = n_blocks * ROW_BLOCK

    for l in range(depth):
        w_ext, wqa, wqb, wk, wv, wohg, womla = _prep_layer_weights(
            w_in[l], w_uq[l], w_ukv[l], w_out[l], hw, q_lora, kv_lora, n_heads)
        (lf_fw, lf_bw, k_fw, k_bw, q_hg, v_hg, sg, q_rot, q_unrot, kk, vv) = _inproj(
            xc, mod3, l, bsz, nct, w_ext, lb_all[l][None, :], q_norm_w[l][None, :], kv_norm_w[l][None, :],
            wqa, wqb, wk, wv, cos_t, sin_t, n_heads)
        o_fw = _gla(lf_fw, k_fw, q_hg, v_hg, nct, reverse=False)
        o_hg = _gla(lf_bw, k_bw, q_hg, v_hg, nct, reverse=True, extra=(o_fw, sg, hg_norm_w[l][None, :]))
        o_mla = _attention(q_unrot, q_rot, kk, vv, n_ctx, n_heads)
        x1, h2p, shared, idx, wts, pos, counts = _postmix(
            xc, o_hg, o_mla, mod3, l, bsz, nct, wohg, womla, ln1_w[l][None, :], ln1_b[l][None, :],
            router_w[l], router_bias[l][None, :], shared_w1[l].astype(BF16), shared_w3[l].astype(BF16),
            shared_w2[l].astype(BF16), alpha)

        counts = counts[0]
        padded = (counts + ROW_BLOCK - 1) // ROW_BLOCK * ROW_BLOCK
        pad_end = jnp.cumsum(padded)
        pad_start = pad_end - padded
        dest = (pad_start[idx.reshape(n_tok, TOP_K)] + pos.reshape(n_tok, TOP_K)).astype(I32)
        block_first_row = jnp.arange(n_blocks, dtype=I32) * ROW_BLOCK
        block_expert = jnp.minimum(
            jnp.sum((pad_end[None, :] <= block_first_row[:, None]).astype(I32), axis=1), n_experts - 1)
        n_used = (pad_end[-1:] // ROW_BLOCK).astype(I32)

        xs = _dispatch(dest, h2p.reshape(n_tok, hw), n_slots)
        ys = _experts(xs, block_expert, n_used, moe_w1, moe_w3, moe_w2, l)
        xc = _combine(dest, wts, ys, shared, x1, mod3, l, bsz, n_ctx // COMBINE_TILE,
                      ln2_w[l][None, :], ln2_b[l][None, :], alpha)
    return xc[:, n_ctx:, :]
```

```python
import functools

import jax
import jax.numpy as jnp
from jax import lax
from jax.experimental import pallas as pl
from jax.experimental.pallas import tpu as pltpu

F32 = jnp.float32
BF16 = jnp.bfloat16
U32 = jnp.uint32
I32 = jnp.int32
HIGHEST = lax.Precision.HIGHEST

HG_HEAD_DIM = 128
MLA_V = 64
MLA_NOPE = 64
MLA_ROPE = 32
GRID_W = 64
ROPE_BASE = 10000.0
TOP_K = 8
ROUTED_SCALE = 2.5
N_MOD = 6
LN_EPS = 1e-6
RMS_EPS = 1e-6

LANE = 128
TOK_TILE = 256
GLA_CHUNK = 128
GLA_SUB = 32
GLA_SAFE_EXPONENT = 60.0
ATT_TQ = 256
ATT_TK = 1024
LOG2_E = 1.4426950408889634
ROW_BLOCK = 512
DISPATCH_TILE = 256
COMBINE_TILE = 128
VMEM_LIMIT = 56 * 1024 * 1024
MASKED = -1e30


def _cparams(sem):
    return pltpu.CompilerParams(dimension_semantics=sem, vmem_limit_bytes=VMEM_LIMIT)


def _sigmoid(z):
    return 1.0 / (1.0 + jnp.exp(-z))


def _silu(z):
    return z * _sigmoid(z)


def _normalize(x):
    mu = jnp.mean(x, axis=-1, keepdims=True)
    xc = x - mu
    var = jnp.mean(xc * xc, axis=-1, keepdims=True)
    return xc * lax.rsqrt(var + LN_EPS)


def _pack_bf16_pair(lo, hi):
    lo_u = lax.bitcast_convert_type(lo.astype(BF16).astype(F32), U32) >> 16
    hi_u = lax.bitcast_convert_type(hi.astype(BF16).astype(F32), U32) & jnp.uint32(0xFFFF0000)
    return hi_u | lo_u


def _unpack_bf16_pair(u):
    lo = lax.bitcast_convert_type(u << 16, F32)
    hi = lax.bitcast_convert_type(u & jnp.uint32(0xFFFF0000), F32)
    return lo, hi


def _mod_kernel(c_ref, w_ref, b_ref, o_ref):
    c = c_ref[...]
    o_ref[0] = jnp.dot(_silu(c), w_ref[0], precision=HIGHEST, preferred_element_type=F32) + b_ref[0]


def _modulation(c_rows, w_mod, b_mod):
    depth, d, n = w_mod.shape
    tn = 1536
    return pl.pallas_call(
        _mod_kernel,
        out_shape=jax.ShapeDtypeStruct((depth, 8, n), F32),
        grid=(depth, n // tn),
        in_specs=[
            pl.BlockSpec((8, d), lambda l, j: (0, 0)),
            pl.BlockSpec((1, d, tn), lambda l, j: (l, 0, j)),
            pl.BlockSpec((1, 1, tn), lambda l, j: (l, 0, j)),
        ],
        out_specs=pl.BlockSpec((1, 8, tn), lambda l, j: (l, 0, j)),
        compiler_params=_cparams(("arbitrary", "arbitrary")),
        name="modulation",
    )(c_rows, w_mod, b_mod.reshape(depth, 1, n))


def _inproj_kernel(x_ref, mod_ref, w_ref, lb_ref, qnw_ref, kvnw_ref, wqa_ref, wqb_ref, wk_ref, wv_ref,
                   cos_ref, sin_ref,
                   lff_ref, lfb_ref, kf_ref, kb_ref, q_ref, v_ref, sg_ref, qr_ref, qu_ref, kk_ref, vv_ref,
                   *, d, hw, q_lora, kv_lora, n_heads):
    x = x_ref[0]
    m = mod_ref[0]
    shift = m[:, 0:d]
    scale = m[:, d:2 * d]
    h = (_normalize(x) * (1.0 + scale) + shift).astype(BF16)
    proj = jnp.dot(h, w_ref[...], preferred_element_type=F32)

    lb = lb_ref[...]

    def forget(z, lbd):
        f = lbd + (1.0 - lbd) * _sigmoid(z)
        return jnp.log(f), 1.0 - f

    lf, kd = forget(proj[:, 0:hw], lb[:, 0:hw])
    lff_ref[0] = lf
    kf_ref[0] = kd.astype(BF16)
    lf, kd = forget(proj[:, hw:2 * hw], lb[:, hw:2 * hw])
    lfb_ref[0] = lf
    kb_ref[0] = kd.astype(BF16)
    v_ref[0] = proj[:, 2 * hw:3 * hw].astype(BF16)
    q_ref[0] = (_silu(proj[:, 3 * hw:4 * hw]) * (HG_HEAD_DIM ** -0.5)).astype(BF16)
    sg_ref[0] = _silu(proj[:, 4 * hw:5 * hw]).astype(BF16)

    o = 5 * hw
    cq = proj[:, o:o + q_lora]
    ckv = proj[:, o + q_lora:o + q_lora + kv_lora]
    o2 = o + q_lora + kv_lora
    kpe_a = proj[:, o2:o2 + LANE]
    kpe_b = proj[:, o2 + LANE:o2 + 2 * LANE]

    cqn = (cq * lax.rsqrt(jnp.mean(cq * cq, axis=-1, keepdims=True) + RMS_EPS) * qnw_ref[...]).astype(BF16)
    ckvn = (ckv * lax.rsqrt(jnp.mean(ckv * ckv, axis=-1, keepdims=True) + RMS_EPS) * kvnw_ref[...]).astype(BF16)

    cos = cos_ref[...]
    sin = sin_ref[...]
    cos_h = jnp.concatenate([cos] * n_heads, axis=1)
    sin_h = jnp.concatenate([sin] * n_heads, axis=1)
    att_scale = (MLA_NOPE + MLA_ROPE) ** -0.5 * LOG2_E
    qa = jnp.dot(cqn, wqa_ref[...], preferred_element_type=F32)
    qb = jnp.dot(cqn, wqb_ref[...], preferred_element_type=F32)
    qu_ref[0] = (qa * att_scale).astype(BF16)
    qr_ref[0] = ((qa * cos_h + qb * sin_h) * att_scale).astype(BF16)

    kr = kpe_a * cos + kpe_b * sin
    kk = jnp.dot(ckvn, wk_ref[...], preferred_element_type=F32) + jnp.concatenate([kr] * n_heads, axis=1)
    kk_ref[0] = kk.astype(BF16)
    vv = jnp.dot(ckvn, wv_ref[...], preferred_element_type=F32)
    lane = lax.broadcasted_iota(I32, vv.shape, 1)
    vv = jnp.where((lane % LANE) == MLA_V, 1.0, vv)
    vv_ref[0] = vv.astype(BF16)


def _inproj(xc, mod3, layer, n_batch, nct, w_ext, lb, qnw, kvnw, wqa, wqb, wk, wv, cos_t, sin_t, n_heads):
    b, t, d = xc.shape
    hw = d // 2
    q_lora = wqa.shape[0]
    kv_lora = wk.shape[0]
    hp = n_heads * LANE
    tm = TOK_TILE
    full = lambda shape: pl.BlockSpec(shape, lambda bi, ti: tuple(0 for _ in shape))
    tok = lambda w: pl.BlockSpec((1, tm, w), lambda bi, ti: (bi, ti, 0))
    out_shape = (
        jax.ShapeDtypeStruct((b, t, hw), F32), jax.ShapeDtypeStruct((b, t, hw), F32),
        jax.ShapeDtypeStruct((b, t, hw), BF16), jax.ShapeDtypeStruct((b, t, hw), BF16),
        jax.ShapeDtypeStruct((b, t, hw), BF16), jax.ShapeDtypeStruct((b, t, hw), BF16),
        jax.ShapeDtypeStruct((b, t, hw), BF16),
        jax.ShapeDtypeStruct((b, t, hp), BF16), jax.ShapeDtypeStruct((b, t, hp), BF16),
        jax.ShapeDtypeStruct((b, t, hp), BF16), jax.ShapeDtypeStruct((b, t, hp), BF16),
    )
    return pl.pallas_call(
        functools.partial(_inproj_kernel, d=d, hw=hw, q_lora=q_lora, kv_lora=kv_lora, n_heads=n_heads),
        out_shape=out_shape,
        grid=(b, t // tm),
        in_specs=[
            tok(d),
            pl.BlockSpec((1, 1, N_MOD * d), lambda bi, ti: (layer * 8 + jnp.where(ti < nct, n_batch, bi), 0, 0)),
            full(w_ext.shape), full(lb.shape), full(qnw.shape), full(kvnw.shape),
            full(wqa.shape), full(wqb.shape), full(wk.shape), full(wv.shape),
            pl.BlockSpec((tm, LANE), lambda bi, ti: (ti, 0)),
            pl.BlockSpec((tm, LANE), lambda bi, ti: (ti, 0)),
        ],
        out_specs=tuple([tok(hw)] * 7 + [tok(hp)] * 4),
        compiler_params=_cparams(("arbitrary", "arbitrary")),
        name="inproj",
    )(xc, mod3, w_ext, lb, qnw, kvnw, wqa, wqb, wk, wv, cos_t, sin_t)


def _gla_kernel(*refs, reverse, n_chunks, n_heads, final):
    if final:
        lf_ref, k_ref, q_ref, v_ref, ofw_ref, sg_ref, nw_ref, o_ref, st_ref = refs
    else:
        lf_ref, k_ref, q_ref, v_ref, o_ref, st_ref = refs
    c, s = GLA_CHUNK, GLA_SUB
    n_sub = c // s

    @pl.when(pl.program_id(1) == 0)
    def _():
        st_ref[...] = jnp.zeros_like(st_ref)

    row = lax.broadcasted_iota(I32, (c, c), 0)
    col = lax.broadcasted_iota(I32, (c, c), 1)
    tri = jnp.where((row <= col) if reverse else (row >= col), 1.0, 0.0).astype(F32)
    sub_row = lax.broadcasted_iota(I32, (s, HG_HEAD_DIM), 0)
    ones = jnp.ones((HG_HEAD_DIM, HG_HEAD_DIM), BF16)

    nt = (((1,), (1,)), ((), ()))

    def sub_geometry(si):
        rs = slice(si * s, (si + 1) * s)
        if reverse:
            return rs, (slice((si + 1) * s, c), (si + 1) * s) if si < n_sub - 1 else (None, None)
        return rs, (slice(0, si * s), si * s - 1) if si > 0 else (None, None)

    def chunk_body(i, carry):
        ci = (n_chunks - 1 - i) if reverse else i
        r0 = pl.multiple_of(ci * c, c)
        rows = pl.ds(r0, c)
        bc_all = jnp.dot(tri, lf_ref[0, rows, :], precision=HIGHEST, preferred_element_type=F32)
        worst = None
        for si in range(n_sub):
            rs, (_, brow) = sub_geometry(si)
            far = bc_all[rs.start:rs.start + 1] if reverse else bc_all[rs.stop - 1:rs.stop]
            d = far if brow is None else far - bc_all[brow:brow + 1]
            worst = d if worst is None else jnp.minimum(worst, d)
        bounded = jnp.min(worst) > -GLA_SAFE_EXPONENT

        def load(h):
            sl = slice(h * HG_HEAD_DIM, (h + 1) * HG_HEAD_DIM)
            return (sl, bc_all[:, sl], k_ref[0, rows, sl].astype(F32), q_ref[0, rows, sl].astype(F32),
                    v_ref[0, rows, sl].astype(F32))

        def store(sl, o):
            if final:
                o = o + ofw_ref[0, rows, sl]
                o = o * lax.rsqrt(jnp.mean(o * o, axis=-1, keepdims=True) + RMS_EPS) * nw_ref[...]
                o_ref[0, rows, sl] = (o * sg_ref[0, rows, sl].astype(F32)).astype(o_ref.dtype)
            else:
                o_ref[0, rows, sl] = o

        def intra(h, fast):
            sl, bc, k, q, v = load(h)
            qhat = (q * jnp.exp(bc)).astype(BF16)
            o_state = lax.dot_general(qhat, st_ref[h].astype(BF16), nt, preferred_element_type=F32)
            pieces = []
            for si in range(n_sub):
                rs, (src, brow) = sub_geometry(si)
                bs, qs, ks, vs = bc[rs], q[rs], k[rs], v[rs]
                acc = o_state[rs]
                beta = jnp.zeros((1, HG_HEAD_DIM), F32) if brow is None else bc[brow:brow + 1]
                if fast:
                    src = slice(rs.start, c) if reverse else slice(0, rs.stop)
                if src is not None:
                    qi = (qs * jnp.exp(bs - beta)).astype(BF16)
                    ksrc = (k[src] * jnp.exp(beta - bc[src])).astype(BF16)
                    a = lax.dot_general(qi, ksrc, nt, preferred_element_type=F32)
                    if fast:
                        n_src = src.stop - src.start
                        r_i = lax.broadcasted_iota(I32, (s, n_src), 0)
                        c_i = lax.broadcasted_iota(I32, (s, n_src), 1)
                        keep = (c_i >= r_i) if reverse else (c_i <= r_i + rs.start)
                        a = jnp.where(keep, a, 0.0)
                    acc = acc + jnp.dot(a.astype(BF16), v[src].astype(BF16), preferred_element_type=F32)
                if not fast:
                    ws = []
                    for j in range(s):
                        mask = (sub_row <= j) if reverse else (sub_row >= j)
                        dlt = jnp.where(mask, bs - bs[j:j + 1], MASKED)
                        ws.append((jnp.exp(dlt) * qs * ks[j:j + 1]).astype(BF16))
                    sums = jnp.dot(jnp.concatenate(ws, axis=0), ones, preferred_element_type=F32)
                    for j in range(s):
                        acc = acc + sums[j * s:(j + 1) * s] * vs[j:j + 1]
                pieces.append(acc)
            store(sl, jnp.concatenate(pieces, axis=0))

        @pl.when(bounded)
        def _():
            for h in range(n_heads):
                intra(h, True)

        @pl.when(jnp.logical_not(bounded))
        def _():
            for h in range(n_heads):
                intra(h, False)

        for h in range(n_heads):
            sl, bc, k, _, v = load(h)
            tot = bc[0:1] if reverse else bc[c - 1:c]
            khat = (k * jnp.exp(tot - bc)).astype(BF16)
            st_ref[h] = st_ref[h] * jnp.exp(tot) + jnp.dot(v.T.astype(BF16), khat, preferred_element_type=F32)
        return carry

    lax.fori_loop(0, n_chunks, chunk_body, 0)


def _gla(lf, k, q, v, nct_blocks, reverse, extra=None):
    b, t, hw = lf.shape
    n_heads = hw // HG_HEAD_DIM
    blk = TOK_TILE
    nb = t // blk

    def blk_index(g):
        if not reverse:
            return g
        return jnp.where(g < nct_blocks, nct_blocks - 1 - g, nb - 1 - (g - nct_blocks))

    spec = pl.BlockSpec((1, blk, hw), lambda bi, g: (bi, blk_index(g), 0))
    in_specs = [spec, spec, spec, spec]
    args = [lf, k, q, v]
    final = extra is not None
    if final:
        ofw, sg, nw = extra
        in_specs += [spec, spec, pl.BlockSpec(nw.shape, lambda bi, g: (0, 0))]
        args += [ofw, sg, nw]
    return pl.pallas_call(
        functools.partial(_gla_kernel, reverse=reverse, n_chunks=blk // GLA_CHUNK, n_heads=n_heads, final=final),
        out_shape=jax.ShapeDtypeStruct((b, t, hw), BF16 if final else F32),
        grid=(b, nb),
        in_specs=in_specs,
        out_specs=spec,
        scratch_shapes=[pltpu.VMEM((n_heads, HG_HEAD_DIM, HG_HEAD_DIM), F32)],
        compiler_params=_cparams(("arbitrary", "arbitrary")),
        name="gla_bwd" if reverse else "gla_fwd",
    )(*args)


def _attn_kernel(qu_ref, qr_ref, k_ref, v_ref, o_ref, s_sc, *, n_ctx, n_lat):
    is_ctx = pl.program_id(2) < n_ctx // ATT_TQ
    nt = (((1,), (1,)), ((), ()))
    n_chunks = n_lat // ATT_TK

    def ctx_scores():
        return lax.dot_general(qu_ref[0], k_ref[0, 0:n_ctx, :], nt, preferred_element_type=F32)

    def finish(acc):
        o_ref[0] = (acc * (1.0 / acc[:, MLA_V:MLA_V + 1])).astype(o_ref.dtype)

    def lane_tile_max(s, m):
        for c in range(s.shape[1] // LANE):
            t = s[:, c * LANE:(c + 1) * LANE]
            m = t if m is None else jnp.maximum(m, t)
        return m

    @pl.when(is_ctx)
    def _():
        s = ctx_scores()
        p = jnp.exp2(s - jnp.max(s, axis=-1, keepdims=True))
        finish(jnp.dot(p.astype(BF16), v_ref[0, 0:n_ctx, :], preferred_element_type=F32))

    @pl.when(jnp.logical_not(is_ctx))
    def _():
        qr = qr_ref[0]
        s_c = ctx_scores()
        m_t = lane_tile_max(s_c, None)
        for j in range(n_chunks):
            rows = slice(n_ctx + j * ATT_TK, n_ctx + (j + 1) * ATT_TK)
            s = lax.dot_general(qr, k_ref[0, rows, :], nt, preferred_element_type=F32)
            s_sc[j] = s
            m_t = lane_tile_max(s, m_t)
        m = jnp.max(m_t, axis=-1, keepdims=True)
        acc = jnp.dot(jnp.exp2(s_c - m).astype(BF16), v_ref[0, 0:n_ctx, :], preferred_element_type=F32)

        def pv(j, acc):
            r0 = pl.multiple_of(n_ctx + j * ATT_TK, ATT_TQ)
            p = jnp.exp2(s_sc[j] - m).astype(BF16)
            return acc + jnp.dot(p, v_ref[0, pl.ds(r0, ATT_TK), :], preferred_element_type=F32)

        finish(lax.fori_loop(0, n_chunks, pv, acc, unroll=2))


def _attention(qu, qr, kk, vv, n_ctx, n_heads):
    b, t, hp = qu.shape
    n_lat = t - n_ctx
    qspec = pl.BlockSpec((1, ATT_TQ, LANE), lambda bi, h, qi: (bi, qi, h))
    kspec = pl.BlockSpec((1, t, LANE), lambda bi, h, qi: (bi, 0, h))
    return pl.pallas_call(
        functools.partial(_attn_kernel, n_ctx=n_ctx, n_lat=n_lat),
        out_shape=jax.ShapeDtypeStruct((b, t, hp), BF16),
        grid=(b, n_heads, t // ATT_TQ),
        in_specs=[qspec, qspec, kspec, kspec],
        out_specs=qspec,
        scratch_shapes=[pltpu.VMEM((n_lat // ATT_TK, ATT_TQ, ATT_TK), F32)],
        compiler_params=_cparams(("arbitrary", "arbitrary", "arbitrary")),
        name="mla_attention",
    )(qu, qr, kk, vv)


def _postmix_kernel(x_ref, ohg_ref, omla_ref, mod_ref, wohg_ref, womla_ref, ln1w_ref, ln1b_ref,
                    rw_ref, rb_ref, sw1_ref, sw3_ref, sw2_ref,
                    x1_ref, h2p_ref, sh_ref, idx_ref, wts_ref, pos_ref, cnt_ref, cnt_sc,
                    *, d, alpha, n_experts):
    @pl.when((pl.program_id(0) == 0) & (pl.program_id(1) == 0))
    def _():
        cnt_sc[...] = jnp.zeros_like(cnt_sc)

    m = mod_ref[0]
    g_a = m[:, 2 * d:3 * d]
    sh_f = m[:, 3 * d:4 * d]
    sc_f = m[:, 4 * d:5 * d]
    mix = (jnp.dot(ohg_ref[0], wohg_ref[...], preferred_element_type=F32)
           + jnp.dot(omla_ref[0], womla_ref[...], preferred_element_type=F32))
    x1 = _normalize(alpha * x_ref[0] + g_a * mix) * ln1w_ref[...] + ln1b_ref[...]
    x1_ref[0] = x1
    h2 = _normalize(x1) * (1.0 + sc_f) + sh_f
    hw = d // 2
    h2p_ref[0] = _pack_bf16_pair(h2[:, 0:hw], h2[:, hw:d])
    h2b = h2.astype(BF16)

    a1 = jnp.dot(h2b, sw1_ref[...], preferred_element_type=F32)
    a3 = jnp.dot(h2b, sw3_ref[...], preferred_element_type=F32)
    hid = (_silu(a1) * a3).astype(BF16)
    sh_ref[0] = jnp.dot(hid, sw2_ref[...], preferred_element_type=F32).astype(sh_ref.dtype)

    logits = jnp.dot(h2, rw_ref[...], precision=HIGHEST, preferred_element_type=F32)
    scores = _sigmoid(logits)
    sel = scores + rb_ref[...]
    tm = scores.shape[0]
    lane = lax.broadcasted_iota(I32, (tm, n_experts), 1)
    masks, tops, idxs = [], [], []
    for _ in range(TOP_K):
        mx = jnp.max(sel, axis=-1, keepdims=True)
        ik = jnp.min(jnp.where(sel == mx, lane, n_experts), axis=-1, keepdims=True)
        oh = lane == ik
        masks.append(oh)
        idxs.append(ik)
        tops.append(jnp.sum(jnp.where(oh, scores, 0.0), axis=-1, keepdims=True))
        sel = jnp.where(oh, -jnp.inf, sel)
    tsum = tops[0]
    for tk in tops[1:]:
        tsum = tsum + tk
    inv = ROUTED_SCALE / tsum

    ohf = jnp.zeros((tm, n_experts), F32)
    for oh in masks:
        ohf = jnp.where(oh, 1.0, ohf)
    r = lax.broadcasted_iota(I32, (tm, tm), 0)
    cc = lax.broadcasted_iota(I32, (tm, tm), 1)
    strict = jnp.where(r > cc, 1.0, 0.0).astype(BF16)
    before = jnp.dot(strict, ohf.astype(BF16), preferred_element_type=F32) + cnt_sc[...]
    cnt_new = cnt_sc[...] + jnp.sum(ohf, axis=0, keepdims=True)
    cnt_sc[...] = cnt_new
    cnt_ref[...] = cnt_new.astype(I32)

    lane_k = lax.broadcasted_iota(I32, (tm, TOP_K), 1)
    idx_o = jnp.zeros((tm, TOP_K), I32)
    wts_o = jnp.zeros((tm, TOP_K), F32)
    pos_o = jnp.zeros((tm, TOP_K), I32)
    for kk in range(TOP_K):
        pk = jnp.sum(jnp.where(masks[kk], before, 0.0), axis=-1, keepdims=True)
        hit = lane_k == kk
        idx_o = jnp.where(hit, idxs[kk], idx_o)
        wts_o = jnp.where(hit, tops[kk] * inv, wts_o)
        pos_o = jnp.where(hit, pk.astype(I32), pos_o)
    idx_ref[0] = idx_o
    wts_ref[0] = wts_o
    pos_ref[0] = pos_o


def _postmix(xc, ohg, omla, mod3, layer, n_batch, nct, wohg, womla, ln1w, ln1b, rw, rb, sw1, sw3, sw2, alpha):
    b, t, d = xc.shape
    tm = TOK_TILE
    n_experts = rw.shape[1]
    full = lambda a: pl.BlockSpec(a.shape, lambda bi, ti: tuple(0 for _ in a.shape))
    tok = lambda w: pl.BlockSpec((1, tm, w), lambda bi, ti: (bi, ti, 0))
    out_shape = (
        jax.ShapeDtypeStruct((b, t, d), F32),
        jax.ShapeDtypeStruct((b, t, d // 2), U32),
        jax.ShapeDtypeStruct((b, t, d), BF16),
        jax.ShapeDtypeStruct((b, t, TOP_K), I32),
        jax.ShapeDtypeStruct((b, t, TOP_K), F32),
        jax.ShapeDtypeStruct((b, t, TOP_K), I32),
        jax.ShapeDtypeStruct((1, n_experts), I32),
    )
    return pl.pallas_call(
        functools.partial(_postmix_kernel, d=d, alpha=alpha, n_experts=n_experts),
        out_shape=out_shape,
        grid=(b, t // tm),
        in_specs=[
            tok(d), tok(ohg.shape[2]), tok(omla.shape[2]),
            pl.BlockSpec((1, 1, N_MOD * d), lambda bi, ti: (layer * 8 + jnp.where(ti < nct, n_batch, bi), 0, 0)),
            full(wohg), full(womla), full(ln1w), full(ln1b), full(rw), full(rb), full(sw1), full(sw3), full(sw2),
        ],
        out_specs=(tok(d), tok(d // 2), tok(d), tok(TOP_K), tok(TOP_K), tok(TOP_K),
                   pl.BlockSpec((1, n_experts), lambda bi, ti: (0, 0))),
        scratch_shapes=[pltpu.VMEM((1, n_experts), F32)],
        compiler_params=_cparams(("arbitrary", "arbitrary")),
        name="postmix_router",
    )(xc, ohg, omla, mod3, wohg, womla, ln1w, ln1b, rw, rb, sw1, sw3, sw2)


def _row_copy(src_ref, src_row, dst_ref, dst_row, sem):
    return pltpu.make_async_copy(src_ref.at[pl.ds(src_row, 1)], dst_ref.at[pl.ds(dst_row, 1)], sem)


def _dispatch_kernel(dest_ref, h_ref, xs_in_ref, xs_ref, sem):
    del xs_in_ref

    def issue(t, c):
        for kk in range(TOP_K):
            _row_copy(h_ref, t, xs_ref, dest_ref[0, 0, t * TOP_K + kk], sem).start(priority=kk % 2)
        return c

    lax.fori_loop(0, DISPATCH_TILE, issue, 0, unroll=2)

    def drain(t, c):
        for kk in range(TOP_K):
            _row_copy(h_ref, 0, xs_ref, 0, sem).wait()
        return c

    lax.fori_loop(0, DISPATCH_TILE, drain, 0, unroll=2)


def _dispatch(dest, h2p, n_slots):
    n, w = h2p.shape
    steps = n // DISPATCH_TILE
    dest3 = dest.reshape(steps, 1, DISPATCH_TILE * TOP_K)
    xs0 = jnp.zeros((n_slots, w), U32)
    return pl.pallas_call(
        _dispatch_kernel,
        out_shape=jax.ShapeDtypeStruct((n_slots, w), U32),
        grid=(steps,),
        in_specs=[
            pl.BlockSpec((1, 1, DISPATCH_TILE * TOP_K), lambda i: (i, 0, 0), memory_space=pltpu.SMEM),
            pl.BlockSpec((DISPATCH_TILE, w), lambda i: (i, 0)),
            pl.BlockSpec(memory_space=pl.ANY),
        ],
        out_specs=pl.BlockSpec(memory_space=pl.ANY),
        scratch_shapes=[pltpu.SemaphoreType.DMA],
        input_output_aliases={2: 0},
        compiler_params=_cparams(("arbitrary",)),
        name="moe_dispatch",
    )(dest3, h2p, xs0)


def _expert_kernel(be_ref, nused_ref, xs_ref, w1_ref, w3_ref, w2_ref, ys_ref, w1b, w3b, w2b):
    i = pl.program_id(0)
    used = i < nused_ref[0]

    @pl.when(used)
    def _():
        changed = (i == 0) | (be_ref[i] != be_ref[jnp.maximum(i - 1, 0)])

        @pl.when(changed)
        def _():
            w1b[...] = w1_ref[0, 0].astype(BF16)
            w3b[...] = w3_ref[0, 0].astype(BF16)
            w2b[...] = w2_ref[0, 0].astype(BF16)

        lo, hi = _unpack_bf16_pair(xs_ref[...])
        x = jnp.concatenate([lo.astype(BF16), hi.astype(BF16)], axis=1)
        h1 = jnp.dot(x, w1b[...], preferred_element_type=F32)
        h3 = jnp.dot(x, w3b[...], preferred_element_type=F32)
        hid = (_silu(h1) * h3).astype(BF16)
        out = jnp.dot(hid, w2b[...], preferred_element_type=F32)
        half = out.shape[1] // 2
        ys_ref[...] = _pack_bf16_pair(out[:, 0:half], out[:, half:])

    @pl.when(jnp.logical_not(used))
    def _():
        ys_ref[...] = jnp.zeros_like(ys_ref)


def _experts(xs, block_expert, n_used, w1, w3, w2, layer):
    n_slots, w = xs.shape
    n_blocks = n_slots // ROW_BLOCK
    _, _, d, hid = w1.shape
    grid_spec = pltpu.PrefetchScalarGridSpec(
        num_scalar_prefetch=2,
        grid=(n_blocks,),
        in_specs=[
            pl.BlockSpec((ROW_BLOCK, w), lambda i, be, nu: (i, 0)),
            pl.BlockSpec((1, 1, d, hid), lambda i, be, nu: (layer, be[i], 0, 0)),
            pl.BlockSpec((1, 1, d, hid), lambda i, be, nu: (layer, be[i], 0, 0)),
            pl.BlockSpec((1, 1, hid, d), lambda i, be, nu: (layer, be[i], 0, 0)),
        ],
        out_specs=pl.BlockSpec((ROW_BLOCK, w), lambda i, be, nu: (i, 0)),
        scratch_shapes=[pltpu.VMEM((d, hid), BF16), pltpu.VMEM((d, hid), BF16), pltpu.VMEM((hid, d), BF16)],
    )
    return pl.pallas_call(
        _expert_kernel,
        out_shape=jax.ShapeDtypeStruct((n_slots, w), U32),
        grid_spec=grid_spec,
        compiler_params=_cparams(("arbitrary",)),
        name="moe_experts",
    )(block_expert, n_used, xs, w1, w3, w2)


def _combine_kernel(dest_ref, wts_ref, ys_ref, sh_ref, x1_ref, mod_ref, lnw_ref, lnb_ref, o_ref, buf, sem,
                    *, d, alpha):
    def issue(t, c):
        for kk in range(TOP_K):
            pltpu.make_async_copy(ys_ref.at[pl.ds(dest_ref[0, 0, t * TOP_K + kk], 1)],
                                  buf.at[kk, pl.ds(t, 1)], sem).start(priority=kk % 2)
        return c

    lax.fori_loop(0, COMBINE_TILE, issue, 0, unroll=2)

    def drain(t, c):
        for kk in range(TOP_K):
            pltpu.make_async_copy(ys_ref.at[pl.ds(0, 1)], buf.at[0, pl.ds(0, 1)], sem).wait()
        return c

    lax.fori_loop(0, COMBINE_TILE, drain, 0, unroll=2)

    hw = d // 2
    sh = sh_ref[0].astype(F32)
    acc_lo = sh[:, 0:hw]
    acc_hi = sh[:, hw:d]
    wts = wts_ref[0]
    for kk in range(TOP_K):
        lo, hi = _unpack_bf16_pair(buf[kk])
        wk = wts[:, kk:kk + 1]
        acc_lo = acc_lo + wk * lo
        acc_hi = acc_hi + wk * hi
    ff = jnp.concatenate([acc_lo, acc_hi], axis=1)
    g_f = mod_ref[0][:, 5 * d:6 * d]
    o_ref[0] = _normalize(alpha * x1_ref[0] + g_f * ff) * lnw_ref[...] + lnb_ref[...]


def _combine(dest, wts, ys, shared, x1, mod3, layer, n_batch, nct_tiles, lnw, lnb, alpha):
    b, t, d = x1.shape
    tm = COMBINE_TILE
    tpb = t // tm
    dest3 = dest.reshape(b * tpb, 1, tm * TOP_K)
    tok = lambda w: pl.BlockSpec((1, tm, w), lambda bi, ti: (bi, ti, 0))
    full = lambda a: pl.BlockSpec(a.shape, lambda bi, ti: tuple(0 for _ in a.shape))
    return pl.pallas_call(
        functools.partial(_combine_kernel, d=d, alpha=alpha),
        out_shape=jax.ShapeDtypeStruct((b, t, d), F32),
        grid=(b, tpb),
        in_specs=[
            pl.BlockSpec((1, 1, tm * TOP_K), lambda bi, ti: (bi * tpb + ti, 0, 0), memory_space=pltpu.SMEM),
            tok(TOP_K),
            pl.BlockSpec(memory_space=pl.ANY),
            tok(d), tok(d),
            pl.BlockSpec((1, 1, N_MOD * d), lambda bi, ti: (layer * 8 + jnp.where(ti < nct_tiles, n_batch, bi), 0, 0)),
            full(lnw), full(lnb),
        ],
        out_specs=tok(d),
        scratch_shapes=[pltpu.VMEM((TOP_K, tm, d // 2), U32), pltpu.SemaphoreType.DMA],
        compiler_params=_cparams(("arbitrary", "arbitrary")),
        name="moe_combine",
    )(dest3, wts, ys, shared, x1, mod3, lnw, lnb)


def _rope_tables(n_ctx, n_lat):
    pos = jnp.arange(n_lat, dtype=I32)
    rowp = (pos // GRID_W).astype(F32)
    colp = (pos % GRID_W).astype(F32)
    n_freq = MLA_ROPE // 4
    inv = ROPE_BASE ** (-jnp.arange(n_freq, dtype=F32) / n_freq)
    ang = jnp.concatenate([rowp[:, None] * inv, colp[:, None] * inv], axis=-1)
    ang = jnp.concatenate([jnp.zeros((n_ctx, MLA_ROPE // 2), F32), ang], axis=0)
    t = n_ctx + n_lat
    ones = jnp.ones((t, MLA_NOPE), F32)
    zeros_tail = jnp.zeros((t, LANE - MLA_NOPE - MLA_ROPE), F32)
    cos_t = jnp.concatenate([ones, jnp.cos(ang), jnp.cos(ang), zeros_tail], axis=1)
    sin_t = jnp.concatenate([jnp.zeros((t, MLA_NOPE), F32), jnp.sin(ang), jnp.sin(ang), zeros_tail], axis=1)
    return cos_t, sin_t


def _rope_swap(w):
    half = MLA_ROPE // 2
    return jnp.concatenate([-w[..., half:], w[..., :half]], axis=-1)


def _prep_layer_weights(w_in_l, w_uq_l, w_ukv_l, w_out_l, hw, q_lora, kv_lora, n_heads):
    d = w_in_l.shape[0]
    base = 5 * hw + q_lora + kv_lora
    kpe_w = w_in_l[:, base:base + MLA_ROPE]
    z_nope = jnp.zeros((d, MLA_NOPE), F32)
    z_tail = jnp.zeros((d, LANE - MLA_NOPE - MLA_ROPE), F32)
    w_ext = jnp.concatenate([w_in_l[:, :base], z_nope, kpe_w, z_tail, z_nope, _rope_swap(kpe_w), z_tail],
                            axis=1).astype(BF16)
    wq = w_uq_l.reshape(q_lora, n_heads, MLA_NOPE + MLA_ROPE)
    zq = jnp.zeros((q_lora, n_heads, LANE - MLA_NOPE - MLA_ROPE), F32)
    wqa = jnp.concatenate([wq, zq], axis=-1).reshape(q_lora, n_heads * LANE).astype(BF16)
    wqb = jnp.concatenate([jnp.zeros((q_lora, n_heads, MLA_NOPE), F32), _rope_swap(wq[..., MLA_NOPE:]), zq],
                          axis=-1).reshape(q_lora, n_heads * LANE).astype(BF16)
    wkv = w_ukv_l.reshape(kv_lora, n_heads, MLA_NOPE + MLA_V)
    wk = jnp.concatenate([wkv[..., :MLA_NOPE], jnp.zeros((kv_lora, n_heads, LANE - MLA_NOPE), F32)],
                         axis=-1).reshape(kv_lora, n_heads * LANE).astype(BF16)
    wv = jnp.concatenate([wkv[..., MLA_NOPE:], jnp.zeros((kv_lora, n_heads, LANE - MLA_V), F32)],
                         axis=-1).reshape(kv_lora, n_heads * LANE).astype(BF16)
    wohg = w_out_l[:hw].astype(BF16)
    wom = w_out_l[hw:].reshape(n_heads, MLA_V, d)
    womla = jnp.concatenate([wom, jnp.zeros((n_heads, LANE - MLA_V, d), F32)], axis=1)
    womla = womla.reshape(n_heads * LANE, d).astype(BF16)
    return w_ext, wqa, wqb, wk, wv, wohg, womla


def kernel(x, c, ctx, c_ctx, w_mod, b_mod, w_in, hg_lb, hg_norm_w, q_norm_w, w_uq, kv_norm_w, w_ukv, w_out,
           ln1_w, ln1_b, router_w, router_bias, moe_w1, moe_w3, moe_w2, shared_w1, shared_w3, shared_w2,
           ln2_w, ln2_b):
    bsz, n_lat, d = x.shape
    n_ctx = ctx.shape[1]
    depth = w_mod.shape[0]
    t = n_ctx + n_lat
    hw = d // 2
    q_lora = w_uq.shape[1]
    kv_lora = w_ukv.shape[1]
    n_heads = (d - hw) // MLA_V
    n_experts = router_w.shape[2]
    alpha = float((2 * depth) ** 0.25)
    assert n_ctx % TOK_TILE == 0 and n_lat % ATT_TK == 0 and bsz < 8
    assert n_ctx % ATT_TQ == 0 and (bsz * t) % DISPATCH_TILE == 0 and t % COMBINE_TILE == 0

    lb_all = jnp.cumsum(jax.nn.softmax(hg_lb.astype(F32), axis=0), axis=0)
    lb_all = lb_all - lb_all[:1]

    c_rows = jnp.concatenate([c, c_ctx[None, :], jnp.zeros((8 - bsz - 1, d), F32)], axis=0)
    mod = _modulation(c_rows, w_mod, b_mod)
    mod3 = mod.reshape(depth * 8, 1, N_MOD * d)

    cos_t, sin_t = _rope_tables(n_ctx, n_lat)
    xc = jnp.concatenate([ctx, x], axis=1)
    nct = n_ctx // TOK_TILE

    n_tok = bsz * t
    n_assign = n_tok * TOP_K
    n_blocks = -(-(n_assign + n_experts * (ROW_BLOCK - 1)) // ROW_BLOCK)
    n_slots = n_blocks * ROW_BLOCK

    for l in range(depth):
        w_ext, wqa, wqb, wk, wv, wohg, womla = _prep_layer_weights(
            w_in[l], w_uq[l], w_ukv[l], w_out[l], hw, q_lora, kv_lora, n_heads)
        (lf_fw, lf_bw, k_fw, k_bw, q_hg, v_hg, sg, q_rot, q_unrot, kk, vv) = _inproj(
            xc, mod3, l, bsz, nct, w_ext, lb_all[l][None, :], q_norm_w[l][None, :], kv_norm_w[l][None, :],
            wqa, wqb, wk, wv, cos_t, sin_t, n_heads)
        o_fw = _gla(lf_fw, k_fw, q_hg, v_hg, nct, reverse=False)
        o_hg = _gla(lf_bw, k_bw, q_hg, v_hg, nct, reverse=True, extra=(o_fw, sg, hg_norm_w[l][None, :]))
        o_mla = _attention(q_unrot, q_rot, kk, vv, n_ctx, n_heads)
        x1, h2p, shared, idx, wts, pos, counts = _postmix(
            xc, o_hg, o_mla, mod3, l, bsz, nct, wohg, womla, ln1_w[l][None, :], ln1_b[l][None, :],
            router_w[l], router_bias[l][None, :], shared_w1[l].astype(BF16), shared_w3[l].astype(BF16),
            shared_w2[l].astype(BF16), alpha)

        counts = counts[0]
        padded = (counts + ROW_BLOCK - 1) // ROW_BLOCK * ROW_BLOCK
        pad_end = jnp.cumsum(padded)
        pad_start = pad_end - padded
        dest = (pad_start[idx.reshape(n_tok, TOP_K)] + pos.reshape(n_tok, TOP_K)).astype(I32)
        block_first_row = jnp.arange(n_blocks, dtype=I32) * ROW_BLOCK
        block_expert = jnp.minimum(
            jnp.sum((pad_end[None, :] <= block_first_row[:, None]).astype(I32), axis=1), n_experts - 1)
        n_used = (pad_end[-1:] // ROW_BLOCK).astype(I32)

        xs = _dispatch(dest, h2p.reshape(n_tok, hw), n_slots)
        ys = _experts(xs, block_expert, n_used, moe_w1, moe_w3, moe_w2, l)
        xc = _combine(dest, wts, ys, shared, x1, mod3, l, bsz, n_ctx // COMBINE_TILE,
                      ln2_w[l][None, :], ln2_b[l][None, :], alpha)
    return xc[:, n_ctx:, :]
```

```python
import functools

import jax
import jax.numpy as jnp
from jax import lax
from jax.experimental import pallas as pl
from jax.experimental.pallas import tpu as pltpu

F32 = jnp.float32
BF16 = jnp.bfloat16
U32 = jnp.uint32
I32 = jnp.int32
HIGHEST = lax.Precision.HIGHEST

HG_HEAD_DIM = 128
MLA_V = 64
MLA_NOPE = 64
MLA_ROPE = 32
GRID_W = 64
ROPE_BASE = 10000.0
TOP_K = 8
ROUTED_SCALE = 2.5
N_MOD = 6
LN_EPS = 1e-6
RMS_EPS = 1e-6

LANE = 128
TOK_TILE = 256
GLA_CHUNK = 128
GLA_SUB = 32
GLA_SAFE_EXPONENT = 80.0
ATT_TQ = 256
ATT_TK = 1024
LOG2_E = 1.4426950408889634
ROW_BLOCK = 512
DISPATCH_TILE = 256
COMBINE_TILE = 128
VMEM_LIMIT = 56 * 1024 * 1024
MASKED = -1e30


def _cparams(sem):
    return pltpu.CompilerParams(dimension_semantics=sem, vmem_limit_bytes=VMEM_LIMIT)


def _sigmoid(z):
    return 1.0 / (1.0 + jnp.exp(-z))


def _silu(z):
    return z * _sigmoid(z)


def _normalize(x):
    mu = jnp.mean(x, axis=-1, keepdims=True)
    xc = x - mu
    var = jnp.mean(xc * xc, axis=-1, keepdims=True)
    return xc * lax.rsqrt(var + LN_EPS)


def _pack_bf16_pair(lo, hi):
    lo_u = lax.bitcast_convert_type(lo.astype(BF16).astype(F32), U32) >> 16
    hi_u = lax.bitcast_convert_type(hi.astype(BF16).astype(F32), U32) & jnp.uint32(0xFFFF0000)
    return hi_u | lo_u


def _unpack_bf16_pair(u):
    lo = lax.bitcast_convert_type(u << 16, F32)
    hi = lax.bitcast_convert_type(u & jnp.uint32(0xFFFF0000), F32)
    return lo, hi


def _mod_kernel(c_ref, w_ref, b_ref, o_ref):
    c = c_ref[...]
    o_ref[0] = jnp.dot(_silu(c), w_ref[0], precision=HIGHEST, preferred_element_type=F32) + b_ref[0]


def _modulation(c_rows, w_mod, b_mod):
    depth, d, n = w_mod.shape
    tn = 1536
    return pl.pallas_call(
        _mod_kernel,
        out_shape=jax.ShapeDtypeStruct((depth, 8, n), F32),
        grid=(depth, n // tn),
        in_specs=[
            pl.BlockSpec((8, d), lambda l, j: (0, 0)),
            pl.BlockSpec((1, d, tn), lambda l, j: (l, 0, j)),
            pl.BlockSpec((1, 1, tn), lambda l, j: (l, 0, j)),
        ],
        out_specs=pl.BlockSpec((1, 8, tn), lambda l, j: (l, 0, j)),
        compiler_params=_cparams(("arbitrary", "arbitrary")),
        name="modulation",
    )(c_rows, w_mod, b_mod.reshape(depth, 1, n))


def _inproj_kernel(x_ref, mod_ref, w_ref, lb_ref, qnw_ref, kvnw_ref, wqa_ref, wqb_ref, wk_ref, wv_ref,
                   cos_ref, sin_ref,
                   lff_ref, lfb_ref, kf_ref, kb_ref, q_ref, v_ref, sg_ref, qr_ref, qu_ref, kk_ref, vv_ref,
                   *, d, hw, q_lora, kv_lora, n_heads):
    x = x_ref[0]
    m = mod_ref[0]
    shift = m[:, 0:d]
    scale = m[:, d:2 * d]
    h = (_normalize(x) * (1.0 + scale) + shift).astype(BF16)
    proj = jnp.dot(h, w_ref[...], preferred_element_type=F32)

    lb = lb_ref[...]

    def forget(z, lbd):
        f = lbd + (1.0 - lbd) * _sigmoid(z)
        return jnp.log(f), 1.0 - f

    lf, kd = forget(proj[:, 0:hw], lb[:, 0:hw])
    lff_ref[0] = lf
    kf_ref[0] = kd.astype(BF16)
    lf, kd = forget(proj[:, hw:2 * hw], lb[:, hw:2 * hw])
    lfb_ref[0] = lf
    kb_ref[0] = kd.astype(BF16)
    v_ref[0] = proj[:, 2 * hw:3 * hw].astype(BF16)
    q_ref[0] = (_silu(proj[:, 3 * hw:4 * hw]) * (HG_HEAD_DIM ** -0.5)).astype(BF16)
    sg_ref[0] = _silu(proj[:, 4 * hw:5 * hw]).astype(BF16)

    o = 5 * hw
    cq = proj[:, o:o + q_lora]
    ckv = proj[:, o + q_lora:o + q_lora + kv_lora]
    o2 = o + q_lora + kv_lora
    kpe_a = proj[:, o2:o2 + LANE]
    kpe_b = proj[:, o2 + LANE:o2 + 2 * LANE]

    cqn = (cq * lax.rsqrt(jnp.mean(cq * cq, axis=-1, keepdims=True) + RMS_EPS) * qnw_ref[...]).astype(BF16)
    ckvn = (ckv * lax.rsqrt(jnp.mean(ckv * ckv, axis=-1, keepdims=True) + RMS_EPS) * kvnw_ref[...]).astype(BF16)

    cos = cos_ref[...]
    sin = sin_ref[...]
    cos_h = jnp.concatenate([cos] * n_heads, axis=1)
    sin_h = jnp.concatenate([sin] * n_heads, axis=1)
    att_scale = (MLA_NOPE + MLA_ROPE) ** -0.5 * LOG2_E
    qa = jnp.dot(cqn, wqa_ref[...], preferred_element_type=F32)
    qb = jnp.dot(cqn, wqb_ref[...], preferred_element_type=F32)
    qu_ref[0] = (qa * att_scale).astype(BF16)
    qr_ref[0] = ((qa * cos_h + qb * sin_h) * att_scale).astype(BF16)

    kr = kpe_a * cos + kpe_b * sin
    kk = jnp.dot(ckvn, wk_ref[...], preferred_element_type=F32) + jnp.concatenate([kr] * n_heads, axis=1)
    kk_ref[0] = kk.astype(BF16)
    vv = jnp.dot(ckvn, wv_ref[...], preferred_element_type=F32)
    lane = lax.broadcasted_iota(I32, vv.shape, 1)
    vv = jnp.where((lane % LANE) == MLA_V, 1.0, vv)
    vv_ref[0] = vv.astype(BF16)


def _inproj(xc, mod3, layer, n_batch, nct, w_ext, lb, qnw, kvnw, wqa, wqb, wk, wv, cos_t, sin_t, n_heads):
    b, t, d = xc.shape
    hw = d // 2
    q_lora = wqa.shape[0]
    kv_lora = wk.shape[0]
    hp = n_heads * LANE
    tm = TOK_TILE
    full = lambda shape: pl.BlockSpec(shape, lambda bi, ti: tuple(0 for _ in shape))
    tok = lambda w: pl.BlockSpec((1, tm, w), lambda bi, ti: (bi, ti, 0))
    out_shape = (
        jax.ShapeDtypeStruct((b, t, hw), F32), jax.ShapeDtypeStruct((b, t, hw), F32),
        jax.ShapeDtypeStruct((b, t, hw), BF16), jax.ShapeDtypeStruct((b, t, hw), BF16),
        jax.ShapeDtypeStruct((b, t, hw), BF16), jax.ShapeDtypeStruct((b, t, hw), BF16),
        jax.ShapeDtypeStruct((b, t, hw), BF16),
        jax.ShapeDtypeStruct((b, t, hp), BF16), jax.ShapeDtypeStruct((b, t, hp), BF16),
        jax.ShapeDtypeStruct((b, t, hp), BF16), jax.ShapeDtypeStruct((b, t, hp), BF16),
    )
    return pl.pallas_call(
        functools.partial(_inproj_kernel, d=d, hw=hw, q_lora=q_lora, kv_lora=kv_lora, n_heads=n_heads),
        out_shape=out_shape,
        grid=(b, t // tm),
        in_specs=[
            tok(d),
            pl.BlockSpec((1, 1, N_MOD * d), lambda bi, ti: (layer * 8 + jnp.where(ti < nct, n_batch, bi), 0, 0)),
            full(w_ext.shape), full(lb.shape), full(qnw.shape), full(kvnw.shape),
            full(wqa.shape), full(wqb.shape), full(wk.shape), full(wv.shape),
            pl.BlockSpec((tm, LANE), lambda bi, ti: (ti, 0)),
            pl.BlockSpec((tm, LANE), lambda bi, ti: (ti, 0)),
        ],
        out_specs=tuple([tok(hw)] * 7 + [tok(hp)] * 4),
        compiler_params=_cparams(("arbitrary", "arbitrary")),
        name="inproj",
    )(xc, mod3, w_ext, lb, qnw, kvnw, wqa, wqb, wk, wv, cos_t, sin_t)


def _gla_kernel(*refs, reverse, n_chunks, n_heads, final):
    if final:
        lf_ref, k_ref, q_ref, v_ref, ofw_ref, sg_ref, nw_ref, o_ref, st_ref = refs
    else:
        lf_ref, k_ref, q_ref, v_ref, o_ref, st_ref = refs
    c, s = GLA_CHUNK, GLA_SUB
    n_sub = c // s

    @pl.when(pl.program_id(1) == 0)
    def _():
        st_ref[...] = jnp.zeros_like(st_ref)

    row = lax.broadcasted_iota(I32, (c, c), 0)
    col = lax.broadcasted_iota(I32, (c, c), 1)
    tri = jnp.where((row <= col) if reverse else (row >= col), 1.0, 0.0).astype(F32)
    sub_row = lax.broadcasted_iota(I32, (s, HG_HEAD_DIM), 0)
    ones = jnp.ones((HG_HEAD_DIM, HG_HEAD_DIM), BF16)

    nt = (((1,), (1,)), ((), ()))

    def sub_geometry(si):
        rs = slice(si * s, (si + 1) * s)
        if reverse:
            return rs, (slice((si + 1) * s, c), (si + 1) * s) if si < n_sub - 1 else (None, None)
        return rs, (slice(0, si * s), si * s - 1) if si > 0 else (None, None)

    def chunk_body(i, carry):
        ci = (n_chunks - 1 - i) if reverse else i
        r0 = pl.multiple_of(ci * c, c)
        rows = pl.ds(r0, c)
        bc_all = jnp.dot(tri, lf_ref[0, rows, :], precision=HIGHEST, preferred_element_type=F32)
        worst = None
        for si in range(n_sub):
            rs, (_, brow) = sub_geometry(si)
            far = bc_all[rs.start:rs.start + 1] if reverse else bc_all[rs.stop - 1:rs.stop]
            d = far if brow is None else far - bc_all[brow:brow + 1]
            worst = d if worst is None else jnp.minimum(worst, d)

        def load(h):
            sl = slice(h * HG_HEAD_DIM, (h + 1) * HG_HEAD_DIM)
            return (sl, bc_all[:, sl], k_ref[0, rows, sl].astype(F32), q_ref[0, rows, sl].astype(F32),
                    v_ref[0, rows, sl].astype(F32))

        def store(sl, o):
            if final:
                o = o + ofw_ref[0, rows, sl]
                o = o * lax.rsqrt(jnp.mean(o * o, axis=-1, keepdims=True) + RMS_EPS) * nw_ref[...]
                o_ref[0, rows, sl] = (o * sg_ref[0, rows, sl].astype(F32)).astype(o_ref.dtype)
            else:
                o_ref[0, rows, sl] = o

        def intra(h, fast):
            sl, bc, k, q, v = load(h)
            qhat = (q * jnp.exp(bc)).astype(BF16)
            o_state = lax.dot_general(qhat, st_ref[h].astype(BF16), nt, preferred_element_type=F32)
            pieces = []
            for si in range(n_sub):
                rs, (src, brow) = sub_geometry(si)
                bs, qs, ks, vs = bc[rs], q[rs], k[rs], v[rs]
                acc = o_state[rs]
                beta = jnp.zeros((1, HG_HEAD_DIM), F32) if brow is None else bc[brow:brow + 1]
                if fast:
                    src = slice(rs.start, c) if reverse else slice(0, rs.stop)
                if src is not None:
                    qi = (qs * jnp.exp(bs - beta)).astype(BF16)
                    ksrc = (k[src] * jnp.exp(beta - bc[src])).astype(BF16)
                    a = lax.dot_general(qi, ksrc, nt, preferred_element_type=F32)
                    if fast:
                        n_src = src.stop - src.start
                        r_i = lax.broadcasted_iota(I32, (s, n_src), 0)
                        c_i = lax.broadcasted_iota(I32, (s, n_src), 1)
                        keep = (c_i >= r_i) if reverse else (c_i <= r_i + rs.start)
                        a = jnp.where(keep, a, 0.0)
                    acc = acc + jnp.dot(a.astype(BF16), v[src].astype(BF16), preferred_element_type=F32)
                if not fast:
                    ws = []
                    for j in range(s):
                        mask = (sub_row <= j) if reverse else (sub_row >= j)
                        dlt = jnp.where(mask, bs - bs[j:j + 1], MASKED)
                        ws.append((jnp.exp(dlt) * qs * ks[j:j + 1]).astype(BF16))
                    sums = jnp.dot(jnp.concatenate(ws, axis=0), ones, preferred_element_type=F32)
                    for j in range(s):
                        acc = acc + sums[j * s:(j + 1) * s] * vs[j:j + 1]
                pieces.append(acc)
            store(sl, jnp.concatenate(pieces, axis=0))

        bounded = jnp.min(worst) > -GLA_SAFE_EXPONENT

        @pl.when(bounded)
        def _():
            for h in range(n_heads):
                intra(h, True)

        @pl.when(jnp.logical_not(bounded))
        def _():
            for h in range(n_heads):
                intra(h, False)

        for h in range(n_heads):
            sl, bc, k, _, v = load(h)
            tot = bc[0:1] if reverse else bc[c - 1:c]
            khat = (k * jnp.exp(tot - bc)).astype(BF16)
            st_ref[h] = st_ref[h] * jnp.exp(tot) + jnp.dot(v.T.astype(BF16), khat, preferred_element_type=F32)
        return carry

    lax.fori_loop(0, n_chunks, chunk_body, 0)


def _gla(lf, k, q, v, nct_blocks, reverse, extra=None):
    b, t, hw = lf.shape
    n_heads = hw // HG_HEAD_DIM
    blk = TOK_TILE
    nb = t // blk

    def blk_index(g):
        if not reverse:
            return g
        return jnp.where(g < nct_blocks, nct_blocks - 1 - g, nb - 1 - (g - nct_blocks))

    spec = pl.BlockSpec((1, blk, hw), lambda bi, g: (bi, blk_index(g), 0))
    in_specs = [spec, spec, spec, spec]
    args = [lf, k, q, v]
    final = extra is not None
    if final:
        ofw, sg, nw = extra
        in_specs += [spec, spec, pl.BlockSpec(nw.shape, lambda bi, g: (0, 0))]
        args += [ofw, sg, nw]
    return pl.pallas_call(
        functools.partial(_gla_kernel, reverse=reverse, n_chunks=blk // GLA_CHUNK, n_heads=n_heads, final=final),
        out_shape=jax.ShapeDtypeStruct((b, t, hw), BF16 if final else F32),
        grid=(b, nb),
        in_specs=in_specs,
        out_specs=spec,
        scratch_shapes=[pltpu.VMEM((n_heads, HG_HEAD_DIM, HG_HEAD_DIM), F32)],
        compiler_params=_cparams(("arbitrary", "arbitrary")),
        name="gla_bwd" if reverse else "gla_fwd",
    )(*args)


def _attn_kernel(qu_ref, qr_ref, k_ref, v_ref, o_ref, s_sc, *, n_ctx, n_lat):
    is_ctx = pl.program_id(2) < n_ctx // ATT_TQ
    nt = (((1,), (1,)), ((), ()))
    n_chunks = n_lat // ATT_TK

    def ctx_scores():
        return lax.dot_general(qu_ref[0], k_ref[0, 0:n_ctx, :], nt, preferred_element_type=F32)

    def finish(acc):
        o_ref[0] = (acc * (1.0 / acc[:, MLA_V:MLA_V + 1])).astype(o_ref.dtype)

    def lane_tile_max(s, m):
        for c in range(s.shape[1] // LANE):
            t = s[:, c * LANE:(c + 1) * LANE]
            m = t if m is None else jnp.maximum(m, t)
        return m

    @pl.when(is_ctx)
    def _():
        s = ctx_scores()
        p = jnp.exp2(s - jnp.max(s, axis=-1, keepdims=True))
        finish(jnp.dot(p.astype(BF16), v_ref[0, 0:n_ctx, :], preferred_element_type=F32))

    @pl.when(jnp.logical_not(is_ctx))
    def _():
        qr = qr_ref[0]
        s_c = ctx_scores()
        m_t = lane_tile_max(s_c, None)
        for j in range(n_chunks):
            rows = slice(n_ctx + j * ATT_TK, n_ctx + (j + 1) * ATT_TK)
            s = lax.dot_general(qr, k_ref[0, rows, :], nt, preferred_element_type=F32)
            s_sc[j] = s
            m_t = lane_tile_max(s, m_t)
        m = jnp.max(m_t, axis=-1, keepdims=True)
        acc = jnp.dot(jnp.exp2(s_c - m).astype(BF16), v_ref[0, 0:n_ctx, :], preferred_element_type=F32)

        def pv(j, acc):
            r0 = pl.multiple_of(n_ctx + j * ATT_TK, ATT_TQ)
            p = jnp.exp2(s_sc[j] - m).astype(BF16)
            return acc + jnp.dot(p, v_ref[0, pl.ds(r0, ATT_TK), :], preferred_element_type=F32)

        finish(lax.fori_loop(0, n_chunks, pv, acc, unroll=2))


def _attention(qu, qr, kk, vv, n_ctx, n_heads):
    b, t, hp = qu.shape
    n_lat = t - n_ctx
    qspec = pl.BlockSpec((1, ATT_TQ, LANE), lambda bi, h, qi: (bi, qi, h))
    kspec = pl.BlockSpec((1, t, LANE), lambda bi, h, qi: (bi, 0, h))
    return pl.pallas_call(
        functools.partial(_attn_kernel, n_ctx=n_ctx, n_lat=n_lat),
        out_shape=jax.ShapeDtypeStruct((b, t, hp), BF16),
        grid=(b, n_heads, t // ATT_TQ),
        in_specs=[qspec, qspec, kspec, kspec],
        out_specs=qspec,
        scratch_shapes=[pltpu.VMEM((n_lat // ATT_TK, ATT_TQ, ATT_TK), F32)],
        compiler_params=_cparams(("arbitrary", "arbitrary", "arbitrary")),
        name="mla_attention",
    )(qu, qr, kk, vv)


def _postmix_kernel(x_ref, ohg_ref, omla_ref, mod_ref, wohg_ref, womla_ref, ln1w_ref, ln1b_ref,
                    rwh_ref, rwl_ref, rb_ref, sw1_ref, sw3_ref, sw2_ref,
                    x1_ref, h2p_ref, sh_ref, idx_ref, wts_ref, pos_ref, cnt_ref, cnt_sc,
                    *, d, alpha, n_experts):
    @pl.when((pl.program_id(0) == 0) & (pl.program_id(1) == 0))
    def _():
        cnt_sc[...] = jnp.zeros_like(cnt_sc)

    m = mod_ref[0]
    g_a = m[:, 2 * d:3 * d]
    sh_f = m[:, 3 * d:4 * d]
    sc_f = m[:, 4 * d:5 * d]
    mix = (jnp.dot(ohg_ref[0], wohg_ref[...], preferred_element_type=F32)
           + jnp.dot(omla_ref[0], womla_ref[...], preferred_element_type=F32))
    x1 = _normalize(alpha * x_ref[0] + g_a * mix) * ln1w_ref[...] + ln1b_ref[...]
    x1_ref[0] = x1
    h2 = _normalize(x1) * (1.0 + sc_f) + sh_f
    hw = d // 2
    h2p_ref[0] = _pack_bf16_pair(h2[:, 0:hw], h2[:, hw:d])
    h2b = h2.astype(BF16)

    a1 = jnp.dot(h2b, sw1_ref[...], preferred_element_type=F32)
    a3 = jnp.dot(h2b, sw3_ref[...], preferred_element_type=F32)
    hid = (_silu(a1) * a3).astype(BF16)
    sh_ref[0] = jnp.dot(hid, sw2_ref[...], preferred_element_type=F32).astype(sh_ref.dtype)

    nt = (((1,), (1,)), ((), ()))
    h2lo = (h2 - h2b.astype(F32)).astype(BF16)
    logits = (lax.dot_general(rwh_ref[...], h2b, nt, preferred_element_type=F32)
              + lax.dot_general(rwh_ref[...], h2lo, nt, preferred_element_type=F32)
              + lax.dot_general(rwl_ref[...], h2b, nt, preferred_element_type=F32))
    scores = _sigmoid(logits)
    sel = scores + rb_ref[...][:, 0:1]
    tm = scores.shape[1]
    erow = lax.broadcasted_iota(I32, (n_experts, tm), 0).astype(F32)
    masks, tops, idxs = [], [], []
    for _ in range(TOP_K):
        mx = jnp.max(sel, axis=0, keepdims=True)
        ik = jnp.min(jnp.where(sel == mx, erow, float(n_experts)), axis=0, keepdims=True)
        oh = erow == ik
        masks.append(oh)
        idxs.append(ik)
        tops.append(jnp.sum(jnp.where(oh, scores, 0.0), axis=0, keepdims=True))
        sel = jnp.where(oh, -jnp.inf, sel)
    tsum = tops[0]
    for tk in tops[1:]:
        tsum = tsum + tk
    inv = ROUTED_SCALE / tsum

    ohf = jnp.zeros((n_experts, tm), F32)
    for oh in masks:
        ohf = jnp.where(oh, 1.0, ohf)
    r = lax.broadcasted_iota(I32, (tm, tm), 0)
    cc = lax.broadcasted_iota(I32, (tm, tm), 1)
    earlier = jnp.where(r < cc, 1.0, 0.0).astype(BF16)
    cnt = cnt_sc[...][:, 0:1]
    before = jnp.dot(ohf.astype(BF16), earlier, preferred_element_type=F32) + cnt
    cnt_new = cnt + jnp.sum(ohf, axis=1, keepdims=True)
    cnt_sc[...] = jnp.broadcast_to(cnt_new, cnt_sc.shape)
    cnt_ref[...] = jnp.broadcast_to(cnt_new, cnt_ref.shape).astype(I32)

    row_k = lax.broadcasted_iota(I32, (TOP_K, tm), 0)
    idx_o = jnp.zeros((TOP_K, tm), I32)
    wts_o = jnp.zeros((TOP_K, tm), F32)
    pos_o = jnp.zeros((TOP_K, tm), I32)
    for kk in range(TOP_K):
        pk = jnp.sum(jnp.where(masks[kk], before, 0.0), axis=0, keepdims=True)
        hit = row_k == kk
        idx_o = jnp.where(hit, idxs[kk].astype(I32), idx_o)
        wts_o = jnp.where(hit, tops[kk] * inv, wts_o)
        pos_o = jnp.where(hit, pk.astype(I32), pos_o)
    idx_ref[0] = idx_o
    wts_ref[0] = wts_o
    pos_ref[0] = pos_o


def _postmix(xc, ohg, omla, mod3, layer, n_batch, nct, wohg, womla, ln1w, ln1b, rw, rb, sw1, sw3, sw2, alpha):
    b, t, d = xc.shape
    tm = TOK_TILE
    n_experts = rw.shape[1]
    rwt = rw.T
    rwh = rwt.astype(BF16)
    rwl = (rwt - rwh.astype(F32)).astype(BF16)
    rb = jnp.broadcast_to(rb.reshape(n_experts, 1), (n_experts, LANE))
    full = lambda a: pl.BlockSpec(a.shape, lambda bi, ti: tuple(0 for _ in a.shape))
    tok = lambda w: pl.BlockSpec((1, tm, w), lambda bi, ti: (bi, ti, 0))
    out_shape = (
        jax.ShapeDtypeStruct((b, t, d), F32),
        jax.ShapeDtypeStruct((b, t, d // 2), U32),
        jax.ShapeDtypeStruct((b, t, d), BF16),
        jax.ShapeDtypeStruct((b, TOP_K, t), I32),
        jax.ShapeDtypeStruct((b, TOP_K, t), F32),
        jax.ShapeDtypeStruct((b, TOP_K, t), I32),
        jax.ShapeDtypeStruct((n_experts, LANE), I32),
    )
    kmaj = pl.BlockSpec((1, TOP_K, tm), lambda bi, ti: (bi, 0, ti))
    return pl.pallas_call(
        functools.partial(_postmix_kernel, d=d, alpha=alpha, n_experts=n_experts),
        out_shape=out_shape,
        grid=(b, t // tm),
        in_specs=[
            tok(d), tok(ohg.shape[2]), tok(omla.shape[2]),
            pl.BlockSpec((1, 1, N_MOD * d), lambda bi, ti: (layer * 8 + jnp.where(ti < nct, n_batch, bi), 0, 0)),
            full(wohg), full(womla), full(ln1w), full(ln1b), full(rwh), full(rwl), full(rb),
            full(sw1), full(sw3), full(sw2),
        ],
        out_specs=(tok(d), tok(d // 2), tok(d), kmaj, kmaj, kmaj,
                   pl.BlockSpec((n_experts, LANE), lambda bi, ti: (0, 0))),
        scratch_shapes=[pltpu.VMEM((n_experts, LANE), F32)],
        compiler_params=_cparams(("arbitrary", "arbitrary")),
        name="postmix_router",
    )(xc, ohg, omla, mod3, wohg, womla, ln1w, ln1b, rwh, rwl, rb, sw1, sw3, sw2)


def _row_copy(src_ref, src_row, dst_ref, dst_row, sem):
    return pltpu.make_async_copy(src_ref.at[pl.ds(src_row, 1)], dst_ref.at[pl.ds(dst_row, 1)], sem)


def _dispatch_kernel(dest_ref, h_ref, xs_in_ref, xs_ref, sem):
    del xs_in_ref

    def issue(t, c):
        for kk in range(TOP_K):
            _row_copy(h_ref, t, xs_ref, dest_ref[0, 0, t * TOP_K + kk], sem).start(priority=kk % 2)
        return c

    lax.fori_loop(0, DISPATCH_TILE, issue, 0, unroll=2)

    def drain(t, c):
        for kk in range(TOP_K):
            _row_copy(h_ref, 0, xs_ref, 0, sem).wait()
        return c

    lax.fori_loop(0, DISPATCH_TILE, drain, 0, unroll=2)


def _dispatch(dest, h2p, n_slots):
    n, w = h2p.shape
    steps = n // DISPATCH_TILE
    dest3 = dest.reshape(steps, 1, DISPATCH_TILE * TOP_K)
    xs0 = jnp.zeros((n_slots, w), U32)
    return pl.pallas_call(
        _dispatch_kernel,
        out_shape=jax.ShapeDtypeStruct((n_slots, w), U32),
        grid=(steps,),
        in_specs=[
            pl.BlockSpec((1, 1, DISPATCH_TILE * TOP_K), lambda i: (i, 0, 0), memory_space=pltpu.SMEM),
            pl.BlockSpec((DISPATCH_TILE, w), lambda i: (i, 0)),
            pl.BlockSpec(memory_space=pl.ANY),
        ],
        out_specs=pl.BlockSpec(memory_space=pl.ANY),
        scratch_shapes=[pltpu.SemaphoreType.DMA],
        input_output_aliases={2: 0},
        compiler_params=_cparams(("arbitrary",)),
        name="moe_dispatch",
    )(dest3, h2p, xs0)


def _expert_kernel(be_ref, nused_ref, xs_ref, w1_ref, w3_ref, w2_ref, ys_ref, w1b, w3b, w2b):
    i = pl.program_id(0)
    used = i < nused_ref[0]

    @pl.when(used)
    def _():
        changed = (i == 0) | (be_ref[i] != be_ref[jnp.maximum(i - 1, 0)])

        @pl.when(changed)
        def _():
            w1b[...] = w1_ref[0, 0].astype(BF16)
            w3b[...] = w3_ref[0, 0].astype(BF16)
            w2b[...] = w2_ref[0, 0].astype(BF16)

        lo, hi = _unpack_bf16_pair(xs_ref[...])
        x = jnp.concatenate([lo.astype(BF16), hi.astype(BF16)], axis=1)
        h1 = jnp.dot(x, w1b[...], preferred_element_type=F32)
        h3 = jnp.dot(x, w3b[...], preferred_element_type=F32)
        hid = (_silu(h1) * h3).astype(BF16)
        out = jnp.dot(hid, w2b[...], preferred_element_type=F32)
        half = out.shape[1] // 2
        ys_ref[...] = _pack_bf16_pair(out[:, 0:half], out[:, half:])

    @pl.when(jnp.logical_not(used))
    def _():
        ys_ref[...] = jnp.zeros_like(ys_ref)


def _experts(xs, block_expert, n_used, w1, w3, w2, layer):
    n_slots, w = xs.shape
    n_blocks = n_slots // ROW_BLOCK
    _, _, d, hid = w1.shape
    grid_spec = pltpu.PrefetchScalarGridSpec(
        num_scalar_prefetch=2,
        grid=(n_blocks,),
        in_specs=[
            pl.BlockSpec((ROW_BLOCK, w), lambda i, be, nu: (i, 0)),
            pl.BlockSpec((1, 1, d, hid), lambda i, be, nu: (layer, be[i], 0, 0)),
            pl.BlockSpec((1, 1, d, hid), lambda i, be, nu: (layer, be[i], 0, 0)),
            pl.BlockSpec((1, 1, hid, d), lambda i, be, nu: (layer, be[i], 0, 0)),
        ],
        out_specs=pl.BlockSpec((ROW_BLOCK, w), lambda i, be, nu: (i, 0)),
        scratch_shapes=[pltpu.VMEM((d, hid), BF16), pltpu.VMEM((d, hid), BF16), pltpu.VMEM((hid, d), BF16)],
    )
    return pl.pallas_call(
        _expert_kernel,
        out_shape=jax.ShapeDtypeStruct((n_slots, w), U32),
        grid_spec=grid_spec,
        compiler_params=_cparams(("arbitrary",)),
        name="moe_experts",
    )(block_expert, n_used, xs, w1, w3, w2)


def _combine_kernel(dest_ref, dnext_ref, wts_ref, ys_ref, sh_ref, x1_ref, mod_ref, lnw_ref, lnb_ref, o_ref,
                    buf, sems, *, d, alpha, n_steps):
    g = pl.program_id(0) * pl.num_programs(1) + pl.program_id(1)
    cur = g % 2
    nxt = 1 - cur

    def issue_tile(tbl_ref, slot):
        def body(t, c):
            for kk in range(TOP_K):
                pltpu.make_async_copy(ys_ref.at[pl.ds(tbl_ref[0, 0, t * TOP_K + kk], 1)],
                                      buf.at[slot, kk, pl.ds(t, 1)], sems.at[slot]).start(priority=kk % 2)
            return c
        lax.fori_loop(0, COMBINE_TILE, body, 0, unroll=2)

    def drain(slot):
        def body(t, c):
            for _ in range(TOP_K):
                pltpu.make_async_copy(ys_ref.at[pl.ds(0, 1)], buf.at[slot, 0, pl.ds(0, 1)], sems.at[slot]).wait()
            return c
        lax.fori_loop(0, COMBINE_TILE, body, 0, unroll=2)

    @pl.when(g == 0)
    def _():
        issue_tile(dest_ref, 0)

    issue_tile(dnext_ref, nxt)
    drain(cur)

    hw = d // 2
    sh = sh_ref[0].astype(F32)
    acc_lo = sh[:, 0:hw]
    acc_hi = sh[:, hw:d]
    wts = wts_ref[0]
    for kk in range(TOP_K):
        lo, hi = _unpack_bf16_pair(buf[cur, kk])
        wk = wts[:, kk:kk + 1]
        acc_lo = acc_lo + wk * lo
        acc_hi = acc_hi + wk * hi
    ff = jnp.concatenate([acc_lo, acc_hi], axis=1)
    g_f = mod_ref[0][:, 5 * d:6 * d]
    o_ref[0] = _normalize(alpha * x1_ref[0] + g_f * ff) * lnw_ref[...] + lnb_ref[...]

    @pl.when(g == n_steps - 1)
    def _():
        drain(nxt)


def _combine(dest, wts, ys, shared, x1, mod3, layer, n_batch, nct_tiles, lnw, lnb, alpha):
    b, t, d = x1.shape
    tm = COMBINE_TILE
    tpb = t // tm
    n_steps = b * tpb
    dest3 = dest.reshape(n_steps, 1, tm * TOP_K)
    tok = lambda w: pl.BlockSpec((1, tm, w), lambda bi, ti: (bi, ti, 0))
    full = lambda a: pl.BlockSpec(a.shape, lambda bi, ti: tuple(0 for _ in a.shape))
    table = lambda shift: pl.BlockSpec(
        (1, 1, tm * TOP_K), lambda bi, ti: (jnp.minimum(bi * tpb + ti + shift, n_steps - 1), 0, 0),
        memory_space=pltpu.SMEM)
    return pl.pallas_call(
        functools.partial(_combine_kernel, d=d, alpha=alpha, n_steps=n_steps),
        out_shape=jax.ShapeDtypeStruct((b, t, d), F32),
        grid=(b, tpb),
        in_specs=[
            table(0), table(1),
            tok(TOP_K),
            pl.BlockSpec(memory_space=pl.ANY),
            tok(d), tok(d),
            pl.BlockSpec((1, 1, N_MOD * d), lambda bi, ti: (layer * 8 + jnp.where(ti < nct_tiles, n_batch, bi), 0, 0)),
            full(lnw), full(lnb),
        ],
        out_specs=tok(d),
        scratch_shapes=[pltpu.VMEM((2, TOP_K, tm, d // 2), U32), pltpu.SemaphoreType.DMA((2,))],
        compiler_params=_cparams(("arbitrary", "arbitrary")),
        name="moe_combine",
    )(dest3, dest3, wts, ys, shared, x1, mod3, lnw, lnb)


def _rope_tables(n_ctx, n_lat):
    pos = jnp.arange(n_lat, dtype=I32)
    rowp = (pos // GRID_W).astype(F32)
    colp = (pos % GRID_W).astype(F32)
    n_freq = MLA_ROPE // 4
    inv = ROPE_BASE ** (-jnp.arange(n_freq, dtype=F32) / n_freq)
    ang = jnp.concatenate([rowp[:, None] * inv, colp[:, None] * inv], axis=-1)
    ang = jnp.concatenate([jnp.zeros((n_ctx, MLA_ROPE // 2), F32), ang], axis=0)
    t = n_ctx + n_lat
    ones = jnp.ones((t, MLA_NOPE), F32)
    zeros_tail = jnp.zeros((t, LANE - MLA_NOPE - MLA_ROPE), F32)
    cos_t = jnp.concatenate([ones, jnp.cos(ang), jnp.cos(ang), zeros_tail], axis=1)
    sin_t = jnp.concatenate([jnp.zeros((t, MLA_NOPE), F32), jnp.sin(ang), jnp.sin(ang), zeros_tail], axis=1)
    return cos_t, sin_t


def _rope_swap(w):
    half = MLA_ROPE // 2
    return jnp.concatenate([-w[..., half:], w[..., :half]], axis=-1)


def _prep_layer_weights(w_in_l, w_uq_l, w_ukv_l, w_out_l, hw, q_lora, kv_lora, n_heads):
    d = w_in_l.shape[0]
    base = 5 * hw + q_lora + kv_lora
    kpe_w = w_in_l[:, base:base + MLA_ROPE]
    z_nope = jnp.zeros((d, MLA_NOPE), F32)
    z_tail = jnp.zeros((d, LANE - MLA_NOPE - MLA_ROPE), F32)
    w_ext = jnp.concatenate([w_in_l[:, :base], z_nope, kpe_w, z_tail, z_nope, _rope_swap(kpe_w), z_tail],
                            axis=1).astype(BF16)
    wq = w_uq_l.reshape(q_lora, n_heads, MLA_NOPE + MLA_ROPE)
    zq = jnp.zeros((q_lora, n_heads, LANE - MLA_NOPE - MLA_ROPE), F32)
    wqa = jnp.concatenate([wq, zq], axis=-1).reshape(q_lora, n_heads * LANE).astype(BF16)
    wqb = jnp.concatenate([jnp.zeros((q_lora, n_heads, MLA_NOPE), F32), _rope_swap(wq[..., MLA_NOPE:]), zq],
                          axis=-1).reshape(q_lora, n_heads * LANE).astype(BF16)
    wkv = w_ukv_l.reshape(kv_lora, n_heads, MLA_NOPE + MLA_V)
    wk = jnp.concatenate([wkv[..., :MLA_NOPE], jnp.zeros((kv_lora, n_heads, LANE - MLA_NOPE), F32)],
                         axis=-1).reshape(kv_lora, n_heads * LANE).astype(BF16)
    wv = jnp.concatenate([wkv[..., MLA_NOPE:], jnp.zeros((kv_lora, n_heads, LANE - MLA_V), F32)],
                         axis=-1).reshape(kv_lora, n_heads * LANE).astype(BF16)
    wohg = w_out_l[:hw].astype(BF16)
    wom = w_out_l[hw:].reshape(n_heads, MLA_V, d)
    womla = jnp.concatenate([wom, jnp.zeros((n_heads, LANE - MLA_V, d), F32)], axis=1)
    womla = womla.reshape(n_heads * LANE, d).astype(BF16)
    return w_ext, wqa, wqb, wk, wv, wohg, womla


def kernel(x, c, ctx, c_ctx, w_mod, b_mod, w_in, hg_lb, hg_norm_w, q_norm_w, w_uq, kv_norm_w, w_ukv, w_out,
           ln1_w, ln1_b, router_w, router_bias, moe_w1, moe_w3, moe_w2, shared_w1, shared_w3, shared_w2,
           ln2_w, ln2_b):
    bsz, n_lat, d = x.shape
    n_ctx = ctx.shape[1]
    depth = w_mod.shape[0]
    t = n_ctx + n_lat
    hw = d // 2
    q_lora = w_uq.shape[1]
    kv_lora = w_ukv.shape[1]
    n_heads = (d - hw) // MLA_V
    n_experts = router_w.shape[2]
    alpha = float((2 * depth) ** 0.25)
    assert n_ctx % TOK_TILE == 0 and n_lat % ATT_TK == 0 and bsz < 8
    assert n_ctx % ATT_TQ == 0 and (bsz * t) % DISPATCH_TILE == 0 and t % COMBINE_TILE == 0

    lb_all = jnp.cumsum(jax.nn.softmax(hg_lb.astype(F32), axis=0), axis=0)
    lb_all = lb_all - lb_all[:1]

    c_rows = jnp.concatenate([c, c_ctx[None, :], jnp.zeros((8 - bsz - 1, d), F32)], axis=0)
    mod = _modulation(c_rows, w_mod, b_mod)
    mod3 = mod.reshape(depth * 8, 1, N_MOD * d)

    cos_t, sin_t = _rope_tables(n_ctx, n_lat)
    xc = jnp.concatenate([ctx, x], axis=1)
    nct = n_ctx // TOK_TILE

    n_tok = bsz * t
    n_assign = n_tok * TOP_K
    n_blocks = -(-(n_assign + n_experts * (ROW_BLOCK - 1)) // ROW_BLOCK)
    n_slots = n_blocks * ROW_BLOCK

    for l in range(depth):
        w_ext, wqa, wqb, wk, wv, wohg, womla = _prep_layer_weights(
            w_in[l], w_uq[l], w_ukv[l], w_out[l], hw, q_lora, kv_lora, n_heads)
        (lf_fw, lf_bw, k_fw, k_bw, q_hg, v_hg, sg, q_rot, q_unrot, kk, vv) = _inproj(
            xc, mod3, l, bsz, nct, w_ext, lb_all[l][None, :], q_norm_w[l][None, :], kv_norm_w[l][None, :],
            wqa, wqb, wk, wv, cos_t, sin_t, n_heads)
        o_fw = _gla(lf_fw, k_fw, q_hg, v_hg, nct, reverse=False)
        o_hg = _gla(lf_bw, k_bw, q_hg, v_hg, nct, reverse=True, extra=(o_fw, sg, hg_norm_w[l][None, :]))
        o_mla = _attention(q_unrot, q_rot, kk, vv, n_ctx, n_heads)
        x1, h2p, shared, idx, wts, pos, counts = _postmix(
            xc, o_hg, o_mla, mod3, l, bsz, nct, wohg, womla, ln1_w[l][None, :], ln1_b[l][None, :],
            router_w[l], router_bias[l][None, :], shared_w1[l].astype(BF16), shared_w3[l].astype(BF16),
            shared_w2[l].astype(BF16), alpha)

        counts = counts[:, 0]
        idx = jnp.swapaxes(idx, 1, 2)
        pos = jnp.swapaxes(pos, 1, 2)
        wts = jnp.swapaxes(wts, 1, 2)
        padded = (counts + ROW_BLOCK - 1) // ROW_BLOCK * ROW_BLOCK
        pad_end = jnp.cumsum(padded)
        pad_start = pad_end - padded
        dest = (pad_start[idx.reshape(n_tok, TOP_K)] + pos.reshape(n_tok, TOP_K)).astype(I32)
        block_first_row = jnp.arange(n_blocks, dtype=I32) * ROW_BLOCK
        block_expert = jnp.minimum(
            jnp.sum((pad_end[None, :] <= block_first_row[:, None]).astype(I32), axis=1), n_experts - 1)
        n_used = (pad_end[-1:] // ROW_BLOCK).astype(I32)

        xs = _dispatch(dest, h2p.reshape(n_tok, hw), n_slots)
        ys = _experts(xs, block_expert, n_used, moe_w1, moe_w3, moe_w2, l)
        xc = _combine(dest, wts, ys, shared, x1, mod3, l, bsz, n_ctx // COMBINE_TILE,
                      ln2_w[l][None, :], ln2_b[l][None, :], alpha)
    return xc[:, n_ctx:, :]
```

```python
import functools

import jax
import jax.numpy as jnp
from jax import lax
from jax.experimental import pallas as pl
from jax.experimental.pallas import tpu as pltpu

F32 = jnp.float32
BF16 = jnp.bfloat16
U32 = jnp.uint32
I32 = jnp.int32
HIGHEST = lax.Precision.HIGHEST

HG_HEAD_DIM = 128
MLA_V = 64
MLA_NOPE = 64
MLA_ROPE = 32
GRID_W = 64
ROPE_BASE = 10000.0
TOP_K = 8
ROUTED_SCALE = 2.5
N_MOD = 6
LN_EPS = 1e-6
RMS_EPS = 1e-6

LANE = 128
TOK_TILE = 256
GLA_CHUNK = 128
GLA_SUB = 32
GLA_SAFE_EXPONENT = 80.0
ATT_TQ = 256
ATT_TK = 1024
ATT_HEADS = 2
LOG2_E = 1.4426950408889634
ROW_BLOCK = 512
DISPATCH_TILE = 256
COMBINE_TILE = 128
VMEM_LIMIT = 56 * 1024 * 1024
MASKED = -1e30


def _cparams(sem):
    return pltpu.CompilerParams(dimension_semantics=sem, vmem_limit_bytes=VMEM_LIMIT)


def _sigmoid(z):
    return 1.0 / (1.0 + jnp.exp(-z))


def _silu(z):
    return z * _sigmoid(z)


def _normalize(x):
    mu = jnp.mean(x, axis=-1, keepdims=True)
    xc = x - mu
    var = jnp.mean(xc * xc, axis=-1, keepdims=True)
    return xc * lax.rsqrt(var + LN_EPS)


def _pack_bf16_pair(lo, hi):
    lo_u = lax.bitcast_convert_type(lo.astype(BF16).astype(F32), U32) >> 16
    hi_u = lax.bitcast_convert_type(hi.astype(BF16).astype(F32), U32) & jnp.uint32(0xFFFF0000)
    return hi_u | lo_u


def _unpack_bf16_pair(u):
    lo = lax.bitcast_convert_type(u << 16, F32)
    hi = lax.bitcast_convert_type(u & jnp.uint32(0xFFFF0000), F32)
    return lo, hi


def _mod_kernel(c_ref, w_ref, b_ref, o_ref):
    c = c_ref[...]
    o_ref[0] = jnp.dot(_silu(c), w_ref[0], precision=HIGHEST, preferred_element_type=F32) + b_ref[0]


def _modulation(c_rows, w_mod, b_mod):
    depth, d, n = w_mod.shape
    tn = 1536
    return pl.pallas_call(
        _mod_kernel,
        out_shape=jax.ShapeDtypeStruct((depth, 8, n), F32),
        grid=(depth, n // tn),
        in_specs=[
            pl.BlockSpec((8, d), lambda l, j: (0, 0)),
            pl.BlockSpec((1, d, tn), lambda l, j: (l, 0, j)),
            pl.BlockSpec((1, 1, tn), lambda l, j: (l, 0, j)),
        ],
        out_specs=pl.BlockSpec((1, 8, tn), lambda l, j: (l, 0, j)),
        compiler_params=_cparams(("arbitrary", "arbitrary")),
        name="modulation",
    )(c_rows, w_mod, b_mod.reshape(depth, 1, n))


def _inproj_kernel(x_ref, mod_ref, w_ref, lb_ref, qnw_ref, kvnw_ref, wqa_ref, wqb_ref, wk_ref, wv_ref,
                   cos_ref, sin_ref,
                   lff_ref, lfb_ref, kf_ref, kb_ref, q_ref, v_ref, sg_ref, qr_ref, qu_ref, kk_ref, vv_ref,
                   *, d, hw, q_lora, kv_lora, n_heads):
    x = x_ref[0]
    m = mod_ref[0]
    shift = m[:, 0:d]
    scale = m[:, d:2 * d]
    h = (_normalize(x) * (1.0 + scale) + shift).astype(BF16)
    proj = jnp.dot(h, w_ref[...], preferred_element_type=F32)

    lb = lb_ref[...]

    def forget(z, lbd):
        f = lbd + (1.0 - lbd) * _sigmoid(z)
        return jnp.log(f), 1.0 - f

    lf, kd = forget(proj[:, 0:hw], lb[:, 0:hw])
    lff_ref[0] = lf
    kf_ref[0] = kd.astype(BF16)
    lf, kd = forget(proj[:, hw:2 * hw], lb[:, hw:2 * hw])
    lfb_ref[0] = lf
    kb_ref[0] = kd.astype(BF16)
    v_ref[0] = proj[:, 2 * hw:3 * hw].astype(BF16)
    q_ref[0] = (_silu(proj[:, 3 * hw:4 * hw]) * (HG_HEAD_DIM ** -0.5)).astype(BF16)
    sg_ref[0] = _silu(proj[:, 4 * hw:5 * hw]).astype(BF16)

    o = 5 * hw
    cq = proj[:, o:o + q_lora]
    ckv = proj[:, o + q_lora:o + q_lora + kv_lora]
    o2 = o + q_lora + kv_lora
    kpe_a = proj[:, o2:o2 + LANE]
    kpe_b = proj[:, o2 + LANE:o2 + 2 * LANE]

    cqn = (cq * lax.rsqrt(jnp.mean(cq * cq, axis=-1, keepdims=True) + RMS_EPS) * qnw_ref[...]).astype(BF16)
    ckvn = (ckv * lax.rsqrt(jnp.mean(ckv * ckv, axis=-1, keepdims=True) + RMS_EPS) * kvnw_ref[...]).astype(BF16)

    cos = cos_ref[...]
    sin = sin_ref[...]
    cos_h = jnp.concatenate([cos] * n_heads, axis=1)
    sin_h = jnp.concatenate([sin] * n_heads, axis=1)
    att_scale = (MLA_NOPE + MLA_ROPE) ** -0.5 * LOG2_E
    qa = jnp.dot(cqn, wqa_ref[...], preferred_element_type=F32)
    qb = jnp.dot(cqn, wqb_ref[...], preferred_element_type=F32)
    qu_ref[0] = (qa * att_scale).astype(BF16)
    qr_ref[0] = ((qa * cos_h + qb * sin_h) * att_scale).astype(BF16)

    kr = kpe_a * cos + kpe_b * sin
    kk = jnp.dot(ckvn, wk_ref[...], preferred_element_type=F32) + jnp.concatenate([kr] * n_heads, axis=1)
    kk_ref[0] = kk.astype(BF16)
    vv = jnp.dot(ckvn, wv_ref[...], preferred_element_type=F32)
    lane = lax.broadcasted_iota(I32, vv.shape, 1)
    vv = jnp.where((lane % LANE) == MLA_V, 1.0, vv)
    vv_ref[0] = vv.astype(BF16)


def _inproj(xc, mod3, layer, n_batch, nct, w_ext, lb, qnw, kvnw, wqa, wqb, wk, wv, cos_t, sin_t, n_heads):
    b, t, d = xc.shape
    hw = d // 2
    q_lora = wqa.shape[0]
    kv_lora = wk.shape[0]
    hp = n_heads * LANE
    tm = TOK_TILE
    full = lambda shape: pl.BlockSpec(shape, lambda bi, ti: tuple(0 for _ in shape))
    tok = lambda w: pl.BlockSpec((1, tm, w), lambda bi, ti: (bi, ti, 0))
    out_shape = (
        jax.ShapeDtypeStruct((b, t, hw), F32), jax.ShapeDtypeStruct((b, t, hw), F32),
        jax.ShapeDtypeStruct((b, t, hw), BF16), jax.ShapeDtypeStruct((b, t, hw), BF16),
        jax.ShapeDtypeStruct((b, t, hw), BF16), jax.ShapeDtypeStruct((b, t, hw), BF16),
        jax.ShapeDtypeStruct((b, t, hw), BF16),
        jax.ShapeDtypeStruct((b, t, hp), BF16), jax.ShapeDtypeStruct((b, t, hp), BF16),
        jax.ShapeDtypeStruct((b, t, hp), BF16), jax.ShapeDtypeStruct((b, t, hp), BF16),
    )
    return pl.pallas_call(
        functools.partial(_inproj_kernel, d=d, hw=hw, q_lora=q_lora, kv_lora=kv_lora, n_heads=n_heads),
        out_shape=out_shape,
        grid=(b, t // tm),
        in_specs=[
            tok(d),
            pl.BlockSpec((1, 1, N_MOD * d), lambda bi, ti: (layer * 8 + jnp.where(ti < nct, n_batch, bi), 0, 0)),
            full(w_ext.shape), full(lb.shape), full(qnw.shape), full(kvnw.shape),
            full(wqa.shape), full(wqb.shape), full(wk.shape), full(wv.shape),
            pl.BlockSpec((tm, LANE), lambda bi, ti: (ti, 0)),
            pl.BlockSpec((tm, LANE), lambda bi, ti: (ti, 0)),
        ],
        out_specs=tuple([tok(hw)] * 7 + [tok(hp)] * 4),
        compiler_params=_cparams(("arbitrary", "arbitrary")),
        name="inproj",
    )(xc, mod3, w_ext, lb, qnw, kvnw, wqa, wqb, wk, wv, cos_t, sin_t)


def _gla_kernel(*refs, reverse, n_chunks, n_heads, final):
    if final:
        lf_ref, k_ref, q_ref, v_ref, ofw_ref, sg_ref, nw_ref, o_ref, st_ref = refs
    else:
        lf_ref, k_ref, q_ref, v_ref, o_ref, st_ref = refs
    c, s = GLA_CHUNK, GLA_SUB
    n_sub = c // s

    @pl.when(pl.program_id(1) == 0)
    def _():
        st_ref[...] = jnp.zeros_like(st_ref)

    row = lax.broadcasted_iota(I32, (c, c), 0)
    col = lax.broadcasted_iota(I32, (c, c), 1)
    tri = jnp.where((row <= col) if reverse else (row >= col), 1.0, 0.0).astype(F32)
    sub_row = lax.broadcasted_iota(I32, (s, HG_HEAD_DIM), 0)
    ones = jnp.ones((HG_HEAD_DIM, HG_HEAD_DIM), BF16)

    nt = (((1,), (1,)), ((), ()))

    def sub_geometry(si):
        rs = slice(si * s, (si + 1) * s)
        if reverse:
            return rs, (slice((si + 1) * s, c), (si + 1) * s) if si < n_sub - 1 else (None, None)
        return rs, (slice(0, si * s), si * s - 1) if si > 0 else (None, None)

    def chunk_body(i, carry):
        ci = (n_chunks - 1 - i) if reverse else i
        r0 = pl.multiple_of(ci * c, c)
        rows = pl.ds(r0, c)
        bc_all = jnp.dot(tri, lf_ref[0, rows, :], precision=HIGHEST, preferred_element_type=F32)
        worst = None
        for si in range(n_sub):
            rs, (_, brow) = sub_geometry(si)
            far = bc_all[rs.start:rs.start + 1] if reverse else bc_all[rs.stop - 1:rs.stop]
            d = far if brow is None else far - bc_all[brow:brow + 1]
            worst = d if worst is None else jnp.minimum(worst, d)

        def load(h):
            sl = slice(h * HG_HEAD_DIM, (h + 1) * HG_HEAD_DIM)
            return (sl, bc_all[:, sl], k_ref[0, rows, sl].astype(F32), q_ref[0, rows, sl].astype(F32),
                    v_ref[0, rows, sl].astype(F32))

        def store(sl, o):
            if final:
                o = o + ofw_ref[0, rows, sl]
                o = o * lax.rsqrt(jnp.mean(o * o, axis=-1, keepdims=True) + RMS_EPS) * nw_ref[...]
                o_ref[0, rows, sl] = (o * sg_ref[0, rows, sl].astype(F32)).astype(o_ref.dtype)
            else:
                o_ref[0, rows, sl] = o

        def intra(h, fast):
            sl, bc, k, q, v = load(h)
            qhat = (q * jnp.exp(bc)).astype(BF16)
            o_state = lax.dot_general(qhat, st_ref[h].astype(BF16), nt, preferred_element_type=F32)
            pieces = []
            for si in range(n_sub):
                rs, (src, brow) = sub_geometry(si)
                bs, qs, ks, vs = bc[rs], q[rs], k[rs], v[rs]
                acc = o_state[rs]
                beta = jnp.zeros((1, HG_HEAD_DIM), F32) if brow is None else bc[brow:brow + 1]
                if fast:
                    src = slice(rs.start, c) if reverse else slice(0, rs.stop)
                if src is not None:
                    qi = (qs * jnp.exp(bs - beta)).astype(BF16)
                    ksrc = (k[src] * jnp.exp(beta - bc[src])).astype(BF16)
                    a = lax.dot_general(qi, ksrc, nt, preferred_element_type=F32)
                    if fast:
                        n_src = src.stop - src.start
                        r_i = lax.broadcasted_iota(I32, (s, n_src), 0)
                        c_i = lax.broadcasted_iota(I32, (s, n_src), 1)
                        keep = (c_i >= r_i) if reverse else (c_i <= r_i + rs.start)
                        a = jnp.where(keep, a, 0.0)
                    acc = acc + jnp.dot(a.astype(BF16), v[src].astype(BF16), preferred_element_type=F32)
                if not fast:
                    ws = []
                    for j in range(s):
                        mask = (sub_row <= j) if reverse else (sub_row >= j)
                        dlt = jnp.where(mask, bs - bs[j:j + 1], MASKED)
                        ws.append((jnp.exp(dlt) * qs * ks[j:j + 1]).astype(BF16))
                    sums = jnp.dot(jnp.concatenate(ws, axis=0), ones, preferred_element_type=F32)
                    for j in range(s):
                        acc = acc + sums[j * s:(j + 1) * s] * vs[j:j + 1]
                pieces.append(acc)
            store(sl, jnp.concatenate(pieces, axis=0))

        bounded = jnp.min(worst) > -GLA_SAFE_EXPONENT

        @pl.when(bounded)
        def _():
            for h in range(n_heads):
                intra(h, True)

        @pl.when(jnp.logical_not(bounded))
        def _():
            for h in range(n_heads):
                intra(h, False)

        for h in range(n_heads):
            sl, bc, k, _, v = load(h)
            tot = bc[0:1] if reverse else bc[c - 1:c]
            khat = (k * jnp.exp(tot - bc)).astype(BF16)
            st_ref[h] = st_ref[h] * jnp.exp(tot) + jnp.dot(v.T.astype(BF16), khat, preferred_element_type=F32)
        return carry

    lax.fori_loop(0, n_chunks, chunk_body, 0)


def _gla(lf, k, q, v, nct_blocks, reverse, extra=None):
    b, t, hw = lf.shape
    n_heads = hw // HG_HEAD_DIM
    blk = TOK_TILE
    nb = t // blk

    def blk_index(g):
        if not reverse:
            return g
        return jnp.where(g < nct_blocks, nct_blocks - 1 - g, nb - 1 - (g - nct_blocks))

    spec = pl.BlockSpec((1, blk, hw), lambda bi, g: (bi, blk_index(g), 0))
    in_specs = [spec, spec, spec, spec]
    args = [lf, k, q, v]
    final = extra is not None
    if final:
        ofw, sg, nw = extra
        in_specs += [spec, spec, pl.BlockSpec(nw.shape, lambda bi, g: (0, 0))]
        args += [ofw, sg, nw]
    return pl.pallas_call(
        functools.partial(_gla_kernel, reverse=reverse, n_chunks=blk // GLA_CHUNK, n_heads=n_heads, final=final),
        out_shape=jax.ShapeDtypeStruct((b, t, hw), BF16 if final else F32),
        grid=(b, nb),
        in_specs=in_specs,
        out_specs=spec,
        scratch_shapes=[pltpu.VMEM((n_heads, HG_HEAD_DIM, HG_HEAD_DIM), F32)],
        compiler_params=_cparams(("arbitrary", "arbitrary")),
        name="gla_bwd" if reverse else "gla_fwd",
    )(*args)


def _attn_kernel(qu_ref, qr_ref, k_ref, v_ref, o_ref, s_sc, *, n_ctx, n_lat):
    is_ctx = pl.program_id(2) < n_ctx // ATT_TQ
    nt = (((1,), (1,)), ((), ()))
    n_chunks = n_lat // ATT_TK
    heads = [slice(h * LANE, (h + 1) * LANE) for h in range(ATT_HEADS)]

    def ctx_scores(hs):
        return lax.dot_general(qu_ref[0, :, hs], k_ref[0, 0:n_ctx, hs], nt, preferred_element_type=F32)

    def finish(hs, acc):
        o_ref[0, :, hs] = (acc * (1.0 / acc[:, MLA_V:MLA_V + 1])).astype(o_ref.dtype)

    def lane_tile_max(s, m):
        for c in range(s.shape[1] // LANE):
            t = s[:, c * LANE:(c + 1) * LANE]
            m = t if m is None else jnp.maximum(m, t)
        return m

    @pl.when(is_ctx)
    def _():
        for hs in heads:
            s = ctx_scores(hs)
            p = jnp.exp2(s - jnp.max(s, axis=-1, keepdims=True))
            finish(hs, jnp.dot(p.astype(BF16), v_ref[0, 0:n_ctx, hs], preferred_element_type=F32))

    @pl.when(jnp.logical_not(is_ctx))
    def _():
        ms, accs = [], []
        for hi, hs in enumerate(heads):
            qr = qr_ref[0, :, hs]
            s_c = ctx_scores(hs)
            m_t = lane_tile_max(s_c, None)
            for j in range(n_chunks):
                rows = slice(n_ctx + j * ATT_TK, n_ctx + (j + 1) * ATT_TK)
                s = lax.dot_general(qr, k_ref[0, rows, hs], nt, preferred_element_type=F32)
                s_sc[hi, j] = s
                m_t = lane_tile_max(s, m_t)
            m = jnp.max(m_t, axis=-1, keepdims=True)
            ms.append(m)
            accs.append(jnp.dot(jnp.exp2(s_c - m).astype(BF16), v_ref[0, 0:n_ctx, hs],
                                preferred_element_type=F32))

        def pv(j, accs):
            r0 = pl.multiple_of(n_ctx + j * ATT_TK, ATT_TQ)
            out = []
            for hi, hs in enumerate(heads):
                p = jnp.exp2(s_sc[hi, j] - ms[hi]).astype(BF16)
                out.append(accs[hi] + jnp.dot(p, v_ref[0, pl.ds(r0, ATT_TK), hs], preferred_element_type=F32))
            return tuple(out)

        accs = lax.fori_loop(0, n_chunks, pv, tuple(accs), unroll=2)
        for hi, hs in enumerate(heads):
            finish(hs, accs[hi])


def _attention(qu, qr, kk, vv, n_ctx, n_heads):
    b, t, hp = qu.shape
    n_lat = t - n_ctx
    w = ATT_HEADS * LANE
    qspec = pl.BlockSpec((1, ATT_TQ, w), lambda bi, h, qi: (bi, qi, h))
    kspec = pl.BlockSpec((1, t, w), lambda bi, h, qi: (bi, 0, h))
    return pl.pallas_call(
        functools.partial(_attn_kernel, n_ctx=n_ctx, n_lat=n_lat),
        out_shape=jax.ShapeDtypeStruct((b, t, hp), BF16),
        grid=(b, n_heads // ATT_HEADS, t // ATT_TQ),
        in_specs=[qspec, qspec, kspec, kspec],
        out_specs=qspec,
        scratch_shapes=[pltpu.VMEM((ATT_HEADS, n_lat // ATT_TK, ATT_TQ, ATT_TK), F32)],
        compiler_params=_cparams(("arbitrary", "arbitrary", "arbitrary")),
        name="mla_attention",
    )(qu, qr, kk, vv)


def _postmix_kernel(x_ref, ohg_ref, omla_ref, mod_ref, wohg_ref, womla_ref, ln1w_ref, ln1b_ref,
                    rwh_ref, rwl_ref, rb_ref, sw1_ref, sw3_ref, sw2_ref,
                    x1_ref, h2p_ref, sh_ref, idx_ref, wts_ref, pos_ref, cnt_ref, cnt_sc,
                    *, d, alpha, n_experts):
    @pl.when((pl.program_id(0) == 0) & (pl.program_id(1) == 0))
    def _():
        cnt_sc[...] = jnp.zeros_like(cnt_sc)

    m = mod_ref[0]
    g_a = m[:, 2 * d:3 * d]
    sh_f = m[:, 3 * d:4 * d]
    sc_f = m[:, 4 * d:5 * d]
    mix = (jnp.dot(ohg_ref[0], wohg_ref[...], preferred_element_type=F32)
           + jnp.dot(omla_ref[0], womla_ref[...], preferred_element_type=F32))
    x1 = _normalize(alpha * x_ref[0] + g_a * mix) * ln1w_ref[...] + ln1b_ref[...]
    x1_ref[0] = x1
    h2 = _normalize(x1) * (1.0 + sc_f) + sh_f
    hw = d // 2
    h2p_ref[0] = _pack_bf16_pair(h2[:, 0:hw], h2[:, hw:d])
    h2b = h2.astype(BF16)

    a1 = jnp.dot(h2b, sw1_ref[...], preferred_element_type=F32)
    a3 = jnp.dot(h2b, sw3_ref[...], preferred_element_type=F32)
    hid = (_silu(a1) * a3).astype(BF16)
    sh_ref[0] = jnp.dot(hid, sw2_ref[...], preferred_element_type=F32).astype(sh_ref.dtype)

    nt = (((1,), (1,)), ((), ()))
    h2lo = (h2 - h2b.astype(F32)).astype(BF16)
    logits = (lax.dot_general(rwh_ref[...], h2b, nt, preferred_element_type=F32)
              + lax.dot_general(rwh_ref[...], h2lo, nt, preferred_element_type=F32)
              + lax.dot_general(rwl_ref[...], h2b, nt, preferred_element_type=F32))
    scores = _sigmoid(logits)
    sel = scores + rb_ref[...][:, 0:1]
    tm = scores.shape[1]
    erow = lax.broadcasted_iota(I32, (n_experts, tm), 0).astype(F32)
    masks, tops, idxs = [], [], []
    for _ in range(TOP_K):
        mx = jnp.max(sel, axis=0, keepdims=True)
        ik = jnp.min(jnp.where(sel == mx, erow, float(n_experts)), axis=0, keepdims=True)
        oh = erow == ik
        masks.append(oh)
        idxs.append(ik)
        tops.append(jnp.sum(jnp.where(oh, scores, 0.0), axis=0, keepdims=True))
        sel = jnp.where(oh, -jnp.inf, sel)
    tsum = tops[0]
    for tk in tops[1:]:
        tsum = tsum + tk
    inv = ROUTED_SCALE / tsum

    ohf = jnp.zeros((n_experts, tm), F32)
    for oh in masks:
        ohf = jnp.where(oh, 1.0, ohf)
    r = lax.broadcasted_iota(I32, (tm, tm), 0)
    cc = lax.broadcasted_iota(I32, (tm, tm), 1)
    earlier = jnp.where(r < cc, 1.0, 0.0).astype(BF16)
    cnt = cnt_sc[...][:, 0:1]
    before = jnp.dot(ohf.astype(BF16), earlier, preferred_element_type=F32) + cnt
    cnt_new = cnt + jnp.sum(ohf, axis=1, keepdims=True)
    cnt_sc[...] = jnp.broadcast_to(cnt_new, cnt_sc.shape)
    cnt_ref[...] = jnp.broadcast_to(cnt_new, cnt_ref.shape).astype(I32)

    row_k = lax.broadcasted_iota(I32, (TOP_K, tm), 0)
    idx_o = jnp.zeros((TOP_K, tm), I32)
    wts_o = jnp.zeros((TOP_K, tm), F32)
    pos_o = jnp.zeros((TOP_K, tm), I32)
    for kk in range(TOP_K):
        pk = jnp.sum(jnp.where(masks[kk], before, 0.0), axis=0, keepdims=True)
        hit = row_k == kk
        idx_o = jnp.where(hit, idxs[kk].astype(I32), idx_o)
        wts_o = jnp.where(hit, tops[kk] * inv, wts_o)
        pos_o = jnp.where(hit, pk.astype(I32), pos_o)
    idx_ref[0] = idx_o
    wts_ref[0] = wts_o
    pos_ref[0] = pos_o


def _postmix(xc, ohg, omla, mod3, layer, n_batch, nct, wohg, womla, ln1w, ln1b, rw, rb, sw1, sw3, sw2, alpha):
    b, t, d = xc.shape
    tm = TOK_TILE
    n_experts = rw.shape[1]
    rwt = rw.T
    rwh = rwt.astype(BF16)
    rwl = (rwt - rwh.astype(F32)).astype(BF16)
    rb = jnp.broadcast_to(rb.reshape(n_experts, 1), (n_experts, LANE))
    full = lambda a: pl.BlockSpec(a.shape, lambda bi, ti: tuple(0 for _ in a.shape))
    tok = lambda w: pl.BlockSpec((1, tm, w), lambda bi, ti: (bi, ti, 0))
    out_shape = (
        jax.ShapeDtypeStruct((b, t, d), F32),
        jax.ShapeDtypeStruct((b, t, d // 2), U32),
        jax.ShapeDtypeStruct((b, t, d), BF16),
        jax.ShapeDtypeStruct((b, TOP_K, t), I32),
        jax.ShapeDtypeStruct((b, TOP_K, t), F32),
        jax.ShapeDtypeStruct((b, TOP_K, t), I32),
        jax.ShapeDtypeStruct((n_experts, LANE), I32),
    )
    kmaj = pl.BlockSpec((1, TOP_K, tm), lambda bi, ti: (bi, 0, ti))
    return pl.pallas_call(
        functools.partial(_postmix_kernel, d=d, alpha=alpha, n_experts=n_experts),
        out_shape=out_shape,
        grid=(b, t // tm),
        in_specs=[
            tok(d), tok(ohg.shape[2]), tok(omla.shape[2]),
            pl.BlockSpec((1, 1, N_MOD * d), lambda bi, ti: (layer * 8 + jnp.where(ti < nct, n_batch, bi), 0, 0)),
            full(wohg), full(womla), full(ln1w), full(ln1b), full(rwh), full(rwl), full(rb),
            full(sw1), full(sw3), full(sw2),
        ],
        out_specs=(tok(d), tok(d // 2), tok(d), kmaj, kmaj, kmaj,
                   pl.BlockSpec((n_experts, LANE), lambda bi, ti: (0, 0))),
        scratch_shapes=[pltpu.VMEM((n_experts, LANE), F32)],
        compiler_params=_cparams(("arbitrary", "arbitrary")),
        name="postmix_router",
    )(xc, ohg, omla, mod3, wohg, womla, ln1w, ln1b, rwh, rwl, rb, sw1, sw3, sw2)


def _row_copy(src_ref, src_row, dst_ref, dst_row, sem):
    return pltpu.make_async_copy(src_ref.at[pl.ds(src_row, 1)], dst_ref.at[pl.ds(dst_row, 1)], sem)


def _dispatch_kernel(dest_ref, h_ref, xs_in_ref, xs_ref, sem):
    del xs_in_ref

    def issue(t, c):
        for kk in range(TOP_K):
            _row_copy(h_ref, t, xs_ref, dest_ref[0, 0, t * TOP_K + kk], sem).start(priority=kk % 2)
        return c

    lax.fori_loop(0, DISPATCH_TILE, issue, 0, unroll=2)

    def drain(t, c):
        for kk in range(TOP_K):
            _row_copy(h_ref, 0, xs_ref, 0, sem).wait()
        return c

    lax.fori_loop(0, DISPATCH_TILE, drain, 0, unroll=2)


def _dispatch(dest, h2p, n_slots):
    n, w = h2p.shape
    steps = n // DISPATCH_TILE
    dest3 = dest.reshape(steps, 1, DISPATCH_TILE * TOP_K)
    xs0 = jnp.zeros((n_slots, w), U32)
    return pl.pallas_call(
        _dispatch_kernel,
        out_shape=jax.ShapeDtypeStruct((n_slots, w), U32),
        grid=(steps,),
        in_specs=[
            pl.BlockSpec((1, 1, DISPATCH_TILE * TOP_K), lambda i: (i, 0, 0), memory_space=pltpu.SMEM),
            pl.BlockSpec((DISPATCH_TILE, w), lambda i: (i, 0)),
            pl.BlockSpec(memory_space=pl.ANY),
        ],
        out_specs=pl.BlockSpec(memory_space=pl.ANY),
        scratch_shapes=[pltpu.SemaphoreType.DMA],
        input_output_aliases={2: 0},
        compiler_params=_cparams(("arbitrary",)),
        name="moe_dispatch",
    )(dest3, h2p, xs0)


def _expert_kernel(be_ref, nused_ref, xs_ref, w1_ref, w3_ref, w2_ref, ys_ref, w1b, w3b, w2b):
    i = pl.program_id(0)
    used = i < nused_ref[0]

    @pl.when(used)
    def _():
        changed = (i == 0) | (be_ref[i] != be_ref[jnp.maximum(i - 1, 0)])

        @pl.when(changed)
        def _():
            w1b[...] = w1_ref[0, 0].astype(BF16)
            w3b[...] = w3_ref[0, 0].astype(BF16)
            w2b[...] = w2_ref[0, 0].astype(BF16)

        lo, hi = _unpack_bf16_pair(xs_ref[...])
        x = jnp.concatenate([lo.astype(BF16), hi.astype(BF16)], axis=1)
        h1 = jnp.dot(x, w1b[...], preferred_element_type=F32)
        h3 = jnp.dot(x, w3b[...], preferred_element_type=F32)
        hid = (_silu(h1) * h3).astype(BF16)
        out = jnp.dot(hid, w2b[...], preferred_element_type=F32)
        half = out.shape[1] // 2
        ys_ref[...] = _pack_bf16_pair(out[:, 0:half], out[:, half:])

    @pl.when(jnp.logical_not(used))
    def _():
        ys_ref[...] = jnp.zeros_like(ys_ref)


def _experts(xs, block_expert, n_used, w1, w3, w2, layer):
    n_slots, w = xs.shape
    n_blocks = n_slots // ROW_BLOCK
    _, _, d, hid = w1.shape
    grid_spec = pltpu.PrefetchScalarGridSpec(
        num_scalar_prefetch=2,
        grid=(n_blocks,),
        in_specs=[
            pl.BlockSpec((ROW_BLOCK, w), lambda i, be, nu: (i, 0)),
            pl.BlockSpec((1, 1, d, hid), lambda i, be, nu: (layer, be[i], 0, 0)),
            pl.BlockSpec((1, 1, d, hid), lambda i, be, nu: (layer, be[i], 0, 0)),
            pl.BlockSpec((1, 1, hid, d), lambda i, be, nu: (layer, be[i], 0, 0)),
        ],
        out_specs=pl.BlockSpec((ROW_BLOCK, w), lambda i, be, nu: (i, 0)),
        scratch_shapes=[pltpu.VMEM((d, hid), BF16), pltpu.VMEM((d, hid), BF16), pltpu.VMEM((hid, d), BF16)],
    )
    return pl.pallas_call(
        _expert_kernel,
        out_shape=jax.ShapeDtypeStruct((n_slots, w), U32),
        grid_spec=grid_spec,
        compiler_params=_cparams(("arbitrary",)),
        name="moe_experts",
    )(block_expert, n_used, xs, w1, w3, w2)


def _combine_kernel(dest_ref, dnext_ref, wts_ref, ys_ref, sh_ref, x1_ref, mod_ref, lnw_ref, lnb_ref, o_ref,
                    buf, sems, *, d, alpha, n_steps):
    g = pl.program_id(0) * pl.num_programs(1) + pl.program_id(1)
    cur = g % 2
    nxt = 1 - cur

    def issue_tile(tbl_ref, slot):
        def body(t, c):
            for kk in range(TOP_K):
                pltpu.make_async_copy(ys_ref.at[pl.ds(tbl_ref[0, 0, t * TOP_K + kk], 1)],
                                      buf.at[slot, kk, pl.ds(t, 1)], sems.at[slot]).start(priority=kk % 2)
            return c
        lax.fori_loop(0, COMBINE_TILE, body, 0, unroll=2)

    def drain(slot):
        def body(t, c):
            for _ in range(TOP_K):
                pltpu.make_async_copy(ys_ref.at[pl.ds(0, 1)], buf.at[slot, 0, pl.ds(0, 1)], sems.at[slot]).wait()
            return c
        lax.fori_loop(0, COMBINE_TILE, body, 0, unroll=2)

    @pl.when(g == 0)
    def _():
        issue_tile(dest_ref, 0)

    issue_tile(dnext_ref, nxt)
    drain(cur)

    hw = d // 2
    sh = sh_ref[0].astype(F32)
    acc_lo = sh[:, 0:hw]
    acc_hi = sh[:, hw:d]
    wts = wts_ref[0]
    for kk in range(TOP_K):
        lo, hi = _unpack_bf16_pair(buf[cur, kk])
        wk = wts[:, kk:kk + 1]
        acc_lo = acc_lo + wk * lo
        acc_hi = acc_hi + wk * hi
    ff = jnp.concatenate([acc_lo, acc_hi], axis=1)
    g_f = mod_ref[0][:, 5 * d:6 * d]
    o_ref[0] = _normalize(alpha * x1_ref[0] + g_f * ff) * lnw_ref[...] + lnb_ref[...]

    @pl.when(g == n_steps - 1)
    def _():
        drain(nxt)


def _combine(dest, wts, ys, shared, x1, mod3, layer, n_batch, nct_tiles, lnw, lnb, alpha):
    b, t, d = x1.shape
    tm = COMBINE_TILE
    tpb = t // tm
    n_steps = b * tpb
    dest3 = dest.reshape(n_steps, 1, tm * TOP_K)
    tok = lambda w: pl.BlockSpec((1, tm, w), lambda bi, ti: (bi, ti, 0))
    full = lambda a: pl.BlockSpec(a.shape, lambda bi, ti: tuple(0 for _ in a.shape))
    table = lambda shift: pl.BlockSpec(
        (1, 1, tm * TOP_K), lambda bi, ti: (jnp.minimum(bi * tpb + ti + shift, n_steps - 1), 0, 0),
        memory_space=pltpu.SMEM)
    return pl.pallas_call(
        functools.partial(_combine_kernel, d=d, alpha=alpha, n_steps=n_steps),
        out_shape=jax.ShapeDtypeStruct((b, t, d), F32),
        grid=(b, tpb),
        in_specs=[
            table(0), table(1),
            tok(TOP_K),
            pl.BlockSpec(memory_space=pl.ANY),
            tok(d), tok(d),
            pl.BlockSpec((1, 1, N_MOD * d), lambda bi, ti: (layer * 8 + jnp.where(ti < nct_tiles, n_batch, bi), 0, 0)),
            full(lnw), full(lnb),
        ],
        out_specs=tok(d),
        scratch_shapes=[pltpu.VMEM((2, TOP_K, tm, d // 2), U32), pltpu.SemaphoreType.DMA((2,))],
        compiler_params=_cparams(("arbitrary", "arbitrary")),
        name="moe_combine",
    )(dest3, dest3, wts, ys, shared, x1, mod3, lnw, lnb)


def _rope_tables(n_ctx, n_lat):
    pos = jnp.arange(n_lat, dtype=I32)
    rowp = (pos // GRID_W).astype(F32)
    colp = (pos % GRID_W).astype(F32)
    n_freq = MLA_ROPE // 4
    inv = ROPE_BASE ** (-jnp.arange(n_freq, dtype=F32) / n_freq)
    ang = jnp.concatenate([rowp[:, None] * inv, colp[:, None] * inv], axis=-1)
    ang = jnp.concatenate([jnp.zeros((n_ctx, MLA_ROPE // 2), F32), ang], axis=0)
    t = n_ctx + n_lat
    ones = jnp.ones((t, MLA_NOPE), F32)
    zeros_tail = jnp.zeros((t, LANE - MLA_NOPE - MLA_ROPE), F32)
    cos_t = jnp.concatenate([ones, jnp.cos(ang), jnp.cos(ang), zeros_tail], axis=1)
    sin_t = jnp.concatenate([jnp.zeros((t, MLA_NOPE), F32), jnp.sin(ang), jnp.sin(ang), zeros_tail], axis=1)
    return cos_t, sin_t


def _rope_swap(w):
    half = MLA_ROPE // 2
    return jnp.concatenate([-w[..., half:], w[..., :half]], axis=-1)


def _prep_layer_weights(w_in_l, w_uq_l, w_ukv_l, w_out_l, hw, q_lora, kv_lora, n_heads):
    d = w_in_l.shape[0]
    base = 5 * hw + q_lora + kv_lora
    kpe_w = w_in_l[:, base:base + MLA_ROPE]
    z_nope = jnp.zeros((d, MLA_NOPE), F32)
    z_tail = jnp.zeros((d, LANE - MLA_NOPE - MLA_ROPE), F32)
    w_ext = jnp.concatenate([w_in_l[:, :base], z_nope, kpe_w, z_tail, z_nope, _rope_swap(kpe_w), z_tail],
                            axis=1).astype(BF16)
    wq = w_uq_l.reshape(q_lora, n_heads, MLA_NOPE + MLA_ROPE)
    zq = jnp.zeros((q_lora, n_heads, LANE - MLA_NOPE - MLA_ROPE), F32)
    wqa = jnp.concatenate([wq, zq], axis=-1).reshape(q_lora, n_heads * LANE).astype(BF16)
    wqb = jnp.concatenate([jnp.zeros((q_lora, n_heads, MLA_NOPE), F32), _rope_swap(wq[..., MLA_NOPE:]), zq],
                          axis=-1).reshape(q_lora, n_heads * LANE).astype(BF16)
    wkv = w_ukv_l.reshape(kv_lora, n_heads, MLA_NOPE + MLA_V)
    wk = jnp.concatenate([wkv[..., :MLA_NOPE], jnp.zeros((kv_lora, n_heads, LANE - MLA_NOPE), F32)],
                         axis=-1).reshape(kv_lora, n_heads * LANE).astype(BF16)
    wv = jnp.concatenate([wkv[..., MLA_NOPE:], jnp.zeros((kv_lora, n_heads, LANE - MLA_V), F32)],
                         axis=-1).reshape(kv_lora, n_heads * LANE).astype(BF16)
    wohg = w_out_l[:hw].astype(BF16)
    wom = w_out_l[hw:].reshape(n_heads, MLA_V, d)
    womla = jnp.concatenate([wom, jnp.zeros((n_heads, LANE - MLA_V, d), F32)], axis=1)
    womla = womla.reshape(n_heads * LANE, d).astype(BF16)
    return w_ext, wqa, wqb, wk, wv, wohg, womla


def kernel(x, c, ctx, c_ctx, w_mod, b_mod, w_in, hg_lb, hg_norm_w, q_norm_w, w_uq, kv_norm_w, w_ukv, w_out,
           ln1_w, ln1_b, router_w, router_bias, moe_w1, moe_w3, moe_w2, shared_w1, shared_w3, shared_w2,
           ln2_w, ln2_b):
    bsz, n_lat, d = x.shape
    n_ctx = ctx.shape[1]
    depth = w_mod.shape[0]
    t = n_ctx + n_lat
    hw = d // 2
    q_lora = w_uq.shape[1]
    kv_lora = w_ukv.shape[1]
    n_heads = (d - hw) // MLA_V
    n_experts = router_w.shape[2]
    alpha = float((2 * depth) ** 0.25)
    assert n_ctx % TOK_TILE == 0 and n_lat % ATT_TK == 0 and bsz < 8
    assert n_ctx % ATT_TQ == 0 and (bsz * t) % DISPATCH_TILE == 0 and t % COMBINE_TILE == 0

    lb_all = jnp.cumsum(jax.nn.softmax(hg_lb.astype(F32), axis=0), axis=0)
    lb_all = lb_all - lb_all[:1]

    c_rows = jnp.concatenate([c, c_ctx[None, :], jnp.zeros((8 - bsz - 1, d), F32)], axis=0)
    mod = _modulation(c_rows, w_mod, b_mod)
    mod3 = mod.reshape(depth * 8, 1, N_MOD * d)

    cos_t, sin_t = _rope_tables(n_ctx, n_lat)
    xc = jnp.concatenate([ctx, x], axis=1)
    nct = n_ctx // TOK_TILE

    n_tok = bsz * t
    n_assign = n_tok * TOP_K
    n_blocks = -(-(n_assign + n_experts * (ROW_BLOCK - 1)) // ROW_BLOCK)
    n_slots = n_blocks * ROW_BLOCK

    for l in range(depth):
        w_ext, wqa, wqb, wk, wv, wohg, womla = _prep_layer_weights(
            w_in[l], w_uq[l], w_ukv[l], w_out[l], hw, q_lora, kv_lora, n_heads)
        (lf_fw, lf_bw, k_fw, k_bw, q_hg, v_hg, sg, q_rot, q_unrot, kk, vv) = _inproj(
            xc, mod3, l, bsz, nct, w_ext, lb_all[l][None, :], q_norm_w[l][None, :], kv_norm_w[l][None, :],
            wqa, wqb, wk, wv, cos_t, sin_t, n_heads)
        o_fw = _gla(lf_fw, k_fw, q_hg, v_hg, nct, reverse=False)
        o_hg = _gla(lf_bw, k_bw, q_hg, v_hg, nct, reverse=True, extra=(o_fw, sg, hg_norm_w[l][None, :]))
        o_mla = _attention(q_unrot, q_rot, kk, vv, n_ctx, n_heads)
        x1, h2p, shared, idx, wts, pos, counts = _postmix(
            xc, o_hg, o_mla, mod3, l, bsz, nct, wohg, womla, ln1_w[l][None, :], ln1_b[l][None, :],
            router_w[l], router_bias[l][None, :], shared_w1[l].astype(BF16), shared_w3[l].astype(BF16),
            shared_w2[l].astype(BF16), alpha)

        counts = counts[:, 0]
        idx = jnp.swapaxes(idx, 1, 2)
        pos = jnp.swapaxes(pos, 1, 2)
        wts = jnp.swapaxes(wts, 1, 2)
        padded = (counts + ROW_BLOCK - 1) // ROW_BLOCK * ROW_BLOCK
        pad_end = jnp.cumsum(padded)
        pad_start = pad_end - padded
        dest = (pad_start[idx.reshape(n_tok, TOP_K)] + pos.reshape(n_tok, TOP_K)).astype(I32)
        block_first_row = jnp.arange(n_blocks, dtype=I32) * ROW_BLOCK
        block_expert = jnp.minimum(
            jnp.sum((pad_end[None, :] <= block_first_row[:, None]).astype(I32), axis=1), n_experts - 1)
        n_used = (pad_end[-1:] // ROW_BLOCK).astype(I32)

        xs = _dispatch(dest, h2p.reshape(n_tok, hw), n_slots)
        ys = _experts(xs, block_expert, n_used, moe_w1, moe_w3, moe_w2, l)
        xc = _combine(dest, wts, ys, shared, x1, mod3, l, bsz, n_ctx // COMBINE_TILE,
                      ln2_w[l][None, :], ln2_b[l][None, :], alpha)
    return xc[:, n_ctx:, :]
```

```python
import functools

import jax
import jax.numpy as jnp
from jax import lax
from jax.experimental import pallas as pl
from jax.experimental.pallas import tpu as pltpu

F32 = jnp.float32
BF16 = jnp.bfloat16
U32 = jnp.uint32
I32 = jnp.int32
HIGHEST = lax.Precision.HIGHEST

HG_HEAD_DIM = 128
MLA_V = 64
MLA_NOPE = 64
MLA_ROPE = 32
GRID_W = 64
ROPE_BASE = 10000.0
TOP_K = 8
ROUTED_SCALE = 2.5
N_MOD = 6
LN_EPS = 1e-6
RMS_EPS = 1e-6

LANE = 128
TOK_TILE = 256
GLA_CHUNK = 128
GLA_SUB = 32
GLA_SAFE_EXPONENT = 80.0
ATT_TQ = 256
ATT_TK = 1024
ATT_HEADS = 2
LOG2_E = 1.4426950408889634
ROW_BLOCK = 512
DISPATCH_TILE = 256
COMBINE_TILE = 128
VMEM_LIMIT = 56 * 1024 * 1024
MASKED = -1e30


def _cparams(sem):
    return pltpu.CompilerParams(dimension_semantics=sem, vmem_limit_bytes=VMEM_LIMIT)


def _sigmoid(z):
    return 1.0 / (1.0 + jnp.exp(-z))


def _silu(z):
    return z * _sigmoid(z)


def _normalize(x):
    mu = jnp.mean(x, axis=-1, keepdims=True)
    xc = x - mu
    var = jnp.mean(xc * xc, axis=-1, keepdims=True)
    return xc * lax.rsqrt(var + LN_EPS)


def _pack_bf16_pair(lo, hi):
    lo_u = lax.bitcast_convert_type(lo.astype(BF16).astype(F32), U32) >> 16
    hi_u = lax.bitcast_convert_type(hi.astype(BF16).astype(F32), U32) & jnp.uint32(0xFFFF0000)
    return hi_u | lo_u


def _unpack_bf16_pair(u):
    lo = lax.bitcast_convert_type(u << 16, F32)
    hi = lax.bitcast_convert_type(u & jnp.uint32(0xFFFF0000), F32)
    return lo, hi


def _mod_kernel(c_ref, w_ref, b_ref, o_ref):
    c = c_ref[...]
    o_ref[0] = jnp.dot(_silu(c), w_ref[0], precision=HIGHEST, preferred_element_type=F32) + b_ref[0]


def _modulation(c_rows, w_mod, b_mod):
    depth, d, n = w_mod.shape
    tn = 1536
    return pl.pallas_call(
        _mod_kernel,
        out_shape=jax.ShapeDtypeStruct((depth, 8, n), F32),
        grid=(depth, n // tn),
        in_specs=[
            pl.BlockSpec((8, d), lambda l, j: (0, 0)),
            pl.BlockSpec((1, d, tn), lambda l, j: (l, 0, j)),
            pl.BlockSpec((1, 1, tn), lambda l, j: (l, 0, j)),
        ],
        out_specs=pl.BlockSpec((1, 8, tn), lambda l, j: (l, 0, j)),
        compiler_params=_cparams(("arbitrary", "arbitrary")),
        name="modulation",
    )(c_rows, w_mod, b_mod.reshape(depth, 1, n))


def _inproj_kernel(x_ref, mod_ref, w_ref, lb_ref, qnw_ref, kvnw_ref, wqa_ref, wqb_ref, wk_ref, wv_ref,
                   cos_ref, sin_ref,
                   lff_ref, lfb_ref, kf_ref, kb_ref, q_ref, v_ref, sg_ref, qr_ref, qu_ref, kk_ref, vv_ref,
                   *, d, hw, q_lora, kv_lora, n_heads):
    x = x_ref[0]
    m = mod_ref[0]
    shift = m[:, 0:d]
    scale = m[:, d:2 * d]
    h = (_normalize(x) * (1.0 + scale) + shift).astype(BF16)
    proj = jnp.dot(h, w_ref[...], preferred_element_type=F32)

    lb = lb_ref[...]

    def forget(z, lbd):
        f = lbd + (1.0 - lbd) * _sigmoid(z)
        return jnp.log(f), 1.0 - f

    lf, kd = forget(proj[:, 0:hw], lb[:, 0:hw])
    lff_ref[0] = lf
    kf_ref[0] = kd.astype(BF16)
    lf, kd = forget(proj[:, hw:2 * hw], lb[:, hw:2 * hw])
    lfb_ref[0] = lf
    kb_ref[0] = kd.astype(BF16)
    v_ref[0] = proj[:, 2 * hw:3 * hw].astype(BF16)
    q_ref[0] = (_silu(proj[:, 3 * hw:4 * hw]) * (HG_HEAD_DIM ** -0.5)).astype(BF16)
    sg_ref[0] = _silu(proj[:, 4 * hw:5 * hw]).astype(BF16)

    o = 5 * hw
    cq = proj[:, o:o + q_lora]
    ckv = proj[:, o + q_lora:o + q_lora + kv_lora]
    o2 = o + q_lora + kv_lora
    kpe_a = proj[:, o2:o2 + LANE]
    kpe_b = proj[:, o2 + LANE:o2 + 2 * LANE]

    cqn = (cq * lax.rsqrt(jnp.mean(cq * cq, axis=-1, keepdims=True) + RMS_EPS) * qnw_ref[...]).astype(BF16)
    ckvn = (ckv * lax.rsqrt(jnp.mean(ckv * ckv, axis=-1, keepdims=True) + RMS_EPS) * kvnw_ref[...]).astype(BF16)

    cos = cos_ref[...]
    sin = sin_ref[...]
    cos_h = jnp.concatenate([cos] * n_heads, axis=1)
    sin_h = jnp.concatenate([sin] * n_heads, axis=1)
    att_scale = (MLA_NOPE + MLA_ROPE) ** -0.5 * LOG2_E
    qa = jnp.dot(cqn, wqa_ref[...], preferred_element_type=F32)
    qb = jnp.dot(cqn, wqb_ref[...], preferred_element_type=F32)
    qu_ref[0] = (qa * att_scale).astype(BF16)
    qr_ref[0] = ((qa * cos_h + qb * sin_h) * att_scale).astype(BF16)

    kr = kpe_a * cos + kpe_b * sin
    kk = jnp.dot(ckvn, wk_ref[...], preferred_element_type=F32) + jnp.concatenate([kr] * n_heads, axis=1)
    kk_ref[0] = kk.astype(BF16)
    vv = jnp.dot(ckvn, wv_ref[...], preferred_element_type=F32)
    lane = lax.broadcasted_iota(I32, vv.shape, 1)
    vv = jnp.where((lane % LANE) == MLA_V, 1.0, vv)
    vv_ref[0] = vv.astype(BF16)


def _inproj(xc, mod3, layer, n_batch, nct, w_ext, lb, qnw, kvnw, wqa, wqb, wk, wv, cos_t, sin_t, n_heads):
    b, t, d = xc.shape
    hw = d // 2
    q_lora = wqa.shape[0]
    kv_lora = wk.shape[0]
    hp = n_heads * LANE
    tm = TOK_TILE
    full = lambda shape: pl.BlockSpec(shape, lambda bi, ti: tuple(0 for _ in shape))
    tok = lambda w: pl.BlockSpec((1, tm, w), lambda bi, ti: (bi, ti, 0))
    out_shape = (
        jax.ShapeDtypeStruct((b, t, hw), F32), jax.ShapeDtypeStruct((b, t, hw), F32),
        jax.ShapeDtypeStruct((b, t, hw), BF16), jax.ShapeDtypeStruct((b, t, hw), BF16),
        jax.ShapeDtypeStruct((b, t, hw), BF16), jax.ShapeDtypeStruct((b, t, hw), BF16),
        jax.ShapeDtypeStruct((b, t, hw), BF16),
        jax.ShapeDtypeStruct((b, t, hp), BF16), jax.ShapeDtypeStruct((b, t, hp), BF16),
        jax.ShapeDtypeStruct((b, t, hp), BF16), jax.ShapeDtypeStruct((b, t, hp), BF16),
    )
    return pl.pallas_call(
        functools.partial(_inproj_kernel, d=d, hw=hw, q_lora=q_lora, kv_lora=kv_lora, n_heads=n_heads),
        out_shape=out_shape,
        grid=(b, t // tm),
        in_specs=[
            tok(d),
            pl.BlockSpec((1, 1, N_MOD * d), lambda bi, ti: (layer * 8 + jnp.where(ti < nct, n_batch, bi), 0, 0)),
            full(w_ext.shape), full(lb.shape), full(qnw.shape), full(kvnw.shape),
            full(wqa.shape), full(wqb.shape), full(wk.shape), full(wv.shape),
            pl.BlockSpec((tm, LANE), lambda bi, ti: (ti, 0)),
            pl.BlockSpec((tm, LANE), lambda bi, ti: (ti, 0)),
        ],
        out_specs=tuple([tok(hw)] * 7 + [tok(hp)] * 4),
        compiler_params=_cparams(("arbitrary", "arbitrary")),
        name="inproj",
    )(xc, mod3, w_ext, lb, qnw, kvnw, wqa, wqb, wk, wv, cos_t, sin_t)


def _gla_kernel(*refs, reverse, n_chunks, n_heads, final):
    if final:
        lf_ref, k_ref, q_ref, v_ref, ofw_ref, sg_ref, nw_ref, o_ref, st_ref = refs
    else:
        lf_ref, k_ref, q_ref, v_ref, o_ref, st_ref = refs
    c, s = GLA_CHUNK, GLA_SUB
    n_sub = c // s

    @pl.when(pl.program_id(1) == 0)
    def _():
        st_ref[...] = jnp.zeros_like(st_ref)

    row = lax.broadcasted_iota(I32, (c, c), 0)
    col = lax.broadcasted_iota(I32, (c, c), 1)
    tri = jnp.where((row <= col) if reverse else (row >= col), 1.0, 0.0).astype(F32)
    sub_row = lax.broadcasted_iota(I32, (s, HG_HEAD_DIM), 0)
    ones = jnp.ones((HG_HEAD_DIM, HG_HEAD_DIM), BF16)

    nt = (((1,), (1,)), ((), ()))

    def sub_geometry(si):
        rs = slice(si * s, (si + 1) * s)
        if reverse:
            return rs, (slice((si + 1) * s, c), (si + 1) * s) if si < n_sub - 1 else (None, None)
        return rs, (slice(0, si * s), si * s - 1) if si > 0 else (None, None)

    def chunk_body(i, carry):
        ci = (n_chunks - 1 - i) if reverse else i
        r0 = pl.multiple_of(ci * c, c)
        rows = pl.ds(r0, c)
        bc_all = jnp.dot(tri, lf_ref[0, rows, :], precision=HIGHEST, preferred_element_type=F32)
        worst = None
        for si in range(n_sub):
            rs, (_, brow) = sub_geometry(si)
            far = bc_all[rs.start:rs.start + 1] if reverse else bc_all[rs.stop - 1:rs.stop]
            d = far if brow is None else far - bc_all[brow:brow + 1]
            worst = d if worst is None else jnp.minimum(worst, d)

        def load(h):
            sl = slice(h * HG_HEAD_DIM, (h + 1) * HG_HEAD_DIM)
            return (sl, bc_all[:, sl], k_ref[0, rows, sl].astype(F32), q_ref[0, rows, sl].astype(F32),
                    v_ref[0, rows, sl].astype(F32))

        def store(sl, o):
            if final:
                o = o + ofw_ref[0, rows, sl]
                o = o * lax.rsqrt(jnp.mean(o * o, axis=-1, keepdims=True) + RMS_EPS) * nw_ref[...]
                o_ref[0, rows, sl] = (o * sg_ref[0, rows, sl].astype(F32)).astype(o_ref.dtype)
            else:
                o_ref[0, rows, sl] = o

        def intra(h, fast):
            sl, bc, k, q, v = load(h)
            qhat = (q * jnp.exp(bc)).astype(BF16)
            o_state = lax.dot_general(qhat, st_ref[h].astype(BF16), nt, preferred_element_type=F32)
            pieces = []
            for si in range(n_sub):
                rs, (src, brow) = sub_geometry(si)
                bs, qs, ks, vs = bc[rs], q[rs], k[rs], v[rs]
                acc = o_state[rs]
                beta = jnp.zeros((1, HG_HEAD_DIM), F32) if brow is None else bc[brow:brow + 1]
                if fast:
                    src = slice(rs.start, c) if reverse else slice(0, rs.stop)
                if src is not None:
                    qi = (qs * jnp.exp(bs - beta)).astype(BF16)
                    ksrc = (k[src] * jnp.exp(beta - bc[src])).astype(BF16)
                    a = lax.dot_general(qi, ksrc, nt, preferred_element_type=F32)
                    if fast:
                        n_src = src.stop - src.start
                        r_i = lax.broadcasted_iota(I32, (s, n_src), 0)
                        c_i = lax.broadcasted_iota(I32, (s, n_src), 1)
                        keep = (c_i >= r_i) if reverse else (c_i <= r_i + rs.start)
                        a = jnp.where(keep, a, 0.0)
                    acc = acc + jnp.dot(a.astype(BF16), v[src].astype(BF16), preferred_element_type=F32)
                if not fast:
                    ws = []
                    for j in range(s):
                        mask = (sub_row <= j) if reverse else (sub_row >= j)
                        dlt = jnp.where(mask, bs - bs[j:j + 1], MASKED)
                        ws.append((jnp.exp(dlt) * qs * ks[j:j + 1]).astype(BF16))
                    sums = jnp.dot(jnp.concatenate(ws, axis=0), ones, preferred_element_type=F32)
                    for j in range(s):
                        acc = acc + sums[j * s:(j + 1) * s] * vs[j:j + 1]
                pieces.append(acc)
            store(sl, jnp.concatenate(pieces, axis=0))

        bounded = jnp.min(worst) > -GLA_SAFE_EXPONENT

        @pl.when(bounded)
        def _():
            for h in range(n_heads):
                intra(h, True)

        @pl.when(jnp.logical_not(bounded))
        def _():
            for h in range(n_heads):
                intra(h, False)

        for h in range(n_heads):
            sl, bc, k, _, v = load(h)
            tot = bc[0:1] if reverse else bc[c - 1:c]
            khat = (k * jnp.exp(tot - bc)).astype(BF16)
            st_ref[h] = st_ref[h] * jnp.exp(tot) + jnp.dot(v.T.astype(BF16), khat, preferred_element_type=F32)
        return carry

    lax.fori_loop(0, n_chunks, chunk_body, 0)


def _gla(lf, k, q, v, nct_blocks, reverse, extra=None):
    b, t, hw = lf.shape
    n_heads = hw // HG_HEAD_DIM
    blk = TOK_TILE
    nb = t // blk

    def blk_index(g):
        if not reverse:
            return g
        return jnp.where(g < nct_blocks, nct_blocks - 1 - g, nb - 1 - (g - nct_blocks))

    spec = pl.BlockSpec((1, blk, hw), lambda bi, g: (bi, blk_index(g), 0))
    in_specs = [spec, spec, spec, spec]
    args = [lf, k, q, v]
    final = extra is not None
    if final:
        ofw, sg, nw = extra
        in_specs += [spec, spec, pl.BlockSpec(nw.shape, lambda bi, g: (0, 0))]
        args += [ofw, sg, nw]
    return pl.pallas_call(
        functools.partial(_gla_kernel, reverse=reverse, n_chunks=blk // GLA_CHUNK, n_heads=n_heads, final=final),
        out_shape=jax.ShapeDtypeStruct((b, t, hw), BF16 if final else F32),
        grid=(b, nb),
        in_specs=in_specs,
        out_specs=spec,
        scratch_shapes=[pltpu.VMEM((n_heads, HG_HEAD_DIM, HG_HEAD_DIM), F32)],
        compiler_params=_cparams(("arbitrary", "arbitrary")),
        name="gla_bwd" if reverse else "gla_fwd",
    )(*args)


def _attn_kernel(qu_ref, qr_ref, k_ref, v_ref, o_ref, s_sc, *, n_ctx, n_lat):
    is_ctx = pl.program_id(2) < n_ctx // ATT_TQ
    nt = (((1,), (1,)), ((), ()))
    n_chunks = n_lat // ATT_TK
    heads = [slice(h * LANE, (h + 1) * LANE) for h in range(ATT_HEADS)]

    def ctx_scores(hs):
        return lax.dot_general(qu_ref[0, :, hs], k_ref[0, 0:n_ctx, hs], nt, preferred_element_type=F32)

    def finish(hs, acc):
        o_ref[0, :, hs] = (acc * (1.0 / acc[:, MLA_V:MLA_V + 1])).astype(o_ref.dtype)

    def lane_tile_max(s, m):
        for c in range(s.shape[1] // LANE):
            t = s[:, c * LANE:(c + 1) * LANE]
            m = t if m is None else jnp.maximum(m, t)
        return m

    @pl.when(is_ctx)
    def _():
        for hs in heads:
            s = ctx_scores(hs)
            p = jnp.exp2(s - jnp.max(s, axis=-1, keepdims=True))
            finish(hs, jnp.dot(p.astype(BF16), v_ref[0, 0:n_ctx, hs], preferred_element_type=F32))

    @pl.when(jnp.logical_not(is_ctx))
    def _():
        ms, accs = [], []
        for hi, hs in enumerate(heads):
            qr = qr_ref[0, :, hs]
            s_c = ctx_scores(hs)
            m_t = lane_tile_max(s_c, None)
            for j in range(n_chunks):
                rows = slice(n_ctx + j * ATT_TK, n_ctx + (j + 1) * ATT_TK)
                s = lax.dot_general(qr, k_ref[0, rows, hs], nt, preferred_element_type=F32)
                s_sc[hi, j] = s
                m_t = lane_tile_max(s, m_t)
            m = jnp.max(m_t, axis=-1, keepdims=True)
            ms.append(m)
            accs.append(jnp.dot(jnp.exp2(s_c - m).astype(BF16), v_ref[0, 0:n_ctx, hs],
                                preferred_element_type=F32))

        for j in range(n_chunks):
            rows = slice(n_ctx + j * ATT_TK, n_ctx + (j + 1) * ATT_TK)
            for hi, hs in enumerate(heads):
                p = jnp.exp2(s_sc[hi, j] - ms[hi]).astype(BF16)
                accs[hi] = accs[hi] + jnp.dot(p, v_ref[0, rows, hs], preferred_element_type=F32)
        for hi, hs in enumerate(heads):
            finish(hs, accs[hi])


def _attention(qu, qr, kk, vv, n_ctx, n_heads):
    b, t, hp = qu.shape
    n_lat = t - n_ctx
    w = ATT_HEADS * LANE
    qspec = pl.BlockSpec((1, ATT_TQ, w), lambda bi, h, qi: (bi, qi, h))
    kspec = pl.BlockSpec((1, t, w), lambda bi, h, qi: (bi, 0, h))
    return pl.pallas_call(
        functools.partial(_attn_kernel, n_ctx=n_ctx, n_lat=n_lat),
        out_shape=jax.ShapeDtypeStruct((b, t, hp), BF16),
        grid=(b, n_heads // ATT_HEADS, t // ATT_TQ),
        in_specs=[qspec, qspec, kspec, kspec],
        out_specs=qspec,
        scratch_shapes=[pltpu.VMEM((ATT_HEADS, n_lat // ATT_TK, ATT_TQ, ATT_TK), F32)],
        compiler_params=_cparams(("arbitrary", "arbitrary", "arbitrary")),
        name="mla_attention",
    )(qu, qr, kk, vv)


def _postmix_kernel(x_ref, ohg_ref, omla_ref, mod_ref, wohg_ref, womla_ref, ln1w_ref, ln1b_ref,
                    rwh_ref, rwl_ref, rb_ref, sw1_ref, sw3_ref, sw2_ref,
                    x1_ref, h2p_ref, sh_ref, idx_ref, wts_ref, pos_ref, cnt_ref, cnt_sc,
                    *, d, alpha, n_experts):
    @pl.when((pl.program_id(0) == 0) & (pl.program_id(1) == 0))
    def _():
        cnt_sc[...] = jnp.zeros_like(cnt_sc)

    m = mod_ref[0]
    g_a = m[:, 2 * d:3 * d]
    sh_f = m[:, 3 * d:4 * d]
    sc_f = m[:, 4 * d:5 * d]
    mix = (jnp.dot(ohg_ref[0], wohg_ref[...], preferred_element_type=F32)
           + jnp.dot(omla_ref[0], womla_ref[...], preferred_element_type=F32))
    x1 = _normalize(alpha * x_ref[0] + g_a * mix) * ln1w_ref[...] + ln1b_ref[...]
    x1_ref[0] = x1
    h2 = _normalize(x1) * (1.0 + sc_f) + sh_f
    hw = d // 2
    h2p_ref[0] = _pack_bf16_pair(h2[:, 0:hw], h2[:, hw:d])
    h2b = h2.astype(BF16)

    a1 = jnp.dot(h2b, sw1_ref[...], preferred_element_type=F32)
    a3 = jnp.dot(h2b, sw3_ref[...], preferred_element_type=F32)
    hid = (_silu(a1) * a3).astype(BF16)
    sh_ref[0] = jnp.dot(hid, sw2_ref[...], preferred_element_type=F32).astype(sh_ref.dtype)

    nt = (((1,), (1,)), ((), ()))
    h2lo = (h2 - h2b.astype(F32)).astype(BF16)
    logits = (lax.dot_general(rwh_ref[...], h2b, nt, preferred_element_type=F32)
              + lax.dot_general(rwh_ref[...], h2lo, nt, preferred_element_type=F32)
              + lax.dot_general(rwl_ref[...], h2b, nt, preferred_element_type=F32))
    scores = _sigmoid(logits)
    sel = scores + rb_ref[...][:, 0:1]
    tm = scores.shape[1]
    erow = lax.broadcasted_iota(I32, (n_experts, tm), 0).astype(F32)
    masks, tops, idxs = [], [], []
    for _ in range(TOP_K):
        mx = jnp.max(sel, axis=0, keepdims=True)
        ik = jnp.min(jnp.where(sel == mx, erow, float(n_experts)), axis=0, keepdims=True)
        oh = erow == ik
        masks.append(oh)
        idxs.append(ik)
        tops.append(jnp.sum(jnp.where(oh, scores, 0.0), axis=0, keepdims=True))
        sel = jnp.where(oh, -jnp.inf, sel)
    tsum = tops[0]
    for tk in tops[1:]:
        tsum = tsum + tk
    inv = ROUTED_SCALE / tsum

    ohf = jnp.zeros((n_experts, tm), F32)
    for oh in masks:
        ohf = jnp.where(oh, 1.0, ohf)
    r = lax.broadcasted_iota(I32, (tm, tm), 0)
    cc = lax.broadcasted_iota(I32, (tm, tm), 1)
    earlier = jnp.where(r < cc, 1.0, 0.0).astype(BF16)
    cnt = cnt_sc[...][:, 0:1]
    before = jnp.dot(ohf.astype(BF16), earlier, preferred_element_type=F32) + cnt
    cnt_new = cnt + jnp.sum(ohf, axis=1, keepdims=True)
    cnt_sc[...] = jnp.broadcast_to(cnt_new, cnt_sc.shape)
    cnt_ref[...] = jnp.broadcast_to(cnt_new, cnt_ref.shape).astype(I32)

    row_k = lax.broadcasted_iota(I32, (TOP_K, tm), 0)
    idx_o = jnp.zeros((TOP_K, tm), I32)
    wts_o = jnp.zeros((TOP_K, tm), F32)
    pos_o = jnp.zeros((TOP_K, tm), I32)
    for kk in range(TOP_K):
        pk = jnp.sum(jnp.where(masks[kk], before, 0.0), axis=0, keepdims=True)
        hit = row_k == kk
        idx_o = jnp.where(hit, idxs[kk].astype(I32), idx_o)
        wts_o = jnp.where(hit, tops[kk] * inv, wts_o)
        pos_o = jnp.where(hit, pk.astype(I32), pos_o)
    idx_ref[0] = idx_o
    wts_ref[0] = wts_o
    pos_ref[0] = pos_o


def _postmix(xc, ohg, omla, mod3, layer, n_batch, nct, wohg, womla, ln1w, ln1b, rw, rb, sw1, sw3, sw2, alpha):
    b, t, d = xc.shape
    tm = TOK_TILE
    n_experts = rw.shape[1]
    rwt = rw.T
    rwh = rwt.astype(BF16)
    rwl = (rwt - rwh.astype(F32)).astype(BF16)
    rb = jnp.broadcast_to(rb.reshape(n_experts, 1), (n_experts, LANE))
    full = lambda a: pl.BlockSpec(a.shape, lambda bi, ti: tuple(0 for _ in a.shape))
    tok = lambda w: pl.BlockSpec((1, tm, w), lambda bi, ti: (bi, ti, 0))
    out_shape = (
        jax.ShapeDtypeStruct((b, t, d), F32),
        jax.ShapeDtypeStruct((b, t, d // 2), U32),
        jax.ShapeDtypeStruct((b, t, d), BF16),
        jax.ShapeDtypeStruct((b, TOP_K, t), I32),
        jax.ShapeDtypeStruct((b, TOP_K, t), F32),
        jax.ShapeDtypeStruct((b, TOP_K, t), I32),
        jax.ShapeDtypeStruct((n_experts, LANE), I32),
    )
    kmaj = pl.BlockSpec((1, TOP_K, tm), lambda bi, ti: (bi, 0, ti))
    return pl.pallas_call(
        functools.partial(_postmix_kernel, d=d, alpha=alpha, n_experts=n_experts),
        out_shape=out_shape,
        grid=(b, t // tm),
        in_specs=[
            tok(d), tok(ohg.shape[2]), tok(omla.shape[2]),
            pl.BlockSpec((1, 1, N_MOD * d), lambda bi, ti: (layer * 8 + jnp.where(ti < nct, n_batch, bi), 0, 0)),
            full(wohg), full(womla), full(ln1w), full(ln1b), full(rwh), full(rwl), full(rb),
            full(sw1), full(sw3), full(sw2),
        ],
        out_specs=(tok(d), tok(d // 2), tok(d), kmaj, kmaj, kmaj,
                   pl.BlockSpec((n_experts, LANE), lambda bi, ti: (0, 0))),
        scratch_shapes=[pltpu.VMEM((n_experts, LANE), F32)],
        compiler_params=_cparams(("arbitrary", "arbitrary")),
        name="postmix_router",
    )(xc, ohg, omla, mod3, wohg, womla, ln1w, ln1b, rwh, rwl, rb, sw1, sw3, sw2)


def _row_copy(src_ref, src_row, dst_ref, dst_row, sem):
    return pltpu.make_async_copy(src_ref.at[pl.ds(src_row, 1)], dst_ref.at[pl.ds(dst_row, 1)], sem)


def _dispatch_kernel(pad_from_ref, pad_len_ref, dest_ref, h_ref, xs_ref, zeros, sem, zsem, *, n_experts):
    def for_pad_rows(act):
        def per_expert(e, c):
            first = pad_from_ref[e]

            def per_row(r, c2):
                act(_row_copy(zeros, 0, xs_ref, first + r, zsem))
                return c2

            return lax.fori_loop(0, pad_len_ref[e], per_row, c)
        lax.fori_loop(0, n_experts, per_expert, 0)

    @pl.when(pl.program_id(0) == 0)
    def _():
        zeros[...] = jnp.zeros_like(zeros)
        for_pad_rows(lambda cp: cp.start())

    def issue(t, c):
        for kk in range(TOP_K):
            _row_copy(h_ref, t, xs_ref, dest_ref[0, 0, t * TOP_K + kk], sem).start(priority=kk % 2)
        return c

    lax.fori_loop(0, DISPATCH_TILE, issue, 0, unroll=2)

    def drain(t, c):
        for kk in range(TOP_K):
            _row_copy(h_ref, 0, xs_ref, 0, sem).wait()
        return c

    lax.fori_loop(0, DISPATCH_TILE, drain, 0, unroll=2)

    @pl.when(pl.program_id(0) == 0)
    def _():
        for_pad_rows(lambda cp: cp.wait())


def _dispatch(dest, h2p, n_slots, pad_from, pad_len):
    n, w = h2p.shape
    steps = n // DISPATCH_TILE
    dest3 = dest.reshape(steps, 1, DISPATCH_TILE * TOP_K)
    grid_spec = pltpu.PrefetchScalarGridSpec(
        num_scalar_prefetch=2,
        grid=(steps,),
        in_specs=[
            pl.BlockSpec((1, 1, DISPATCH_TILE * TOP_K), lambda i, pf, pn: (i, 0, 0), memory_space=pltpu.SMEM),
            pl.BlockSpec((DISPATCH_TILE, w), lambda i, pf, pn: (i, 0)),
        ],
        out_specs=pl.BlockSpec(memory_space=pl.ANY),
        scratch_shapes=[pltpu.VMEM((8, w), U32), pltpu.SemaphoreType.DMA, pltpu.SemaphoreType.DMA],
    )
    return pl.pallas_call(
        functools.partial(_dispatch_kernel, n_experts=pad_from.shape[0]),
        out_shape=jax.ShapeDtypeStruct((n_slots, w), U32),
        grid_spec=grid_spec,
        compiler_params=_cparams(("arbitrary",)),
        name="moe_dispatch",
    )(pad_from, pad_len, dest3, h2p)


def _expert_kernel(be_ref, nused_ref, xs_ref, w1_ref, w3_ref, w2_ref, ys_ref, w1b, w3b, w2b):
    i = pl.program_id(0)
    used = i < nused_ref[0]

    @pl.when(used)
    def _():
        changed = (i == 0) | (be_ref[i] != be_ref[jnp.maximum(i - 1, 0)])

        @pl.when(changed)
        def _():
            w1b[...] = w1_ref[0, 0].astype(BF16)
            w3b[...] = w3_ref[0, 0].astype(BF16)
            w2b[...] = w2_ref[0, 0].astype(BF16)

        lo, hi = _unpack_bf16_pair(xs_ref[...])
        x = jnp.concatenate([lo.astype(BF16), hi.astype(BF16)], axis=1)
        h1 = jnp.dot(x, w1b[...], preferred_element_type=F32)
        h3 = jnp.dot(x, w3b[...], preferred_element_type=F32)
        hid = (_silu(h1) * h3).astype(BF16)
        out = jnp.dot(hid, w2b[...], preferred_element_type=F32)
        half = out.shape[1] // 2
        ys_ref[...] = _pack_bf16_pair(out[:, 0:half], out[:, half:])

    @pl.when(jnp.logical_not(used))
    def _():
        ys_ref[...] = jnp.zeros_like(ys_ref)


def _experts(xs, block_expert, n_used, w1, w3, w2, layer):
    n_slots, w = xs.shape
    n_blocks = n_slots // ROW_BLOCK
    _, _, d, hid = w1.shape
    grid_spec = pltpu.PrefetchScalarGridSpec(
        num_scalar_prefetch=2,
        grid=(n_blocks,),
        in_specs=[
            pl.BlockSpec((ROW_BLOCK, w), lambda i, be, nu: (jnp.minimum(i, nu[0] - 1), 0)),
            pl.BlockSpec((1, 1, d, hid), lambda i, be, nu: (layer, be[i], 0, 0)),
            pl.BlockSpec((1, 1, d, hid), lambda i, be, nu: (layer, be[i], 0, 0)),
            pl.BlockSpec((1, 1, hid, d), lambda i, be, nu: (layer, be[i], 0, 0)),
        ],
        out_specs=pl.BlockSpec((ROW_BLOCK, w), lambda i, be, nu: (i, 0)),
        scratch_shapes=[pltpu.VMEM((d, hid), BF16), pltpu.VMEM((d, hid), BF16), pltpu.VMEM((hid, d), BF16)],
    )
    return pl.pallas_call(
        _expert_kernel,
        out_shape=jax.ShapeDtypeStruct((n_slots, w), U32),
        grid_spec=grid_spec,
        compiler_params=_cparams(("arbitrary",)),
        name="moe_experts",
    )(block_expert, n_used, xs, w1, w3, w2)


def _combine_kernel(dest_ref, dnext_ref, wts_ref, ys_ref, sh_ref, x1_ref, mod_ref, lnw_ref, lnb_ref, o_ref,
                    buf, sems, *, d, alpha, n_steps):
    g = pl.program_id(0) * pl.num_programs(1) + pl.program_id(1)
    cur = g % 2
    nxt = 1 - cur

    def issue_tile(tbl_ref, slot):
        def body(t, c):
            for kk in range(TOP_K):
                pltpu.make_async_copy(ys_ref.at[pl.ds(tbl_ref[0, 0, t * TOP_K + kk], 1)],
                                      buf.at[slot, kk, pl.ds(t, 1)], sems.at[slot]).start(priority=kk % 2)
            return c
        lax.fori_loop(0, COMBINE_TILE, body, 0, unroll=2)

    def drain(slot):
        def body(t, c):
            for _ in range(TOP_K):
                pltpu.make_async_copy(ys_ref.at[pl.ds(0, 1)], buf.at[slot, 0, pl.ds(0, 1)], sems.at[slot]).wait()
            return c
        lax.fori_loop(0, COMBINE_TILE, body, 0, unroll=2)

    @pl.when(g == 0)
    def _():
        issue_tile(dest_ref, 0)

    issue_tile(dnext_ref, nxt)
    drain(cur)

    hw = d // 2
    sh = sh_ref[0].astype(F32)
    acc_lo = sh[:, 0:hw]
    acc_hi = sh[:, hw:d]
    wts = wts_ref[0]
    for kk in range(TOP_K):
        lo, hi = _unpack_bf16_pair(buf[cur, kk])
        wk = wts[:, kk:kk + 1]
        acc_lo = acc_lo + wk * lo
        acc_hi = acc_hi + wk * hi
    ff = jnp.concatenate([acc_lo, acc_hi], axis=1)
    g_f = mod_ref[0][:, 5 * d:6 * d]
    o_ref[0] = _normalize(alpha * x1_ref[0] + g_f * ff) * lnw_ref[...] + lnb_ref[...]

    @pl.when(g == n_steps - 1)
    def _():
        drain(nxt)


def _combine(dest, wts, ys, shared, x1, mod3, layer, n_batch, nct_tiles, lnw, lnb, alpha, latent_only):
    b, t, d = x1.shape
    tm = COMBINE_TILE
    tpb = t // tm
    n_steps = b * tpb
    if latent_only:
        out_rows = t - nct_tiles * tm
        out_spec = pl.BlockSpec((1, tm, d), lambda bi, ti: (bi, jnp.maximum(ti - nct_tiles, 0), 0))
    else:
        out_rows = t
        out_spec = pl.BlockSpec((1, tm, d), lambda bi, ti: (bi, ti, 0))
    dest3 = dest.reshape(n_steps, 1, tm * TOP_K)
    tok = lambda w: pl.BlockSpec((1, tm, w), lambda bi, ti: (bi, ti, 0))
    full = lambda a: pl.BlockSpec(a.shape, lambda bi, ti: tuple(0 for _ in a.shape))
    table = lambda shift: pl.BlockSpec(
        (1, 1, tm * TOP_K), lambda bi, ti: (jnp.minimum(bi * tpb + ti + shift, n_steps - 1), 0, 0),
        memory_space=pltpu.SMEM)
    return pl.pallas_call(
        functools.partial(_combine_kernel, d=d, alpha=alpha, n_steps=n_steps),
        out_shape=jax.ShapeDtypeStruct((b, out_rows, d), F32),
        grid=(b, tpb),
        in_specs=[
            table(0), table(1),
            tok(TOP_K),
            pl.BlockSpec(memory_space=pl.ANY),
            tok(d), tok(d),
            pl.BlockSpec((1, 1, N_MOD * d), lambda bi, ti: (layer * 8 + jnp.where(ti < nct_tiles, n_batch, bi), 0, 0)),
            full(lnw), full(lnb),
        ],
        out_specs=out_spec,
        scratch_shapes=[pltpu.VMEM((2, TOP_K, tm, d // 2), U32), pltpu.SemaphoreType.DMA((2,))],
        compiler_params=_cparams(("arbitrary", "arbitrary")),
        name="moe_combine",
    )(dest3, dest3, wts, ys, shared, x1, mod3, lnw, lnb)


def _rope_tables(n_ctx, n_lat):
    pos = jnp.arange(n_lat, dtype=I32)
    rowp = (pos // GRID_W).astype(F32)
    colp = (pos % GRID_W).astype(F32)
    n_freq = MLA_ROPE // 4
    inv = ROPE_BASE ** (-jnp.arange(n_freq, dtype=F32) / n_freq)
    ang = jnp.concatenate([rowp[:, None] * inv, colp[:, None] * inv], axis=-1)
    ang = jnp.concatenate([jnp.zeros((n_ctx, MLA_ROPE // 2), F32), ang], axis=0)
    t = n_ctx + n_lat
    ones = jnp.ones((t, MLA_NOPE), F32)
    zeros_tail = jnp.zeros((t, LANE - MLA_NOPE - MLA_ROPE), F32)
    cos_t = jnp.concatenate([ones, jnp.cos(ang), jnp.cos(ang), zeros_tail], axis=1)
    sin_t = jnp.concatenate([jnp.zeros((t, MLA_NOPE), F32), jnp.sin(ang), jnp.sin(ang), zeros_tail], axis=1)
    return cos_t, sin_t


def _rope_swap(w):
    half = MLA_ROPE // 2
    return jnp.concatenate([-w[..., half:], w[..., :half]], axis=-1)


def _prep_layer_weights(w_in_l, w_uq_l, w_ukv_l, w_out_l, hw, q_lora, kv_lora, n_heads):
    d = w_in_l.shape[0]
    base = 5 * hw + q_lora + kv_lora
    kpe_w = w_in_l[:, base:base + MLA_ROPE]
    z_nope = jnp.zeros((d, MLA_NOPE), F32)
    z_tail = jnp.zeros((d, LANE - MLA_NOPE - MLA_ROPE), F32)
    w_ext = jnp.concatenate([w_in_l[:, :base], z_nope, kpe_w, z_tail, z_nope, _rope_swap(kpe_w), z_tail],
                            axis=1).astype(BF16)
    wq = w_uq_l.reshape(q_lora, n_heads, MLA_NOPE + MLA_ROPE)
    zq = jnp.zeros((q_lora, n_heads, LANE - MLA_NOPE - MLA_ROPE), F32)
    wqa = jnp.concatenate([wq, zq], axis=-1).reshape(q_lora, n_heads * LANE).astype(BF16)
    wqb = jnp.concatenate([jnp.zeros((q_lora, n_heads, MLA_NOPE), F32), _rope_swap(wq[..., MLA_NOPE:]), zq],
                          axis=-1).reshape(q_lora, n_heads * LANE).astype(BF16)
    wkv = w_ukv_l.reshape(kv_lora, n_heads, MLA_NOPE + MLA_V)
    wk = jnp.concatenate([wkv[..., :MLA_NOPE], jnp.zeros((kv_lora, n_heads, LANE - MLA_NOPE), F32)],
                         axis=-1).reshape(kv_lora, n_heads * LANE).astype(BF16)
    wv = jnp.concatenate([wkv[..., MLA_NOPE:], jnp.zeros((kv_lora, n_heads, LANE - MLA_V), F32)],
                         axis=-1).reshape(kv_lora, n_heads * LANE).astype(BF16)
    wohg = w_out_l[:hw].astype(BF16)
    wom = w_out_l[hw:].reshape(n_heads, MLA_V, d)
    womla = jnp.concatenate([wom, jnp.zeros((n_heads, LANE - MLA_V, d), F32)], axis=1)
    womla = womla.reshape(n_heads * LANE, d).astype(BF16)
    return w_ext, wqa, wqb, wk, wv, wohg, womla


def kernel(x, c, ctx, c_ctx, w_mod, b_mod, w_in, hg_lb, hg_norm_w, q_norm_w, w_uq, kv_norm_w, w_ukv, w_out,
           ln1_w, ln1_b, router_w, router_bias, moe_w1, moe_w3, moe_w2, shared_w1, shared_w3, shared_w2,
           ln2_w, ln2_b):
    bsz, n_lat, d = x.shape
    n_ctx = ctx.shape[1]
    depth = w_mod.shape[0]
    t = n_ctx + n_lat
    hw = d // 2
    q_lora = w_uq.shape[1]
    kv_lora = w_ukv.shape[1]
    n_heads = (d - hw) // MLA_V
    n_experts = router_w.shape[2]
    alpha = float((2 * depth) ** 0.25)
    assert n_ctx % TOK_TILE == 0 and n_lat % ATT_TK == 0 and bsz < 8
    assert n_ctx % ATT_TQ == 0 and (bsz * t) % DISPATCH_TILE == 0 and t % COMBINE_TILE == 0

    lb_all = jnp.cumsum(jax.nn.softmax(hg_lb.astype(F32), axis=0), axis=0)
    lb_all = lb_all - lb_all[:1]

    c_rows = jnp.concatenate([c, c_ctx[None, :], jnp.zeros((8 - bsz - 1, d), F32)], axis=0)
    mod = _modulation(c_rows, w_mod, b_mod)
    mod3 = mod.reshape(depth * 8, 1, N_MOD * d)

    cos_t, sin_t = _rope_tables(n_ctx, n_lat)
    xc = jnp.concatenate([ctx, x], axis=1)
    nct = n_ctx // TOK_TILE

    n_tok = bsz * t
    n_assign = n_tok * TOP_K
    n_blocks = -(-(n_assign + n_experts * (ROW_BLOCK - 1)) // ROW_BLOCK)
    n_slots = n_blocks * ROW_BLOCK

    for l in range(depth):
        w_ext, wqa, wqb, wk, wv, wohg, womla = _prep_layer_weights(
            w_in[l], w_uq[l], w_ukv[l], w_out[l], hw, q_lora, kv_lora, n_heads)
        (lf_fw, lf_bw, k_fw, k_bw, q_hg, v_hg, sg, q_rot, q_unrot, kk, vv) = _inproj(
            xc, mod3, l, bsz, nct, w_ext, lb_all[l][None, :], q_norm_w[l][None, :], kv_norm_w[l][None, :],
            wqa, wqb, wk, wv, cos_t, sin_t, n_heads)
        o_fw = _gla(lf_fw, k_fw, q_hg, v_hg, nct, reverse=False)
        o_hg = _gla(lf_bw, k_bw, q_hg, v_hg, nct, reverse=True, extra=(o_fw, sg, hg_norm_w[l][None, :]))
        o_mla = _attention(q_unrot, q_rot, kk, vv, n_ctx, n_heads)
        x1, h2p, shared, idx, wts, pos, counts = _postmix(
            xc, o_hg, o_mla, mod3, l, bsz, nct, wohg, womla, ln1_w[l][None, :], ln1_b[l][None, :],
            router_w[l], router_bias[l][None, :], shared_w1[l].astype(BF16), shared_w3[l].astype(BF16),
            shared_w2[l].astype(BF16), alpha)

        counts = counts[:, 0]
        idx = jnp.swapaxes(idx, 1, 2)
        pos = jnp.swapaxes(pos, 1, 2)
        wts = jnp.swapaxes(wts, 1, 2)
        padded = (counts + ROW_BLOCK - 1) // ROW_BLOCK * ROW_BLOCK
        pad_end = jnp.cumsum(padded)
        pad_start = pad_end - padded
        dest = (pad_start[idx.reshape(n_tok, TOP_K)] + pos.reshape(n_tok, TOP_K)).astype(I32)
        block_first_row = jnp.arange(n_blocks, dtype=I32) * ROW_BLOCK
        block_expert = jnp.minimum(
            jnp.sum((pad_end[None, :] <= block_first_row[:, None]).astype(I32), axis=1), n_experts - 1)
        n_used = (pad_end[-1:] // ROW_BLOCK).astype(I32)

        xs = _dispatch(dest, h2p.reshape(n_tok, hw), n_slots, (pad_start + counts).astype(I32),
                       (padded - counts).astype(I32))
        ys = _experts(xs, block_expert, n_used, moe_w1, moe_w3, moe_w2, l)
        xc = _combine(dest, wts, ys, shared, x1, mod3, l, bsz, n_ctx // COMBINE_TILE,
                      ln2_w[l][None, :], ln2_b[l][None, :], alpha, latent_only=(l == depth - 1))
    return xc
```

```python
import functools

import jax
import jax.numpy as jnp
from jax import lax
from jax.experimental import pallas as pl
from jax.experimental.pallas import tpu as pltpu

F32 = jnp.float32
BF16 = jnp.bfloat16
U32 = jnp.uint32
I32 = jnp.int32
HIGHEST = lax.Precision.HIGHEST

HG_HEAD_DIM = 128
MLA_V = 64
MLA_NOPE = 64
MLA_ROPE = 32
GRID_W = 64
ROPE_BASE = 10000.0
TOP_K = 8
ROUTED_SCALE = 2.5
N_MOD = 6
LN_EPS = 1e-6
RMS_EPS = 1e-6

LANE = 128
TOK_TILE = 256
GLA_CHUNK = 128
GLA_SUB = 32
GLA_SAFE_EXPONENT = 80.0
ATT_TQ = 256
ATT_TK = 1024
ATT_HEADS = 2
LOG2_E = 1.4426950408889634
ROW_BLOCK = 512
DISPATCH_TILE = 256
COMBINE_TILE = 128
VMEM_LIMIT = 56 * 1024 * 1024
MASKED = -1e30


def _cparams(sem):
    return pltpu.CompilerParams(dimension_semantics=sem, vmem_limit_bytes=VMEM_LIMIT)


def _sigmoid(z):
    return 1.0 / (1.0 + jnp.exp(-z))


def _silu(z):
    return z * _sigmoid(z)


def _normalize(x):
    mu = jnp.mean(x, axis=-1, keepdims=True)
    xc = x - mu
    var = jnp.mean(xc * xc, axis=-1, keepdims=True)
    return xc * lax.rsqrt(var + LN_EPS)


def _pack_bf16_pair(lo, hi):
    lo_u = lax.bitcast_convert_type(lo.astype(BF16).astype(F32), U32) >> 16
    hi_u = lax.bitcast_convert_type(hi.astype(BF16).astype(F32), U32) & jnp.uint32(0xFFFF0000)
    return hi_u | lo_u


def _unpack_bf16_pair(u):
    lo = lax.bitcast_convert_type(u << 16, F32)
    hi = lax.bitcast_convert_type(u & jnp.uint32(0xFFFF0000), F32)
    return lo, hi


def _mod_kernel(c_ref, w_ref, b_ref, o_ref):
    c = c_ref[...]
    o_ref[0] = jnp.dot(_silu(c), w_ref[0], precision=HIGHEST, preferred_element_type=F32) + b_ref[0]


def _modulation(c_rows, w_mod, b_mod):
    depth, d, n = w_mod.shape
    tn = 1536
    return pl.pallas_call(
        _mod_kernel,
        out_shape=jax.ShapeDtypeStruct((depth, 8, n), F32),
        grid=(depth, n // tn),
        in_specs=[
            pl.BlockSpec((8, d), lambda l, j: (0, 0)),
            pl.BlockSpec((1, d, tn), lambda l, j: (l, 0, j)),
            pl.BlockSpec((1, 1, tn), lambda l, j: (l, 0, j)),
        ],
        out_specs=pl.BlockSpec((1, 8, tn), lambda l, j: (l, 0, j)),
        compiler_params=_cparams(("arbitrary", "arbitrary")),
        name="modulation",
    )(c_rows, w_mod, b_mod.reshape(depth, 1, n))


def _inproj_kernel(x_ref, mod_ref, w_ref, lb_ref, qnw_ref, kvnw_ref, wqa_ref, wqb_ref, wk_ref, wv_ref,
                   cos_ref, sin_ref,
                   lff_ref, lfb_ref, kf_ref, kb_ref, q_ref, v_ref, sg_ref, qr_ref, qu_ref, kk_ref, vv_ref,
                   *, d, hw, q_lora, kv_lora, n_heads):
    x = x_ref[0]
    m = mod_ref[0]
    shift = m[:, 0:d]
    scale = m[:, d:2 * d]
    h = (_normalize(x) * (1.0 + scale) + shift).astype(BF16)
    proj = jnp.dot(h, w_ref[...], preferred_element_type=F32)

    lb = lb_ref[...]

    def forget(z, lbd):
        f = lbd + (1.0 - lbd) * _sigmoid(z)
        return jnp.log(f), 1.0 - f

    lf, kd = forget(proj[:, 0:hw], lb[:, 0:hw])
    lff_ref[0] = lf
    kf_ref[0] = kd.astype(BF16)
    lf, kd = forget(proj[:, hw:2 * hw], lb[:, hw:2 * hw])
    lfb_ref[0] = lf
    kb_ref[0] = kd.astype(BF16)
    v_ref[0] = proj[:, 2 * hw:3 * hw].astype(BF16)
    q_ref[0] = (_silu(proj[:, 3 * hw:4 * hw]) * (HG_HEAD_DIM ** -0.5)).astype(BF16)
    sg_ref[0] = _silu(proj[:, 4 * hw:5 * hw]).astype(BF16)

    o = 5 * hw
    cq = proj[:, o:o + q_lora]
    ckv = proj[:, o + q_lora:o + q_lora + kv_lora]
    o2 = o + q_lora + kv_lora
    kpe_a = proj[:, o2:o2 + LANE]
    kpe_b = proj[:, o2 + LANE:o2 + 2 * LANE]

    cqn = (cq * lax.rsqrt(jnp.mean(cq * cq, axis=-1, keepdims=True) + RMS_EPS) * qnw_ref[...]).astype(BF16)
    ckvn = (ckv * lax.rsqrt(jnp.mean(ckv * ckv, axis=-1, keepdims=True) + RMS_EPS) * kvnw_ref[...]).astype(BF16)

    cos = cos_ref[...]
    sin = sin_ref[...]
    cos_h = jnp.concatenate([cos] * n_heads, axis=1)
    sin_h = jnp.concatenate([sin] * n_heads, axis=1)
    att_scale = (MLA_NOPE + MLA_ROPE) ** -0.5 * LOG2_E
    qa = jnp.dot(cqn, wqa_ref[...], preferred_element_type=F32)
    qb = jnp.dot(cqn, wqb_ref[...], preferred_element_type=F32)
    qu_ref[0] = (qa * att_scale).astype(BF16)
    qr_ref[0] = ((qa * cos_h + qb * sin_h) * att_scale).astype(BF16)

    kr = kpe_a * cos + kpe_b * sin
    kk = jnp.dot(ckvn, wk_ref[...], preferred_element_type=F32) + jnp.concatenate([kr] * n_heads, axis=1)
    kk_ref[0] = kk.astype(BF16)
    vv = jnp.dot(ckvn, wv_ref[...], preferred_element_type=F32)
    lane = lax.broadcasted_iota(I32, vv.shape, 1)
    vv = jnp.where((lane % LANE) == MLA_V, 1.0, vv)
    vv_ref[0] = vv.astype(BF16)


def _inproj(xc, mod3, layer, n_batch, nct, w_ext, lb, qnw, kvnw, wqa, wqb, wk, wv, cos_t, sin_t, n_heads):
    b, t, d = xc.shape
    hw = d // 2
    q_lora = wqa.shape[0]
    kv_lora = wk.shape[0]
    hp = n_heads * LANE
    tm = TOK_TILE
    full = lambda shape: pl.BlockSpec(shape, lambda bi, ti: tuple(0 for _ in shape))
    tok = lambda w: pl.BlockSpec((1, tm, w), lambda bi, ti: (bi, ti, 0))
    out_shape = (
        jax.ShapeDtypeStruct((b, t, hw), F32), jax.ShapeDtypeStruct((b, t, hw), F32),
        jax.ShapeDtypeStruct((b, t, hw), BF16), jax.ShapeDtypeStruct((b, t, hw), BF16),
        jax.ShapeDtypeStruct((b, t, hw), BF16), jax.ShapeDtypeStruct((b, t, hw), BF16),
        jax.ShapeDtypeStruct((b, t, hw), BF16),
        jax.ShapeDtypeStruct((b, t, hp), BF16), jax.ShapeDtypeStruct((b, t, hp), BF16),
        jax.ShapeDtypeStruct((b, t, hp), BF16), jax.ShapeDtypeStruct((b, t, hp), BF16),
    )
    return pl.pallas_call(
        functools.partial(_inproj_kernel, d=d, hw=hw, q_lora=q_lora, kv_lora=kv_lora, n_heads=n_heads),
        out_shape=out_shape,
        grid=(b, t // tm),
        in_specs=[
            tok(d),
            pl.BlockSpec((1, 1, N_MOD * d), lambda bi, ti: (layer * 8 + jnp.where(ti < nct, n_batch, bi), 0, 0)),
            full(w_ext.shape), full(lb.shape), full(qnw.shape), full(kvnw.shape),
            full(wqa.shape), full(wqb.shape), full(wk.shape), full(wv.shape),
            pl.BlockSpec((tm, LANE), lambda bi, ti: (ti, 0)),
            pl.BlockSpec((tm, LANE), lambda bi, ti: (ti, 0)),
        ],
        out_specs=tuple([tok(hw)] * 7 + [tok(hp)] * 4),
        compiler_params=_cparams(("arbitrary", "arbitrary")),
        name="inproj",
    )(xc, mod3, w_ext, lb, qnw, kvnw, wqa, wqb, wk, wv, cos_t, sin_t)


def _gla_kernel(*refs, reverse, n_chunks, n_heads, final):
    if final:
        lf_ref, k_ref, q_ref, v_ref, ofw_ref, sg_ref, nw_ref, o_ref, st_ref = refs
    else:
        lf_ref, k_ref, q_ref, v_ref, o_ref, st_ref = refs
    c, s = GLA_CHUNK, GLA_SUB
    n_sub = c // s

    @pl.when(pl.program_id(1) == 0)
    def _():
        st_ref[...] = jnp.zeros_like(st_ref)

    row = lax.broadcasted_iota(I32, (c, c), 0)
    col = lax.broadcasted_iota(I32, (c, c), 1)
    tri = jnp.where((row <= col) if reverse else (row >= col), 1.0, 0.0).astype(F32)
    sub_row = lax.broadcasted_iota(I32, (s, HG_HEAD_DIM), 0)
    ones = jnp.ones((HG_HEAD_DIM, HG_HEAD_DIM), BF16)

    nt = (((1,), (1,)), ((), ()))

    def sub_geometry(si):
        rs = slice(si * s, (si + 1) * s)
        if reverse:
            return rs, (slice((si + 1) * s, c), (si + 1) * s) if si < n_sub - 1 else (None, None)
        return rs, (slice(0, si * s), si * s - 1) if si > 0 else (None, None)

    def chunk_body(i, carry):
        ci = (n_chunks - 1 - i) if reverse else i
        r0 = pl.multiple_of(ci * c, c)
        rows = pl.ds(r0, c)
        bc_all = jnp.dot(tri, lf_ref[0, rows, :], precision=HIGHEST, preferred_element_type=F32)
        worst = None
        for si in range(n_sub):
            rs, (_, brow) = sub_geometry(si)
            far = bc_all[rs.start:rs.start + 1] if reverse else bc_all[rs.stop - 1:rs.stop]
            d = far if brow is None else far - bc_all[brow:brow + 1]
            worst = d if worst is None else jnp.minimum(worst, d)

        def load(h):
            sl = slice(h * HG_HEAD_DIM, (h + 1) * HG_HEAD_DIM)
            return (sl, bc_all[:, sl], k_ref[0, rows, sl].astype(F32), q_ref[0, rows, sl].astype(F32),
                    v_ref[0, rows, sl].astype(F32))

        def store(sl, o):
            if final:
                o = o + ofw_ref[0, rows, sl]
                o = o * lax.rsqrt(jnp.mean(o * o, axis=-1, keepdims=True) + RMS_EPS) * nw_ref[...]
                o_ref[0, rows, sl] = (o * sg_ref[0, rows, sl].astype(F32)).astype(o_ref.dtype)
            else:
                o_ref[0, rows, sl] = o

        def intra(h, fast):
            sl, bc, k, q, v = load(h)
            qhat = (q * jnp.exp(bc)).astype(BF16)
            o_state = lax.dot_general(qhat, st_ref[h].astype(BF16), nt, preferred_element_type=F32)
            pieces = []
            for si in range(n_sub):
                rs, (src, brow) = sub_geometry(si)
                bs, qs, ks, vs = bc[rs], q[rs], k[rs], v[rs]
                acc = o_state[rs]
                beta = jnp.zeros((1, HG_HEAD_DIM), F32) if brow is None else bc[brow:brow + 1]
                if fast:
                    src = slice(rs.start, c) if reverse else slice(0, rs.stop)
                if src is not None:
                    qi = (qs * jnp.exp(bs - beta)).astype(BF16)
                    ksrc = (k[src] * jnp.exp(beta - bc[src])).astype(BF16)
                    a = lax.dot_general(qi, ksrc, nt, preferred_element_type=F32)
                    if fast:
                        n_src = src.stop - src.start
                        r_i = lax.broadcasted_iota(I32, (s, n_src), 0)
                        c_i = lax.broadcasted_iota(I32, (s, n_src), 1)
                        keep = (c_i >= r_i) if reverse else (c_i <= r_i + rs.start)
                        a = jnp.where(keep, a, 0.0)
                    acc = acc + jnp.dot(a.astype(BF16), v[src].astype(BF16), preferred_element_type=F32)
                if not fast:
                    ws = []
                    for j in range(s):
                        mask = (sub_row <= j) if reverse else (sub_row >= j)
                        dlt = jnp.where(mask, bs - bs[j:j + 1], MASKED)
                        ws.append((jnp.exp(dlt) * qs * ks[j:j + 1]).astype(BF16))
                    sums = jnp.dot(jnp.concatenate(ws, axis=0), ones, preferred_element_type=F32)
                    for j in range(s):
                        acc = acc + sums[j * s:(j + 1) * s] * vs[j:j + 1]
                pieces.append(acc)
            store(sl, jnp.concatenate(pieces, axis=0))

        bounded = jnp.min(worst) > -GLA_SAFE_EXPONENT

        @pl.when(bounded)
        def _():
            for h in range(n_heads):
                intra(h, True)

        @pl.when(jnp.logical_not(bounded))
        def _():
            for h in range(n_heads):
                intra(h, False)

        for h in range(n_heads):
            sl, bc, k, _, v = load(h)
            tot = bc[0:1] if reverse else bc[c - 1:c]
            khat = (k * jnp.exp(tot - bc)).astype(BF16)
            st_ref[h] = st_ref[h] * jnp.exp(tot) + jnp.dot(v.T.astype(BF16), khat, preferred_element_type=F32)
        return carry

    lax.fori_loop(0, n_chunks, chunk_body, 0)


def _gla(lf, k, q, v, nct_blocks, reverse, extra=None):
    b, t, hw = lf.shape
    n_heads = hw // HG_HEAD_DIM
    blk = TOK_TILE
    nb = t // blk

    def blk_index(g):
        if not reverse:
            return g
        return jnp.where(g < nct_blocks, nct_blocks - 1 - g, nb - 1 - (g - nct_blocks))

    spec = pl.BlockSpec((1, blk, hw), lambda bi, g: (bi, blk_index(g), 0))
    in_specs = [spec, spec, spec, spec]
    args = [lf, k, q, v]
    final = extra is not None
    if final:
        ofw, sg, nw = extra
        in_specs += [spec, spec, pl.BlockSpec(nw.shape, lambda bi, g: (0, 0))]
        args += [ofw, sg, nw]
    return pl.pallas_call(
        functools.partial(_gla_kernel, reverse=reverse, n_chunks=blk // GLA_CHUNK, n_heads=n_heads, final=final),
        out_shape=jax.ShapeDtypeStruct((b, t, hw), BF16 if final else F32),
        grid=(b, nb),
        in_specs=in_specs,
        out_specs=spec,
        scratch_shapes=[pltpu.VMEM((n_heads, HG_HEAD_DIM, HG_HEAD_DIM), F32)],
        compiler_params=_cparams(("arbitrary", "arbitrary")),
        name="gla_bwd" if reverse else "gla_fwd",
    )(*args)


def _attn_kernel(qu_ref, qr_ref, k_ref, v_ref, o_ref, s_sc, *, n_ctx, n_lat):
    is_ctx = pl.program_id(2) < n_ctx // ATT_TQ
    nt = (((1,), (1,)), ((), ()))
    n_chunks = n_lat // ATT_TK
    heads = [slice(h * LANE, (h + 1) * LANE) for h in range(ATT_HEADS)]

    def ctx_scores(hs):
        return lax.dot_general(qu_ref[0, :, hs], k_ref[0, 0:n_ctx, hs], nt, preferred_element_type=F32)

    def finish(hs, acc):
        o_ref[0, :, hs] = (acc * (1.0 / acc[:, MLA_V:MLA_V + 1])).astype(o_ref.dtype)

    def lane_tile_max(s, m):
        for c in range(s.shape[1] // LANE):
            t = s[:, c * LANE:(c + 1) * LANE]
            m = t if m is None else jnp.maximum(m, t)
        return m

    @pl.when(is_ctx)
    def _():
        for hs in heads:
            s = ctx_scores(hs)
            p = jnp.exp2(s - jnp.max(s, axis=-1, keepdims=True))
            finish(hs, jnp.dot(p.astype(BF16), v_ref[0, 0:n_ctx, hs], preferred_element_type=F32))

    @pl.when(jnp.logical_not(is_ctx))
    def _():
        ms, accs = [], []
        for hi, hs in enumerate(heads):
            qr = qr_ref[0, :, hs]
            s_c = ctx_scores(hs)
            m_t = lane_tile_max(s_c, None)
            for j in range(n_chunks):
                rows = slice(n_ctx + j * ATT_TK, n_ctx + (j + 1) * ATT_TK)
                s = lax.dot_general(qr, k_ref[0, rows, hs], nt, preferred_element_type=F32)
                s_sc[hi, j] = s
                m_t = lane_tile_max(s, m_t)
            m = jnp.max(m_t, axis=-1, keepdims=True)
            ms.append(m)
            accs.append(jnp.dot(jnp.exp2(s_c - m).astype(BF16), v_ref[0, 0:n_ctx, hs],
                                preferred_element_type=F32))

        for j in range(n_chunks):
            rows = slice(n_ctx + j * ATT_TK, n_ctx + (j + 1) * ATT_TK)
            for hi, hs in enumerate(heads):
                p = jnp.exp2(s_sc[hi, j] - ms[hi]).astype(BF16)
                accs[hi] = accs[hi] + jnp.dot(p, v_ref[0, rows, hs], preferred_element_type=F32)
        for hi, hs in enumerate(heads):
            finish(hs, accs[hi])


def _attention(qu, qr, kk, vv, n_ctx, n_heads):
    b, t, hp = qu.shape
    n_lat = t - n_ctx
    w = ATT_HEADS * LANE
    qspec = pl.BlockSpec((1, ATT_TQ, w), lambda bi, h, qi: (bi, qi, h))
    kspec = pl.BlockSpec((1, t, w), lambda bi, h, qi: (bi, 0, h))
    return pl.pallas_call(
        functools.partial(_attn_kernel, n_ctx=n_ctx, n_lat=n_lat),
        out_shape=jax.ShapeDtypeStruct((b, t, hp), BF16),
        grid=(b, n_heads // ATT_HEADS, t // ATT_TQ),
        in_specs=[qspec, qspec, kspec, kspec],
        out_specs=qspec,
        scratch_shapes=[pltpu.VMEM((ATT_HEADS, n_lat // ATT_TK, ATT_TQ, ATT_TK), F32)],
        compiler_params=_cparams(("arbitrary", "arbitrary", "arbitrary")),
        name="mla_attention",
    )(qu, qr, kk, vv)


def _postmix_kernel(x_ref, ohg_ref, omla_ref, mod_ref, wohg_ref, womla_ref, ln1w_ref, ln1b_ref,
                    rwh_ref, rwl_ref, rb_ref, sw1_ref, sw3_ref, sw2_ref,
                    x1_ref, h2p_ref, sh_ref, idx_ref, wts_ref, pos_ref, cnt_ref, cnt_sc,
                    *, d, alpha, n_experts):
    @pl.when((pl.program_id(0) == 0) & (pl.program_id(1) == 0))
    def _():
        cnt_sc[...] = jnp.zeros_like(cnt_sc)

    m = mod_ref[0]
    g_a = m[:, 2 * d:3 * d]
    sh_f = m[:, 3 * d:4 * d]
    sc_f = m[:, 4 * d:5 * d]
    mix = (jnp.dot(ohg_ref[0], wohg_ref[...], preferred_element_type=F32)
           + jnp.dot(omla_ref[0], womla_ref[...], preferred_element_type=F32))
    x1 = _normalize(alpha * x_ref[0] + g_a * mix) * ln1w_ref[...] + ln1b_ref[...]
    x1_ref[0] = x1
    h2 = _normalize(x1) * (1.0 + sc_f) + sh_f
    hw = d // 2
    h2p_ref[0] = _pack_bf16_pair(h2[:, 0:hw], h2[:, hw:d])
    h2b = h2.astype(BF16)

    a1 = jnp.dot(h2b, sw1_ref[...], preferred_element_type=F32)
    a3 = jnp.dot(h2b, sw3_ref[...], preferred_element_type=F32)
    hid = (_silu(a1) * a3).astype(BF16)
    sh_ref[0] = jnp.dot(hid, sw2_ref[...], preferred_element_type=F32).astype(sh_ref.dtype)

    nt = (((1,), (1,)), ((), ()))
    h2lo = (h2 - h2b.astype(F32)).astype(BF16)
    logits = (lax.dot_general(rwh_ref[...], h2b, nt, preferred_element_type=F32)
              + lax.dot_general(rwh_ref[...], h2lo, nt, preferred_element_type=F32)
              + lax.dot_general(rwl_ref[...], h2b, nt, preferred_element_type=F32))
    scores = _sigmoid(logits)
    sel = scores + rb_ref[...][:, 0:1]
    tm = scores.shape[1]
    erow = lax.broadcasted_iota(I32, (n_experts, tm), 0).astype(F32)
    masks, tops, idxs = [], [], []
    for _ in range(TOP_K):
        mx = jnp.max(sel, axis=0, keepdims=True)
        ik = jnp.min(jnp.where(sel == mx, erow, float(n_experts)), axis=0, keepdims=True)
        oh = erow == ik
        masks.append(oh)
        idxs.append(ik)
        tops.append(jnp.sum(jnp.where(oh, scores, 0.0), axis=0, keepdims=True))
        sel = jnp.where(oh, -jnp.inf, sel)
    tsum = tops[0]
    for tk in tops[1:]:
        tsum = tsum + tk
    inv = ROUTED_SCALE / tsum

    ohf = jnp.zeros((n_experts, tm), F32)
    for oh in masks:
        ohf = jnp.where(oh, 1.0, ohf)
    r = lax.broadcasted_iota(I32, (tm, tm), 0)
    cc = lax.broadcasted_iota(I32, (tm, tm), 1)
    earlier = jnp.where(r < cc, 1.0, 0.0).astype(BF16)
    cnt = cnt_sc[...][:, 0:1]
    before = jnp.dot(ohf.astype(BF16), earlier, preferred_element_type=F32) + cnt
    cnt_new = cnt + jnp.sum(ohf, axis=1, keepdims=True)
    cnt_sc[...] = jnp.broadcast_to(cnt_new, cnt_sc.shape)
    cnt_ref[...] = jnp.broadcast_to(cnt_new, cnt_ref.shape).astype(I32)

    row_k = lax.broadcasted_iota(I32, (TOP_K, tm), 0)
    idx_o = jnp.zeros((TOP_K, tm), I32)
    wts_o = jnp.zeros((TOP_K, tm), F32)
    pos_o = jnp.zeros((TOP_K, tm), I32)
    for kk in range(TOP_K):
        pk = jnp.sum(jnp.where(masks[kk], before, 0.0), axis=0, keepdims=True)
        hit = row_k == kk
        idx_o = jnp.where(hit, idxs[kk].astype(I32), idx_o)
        wts_o = jnp.where(hit, tops[kk] * inv, wts_o)
        pos_o = jnp.where(hit, pk.astype(I32), pos_o)
    idx_ref[0] = idx_o
    wts_ref[0] = wts_o
    pos_ref[0] = pos_o


def _postmix(xc, ohg, omla, mod3, layer, n_batch, nct, wohg, womla, ln1w, ln1b, rw, rb, sw1, sw3, sw2, alpha):
    b, t, d = xc.shape
    tm = TOK_TILE
    n_experts = rw.shape[1]
    rwt = rw.T
    rwh = rwt.astype(BF16)
    rwl = (rwt - rwh.astype(F32)).astype(BF16)
    rb = jnp.broadcast_to(rb.reshape(n_experts, 1), (n_experts, LANE))
    full = lambda a: pl.BlockSpec(a.shape, lambda bi, ti: tuple(0 for _ in a.shape))
    tok = lambda w: pl.BlockSpec((1, tm, w), lambda bi, ti: (bi, ti, 0))
    out_shape = (
        jax.ShapeDtypeStruct((b, t, d), F32),
        jax.ShapeDtypeStruct((b, t, d // 2), U32),
        jax.ShapeDtypeStruct((b, t, d), BF16),
        jax.ShapeDtypeStruct((b, TOP_K, t), I32),
        jax.ShapeDtypeStruct((b, TOP_K, t), F32),
        jax.ShapeDtypeStruct((b, TOP_K, t), I32),
        jax.ShapeDtypeStruct((n_experts, LANE), I32),
    )
    kmaj = pl.BlockSpec((1, TOP_K, tm), lambda bi, ti: (bi, 0, ti))
    return pl.pallas_call(
        functools.partial(_postmix_kernel, d=d, alpha=alpha, n_experts=n_experts),
        out_shape=out_shape,
        grid=(b, t // tm),
        in_specs=[
            tok(d), tok(ohg.shape[2]), tok(omla.shape[2]),
            pl.BlockSpec((1, 1, N_MOD * d), lambda bi, ti: (layer * 8 + jnp.where(ti < nct, n_batch, bi), 0, 0)),
            full(wohg), full(womla), full(ln1w), full(ln1b), full(rwh), full(rwl), full(rb),
            full(sw1), full(sw3), full(sw2),
        ],
        out_specs=(tok(d), tok(d // 2), tok(d), kmaj, kmaj, kmaj,
                   pl.BlockSpec((n_experts, LANE), lambda bi, ti: (0, 0))),
        scratch_shapes=[pltpu.VMEM((n_experts, LANE), F32)],
        compiler_params=_cparams(("arbitrary", "arbitrary")),
        name="postmix_router",
    )(xc, ohg, omla, mod3, wohg, womla, ln1w, ln1b, rwh, rwl, rb, sw1, sw3, sw2)


def _row_copy(src_ref, src_row, dst_ref, dst_row, sem):
    return pltpu.make_async_copy(src_ref.at[pl.ds(src_row, 1)], dst_ref.at[pl.ds(dst_row, 1)], sem)


def _dispatch_kernel(pad_from_ref, pad_len_ref, dest_ref, h_ref, xs_ref, zeros, sem, zsem, *, n_experts):
    tile_rows = zeros.shape[0]

    def for_pad_rows(act):
        def per_expert(e, c):
            first, n = pad_from_ref[e], pad_len_ref[e]
            head = jnp.minimum((-first) & (tile_rows - 1), n)

            def per_row(r, c2):
                act(_row_copy(zeros, 0, xs_ref, first + r, zsem))
                return c2

            def per_tile(i, c2):
                start = pl.multiple_of(first + head + i * tile_rows, tile_rows)
                act(pltpu.make_async_copy(zeros, xs_ref.at[pl.ds(start, tile_rows)], zsem))
                return c2

            c = lax.fori_loop(0, head, per_row, c)
            return lax.fori_loop(0, lax.shift_right_logical(n - head, tile_rows.bit_length() - 1), per_tile, c)
        lax.fori_loop(0, n_experts, per_expert, 0)

    @pl.when(pl.program_id(0) == 0)
    def _():
        zeros[...] = jnp.zeros_like(zeros)
        for_pad_rows(lambda cp: cp.start())

    def issue(t, c):
        for kk in range(TOP_K):
            _row_copy(h_ref, t, xs_ref, dest_ref[0, 0, t * TOP_K + kk], sem).start(priority=kk % 2)
        return c

    lax.fori_loop(0, DISPATCH_TILE, issue, 0, unroll=2)

    def drain(t, c):
        for kk in range(TOP_K):
            _row_copy(h_ref, 0, xs_ref, 0, sem).wait()
        return c

    lax.fori_loop(0, DISPATCH_TILE, drain, 0, unroll=2)

    @pl.when(pl.program_id(0) == 0)
    def _():
        for_pad_rows(lambda cp: cp.wait())


def _dispatch(dest, h2p, n_slots, pad_from, pad_len):
    n, w = h2p.shape
    steps = n // DISPATCH_TILE
    dest3 = dest.reshape(steps, 1, DISPATCH_TILE * TOP_K)
    grid_spec = pltpu.PrefetchScalarGridSpec(
        num_scalar_prefetch=2,
        grid=(steps,),
        in_specs=[
            pl.BlockSpec((1, 1, DISPATCH_TILE * TOP_K), lambda i, pf, pn: (i, 0, 0), memory_space=pltpu.SMEM),
            pl.BlockSpec((DISPATCH_TILE, w), lambda i, pf, pn: (i, 0)),
        ],
        out_specs=pl.BlockSpec(memory_space=pl.ANY),
        scratch_shapes=[pltpu.VMEM((8, w), U32), pltpu.SemaphoreType.DMA, pltpu.SemaphoreType.DMA],
    )
    return pl.pallas_call(
        functools.partial(_dispatch_kernel, n_experts=pad_from.shape[0]),
        out_shape=jax.ShapeDtypeStruct((n_slots, w), U32),
        grid_spec=grid_spec,
        compiler_params=_cparams(("arbitrary",)),
        name="moe_dispatch",
    )(pad_from, pad_len, dest3, h2p)


def _expert_kernel(be_ref, nused_ref, xs_ref, w1_ref, w3_ref, w2_ref, ys_ref, w1b, w3b, w2b):
    i = pl.program_id(0)
    used = i < nused_ref[0]

    @pl.when(used)
    def _():
        changed = (i == 0) | (be_ref[i] != be_ref[jnp.maximum(i - 1, 0)])

        @pl.when(changed)
        def _():
            w1b[...] = w1_ref[0, 0].astype(BF16)
            w3b[...] = w3_ref[0, 0].astype(BF16)
            w2b[...] = w2_ref[0, 0].astype(BF16)

        lo, hi = _unpack_bf16_pair(xs_ref[...])
        x = jnp.concatenate([lo.astype(BF16), hi.astype(BF16)], axis=1)
        h1 = jnp.dot(x, w1b[...], preferred_element_type=F32)
        h3 = jnp.dot(x, w3b[...], preferred_element_type=F32)
        hid = (_silu(h1) * h3).astype(BF16)
        out = jnp.dot(hid, w2b[...], preferred_element_type=F32)
        half = out.shape[1] // 2
        ys_ref[...] = _pack_bf16_pair(out[:, 0:half], out[:, half:])

    @pl.when(jnp.logical_not(used))
    def _():
        ys_ref[...] = jnp.zeros_like(ys_ref)


def _experts(xs, block_expert, n_used, w1, w3, w2, layer):
    n_slots, w = xs.shape
    n_blocks = n_slots // ROW_BLOCK
    _, _, d, hid = w1.shape
    grid_spec = pltpu.PrefetchScalarGridSpec(
        num_scalar_prefetch=2,
        grid=(n_blocks,),
        in_specs=[
            pl.BlockSpec((ROW_BLOCK, w), lambda i, be, nu: (jnp.minimum(i, nu[0] - 1), 0)),
            pl.BlockSpec((1, 1, d, hid), lambda i, be, nu: (layer, be[i], 0, 0)),
            pl.BlockSpec((1, 1, d, hid), lambda i, be, nu: (layer, be[i], 0, 0)),
            pl.BlockSpec((1, 1, hid, d), lambda i, be, nu: (layer, be[i], 0, 0)),
        ],
        out_specs=pl.BlockSpec((ROW_BLOCK, w), lambda i, be, nu: (i, 0)),
        scratch_shapes=[pltpu.VMEM((d, hid), BF16), pltpu.VMEM((d, hid), BF16), pltpu.VMEM((hid, d), BF16)],
    )
    return pl.pallas_call(
        _expert_kernel,
        out_shape=jax.ShapeDtypeStruct((n_slots, w), U32),
        grid_spec=grid_spec,
        compiler_params=_cparams(("arbitrary",)),
        name="moe_experts",
    )(block_expert, n_used, xs, w1, w3, w2)


def _combine_kernel(dest_ref, dnext_ref, wts_ref, ys_ref, sh_ref, x1_ref, mod_ref, lnw_ref, lnb_ref, o_ref,
                    buf, sems, *, d, alpha, n_steps):
    g = pl.program_id(0) * pl.num_programs(1) + pl.program_id(1)
    cur = g % 2
    nxt = 1 - cur

    def issue_tile(tbl_ref, slot):
        def body(t, c):
            for kk in range(TOP_K):
                pltpu.make_async_copy(ys_ref.at[pl.ds(tbl_ref[0, 0, t * TOP_K + kk], 1)],
                                      buf.at[slot, kk, pl.ds(t, 1)], sems.at[slot]).start(priority=kk % 2)
            return c
        lax.fori_loop(0, COMBINE_TILE, body, 0, unroll=2)

    def drain(slot):
        def body(t, c):
            for _ in range(TOP_K):
                pltpu.make_async_copy(ys_ref.at[pl.ds(0, 1)], buf.at[slot, 0, pl.ds(0, 1)], sems.at[slot]).wait()
            return c
        lax.fori_loop(0, COMBINE_TILE, body, 0, unroll=2)

    @pl.when(g == 0)
    def _():
        issue_tile(dest_ref, 0)

    issue_tile(dnext_ref, nxt)
    drain(cur)

    hw = d // 2
    sh = sh_ref[0].astype(F32)
    acc_lo = sh[:, 0:hw]
    acc_hi = sh[:, hw:d]
    wts = wts_ref[0]
    for kk in range(TOP_K):
        lo, hi = _unpack_bf16_pair(buf[cur, kk])
        wk = wts[:, kk:kk + 1]
        acc_lo = acc_lo + wk * lo
        acc_hi = acc_hi + wk * hi
    ff = jnp.concatenate([acc_lo, acc_hi], axis=1)
    g_f = mod_ref[0][:, 5 * d:6 * d]
    o_ref[0] = _normalize(alpha * x1_ref[0] + g_f * ff) * lnw_ref[...] + lnb_ref[...]

    @pl.when(g == n_steps - 1)
    def _():
        drain(nxt)


def _combine(dest, wts, ys, shared, x1, mod3, layer, n_batch, nct_tiles, lnw, lnb, alpha, latent_only):
    b, t, d = x1.shape
    tm = COMBINE_TILE
    tpb = t // tm
    n_steps = b * tpb
    if latent_only:
        out_rows = t - nct_tiles * tm
        out_spec = pl.BlockSpec((1, tm, d), lambda bi, ti: (bi, jnp.maximum(ti - nct_tiles, 0), 0))
    else:
        out_rows = t
        out_spec = pl.BlockSpec((1, tm, d), lambda bi, ti: (bi, ti, 0))
    dest3 = dest.reshape(n_steps, 1, tm * TOP_K)
    tok = lambda w: pl.BlockSpec((1, tm, w), lambda bi, ti: (bi, ti, 0))
    full = lambda a: pl.BlockSpec(a.shape, lambda bi, ti: tuple(0 for _ in a.shape))
    table = lambda shift: pl.BlockSpec(
        (1, 1, tm * TOP_K), lambda bi, ti: (jnp.minimum(bi * tpb + ti + shift, n_steps - 1), 0, 0),
        memory_space=pltpu.SMEM)
    return pl.pallas_call(
        functools.partial(_combine_kernel, d=d, alpha=alpha, n_steps=n_steps),
        out_shape=jax.ShapeDtypeStruct((b, out_rows, d), F32),
        grid=(b, tpb),
        in_specs=[
            table(0), table(1),
            tok(TOP_K),
            pl.BlockSpec(memory_space=pl.ANY),
            tok(d), tok(d),
            pl.BlockSpec((1, 1, N_MOD * d), lambda bi, ti: (layer * 8 + jnp.where(ti < nct_tiles, n_batch, bi), 0, 0)),
            full(lnw), full(lnb),
        ],
        out_specs=out_spec,
        scratch_shapes=[pltpu.VMEM((2, TOP_K, tm, d // 2), U32), pltpu.SemaphoreType.DMA((2,))],
        compiler_params=_cparams(("arbitrary", "arbitrary")),
        name="moe_combine",
    )(dest3, dest3, wts, ys, shared, x1, mod3, lnw, lnb)


def _rope_tables(n_ctx, n_lat):
    pos = jnp.arange(n_lat, dtype=I32)
    rowp = (pos // GRID_W).astype(F32)
    colp = (pos % GRID_W).astype(F32)
    n_freq = MLA_ROPE // 4
    inv = ROPE_BASE ** (-jnp.arange(n_freq, dtype=F32) / n_freq)
    ang = jnp.concatenate([rowp[:, None] * inv, colp[:, None] * inv], axis=-1)
    ang = jnp.concatenate([jnp.zeros((n_ctx, MLA_ROPE // 2), F32), ang], axis=0)
    t = n_ctx + n_lat
    ones = jnp.ones((t, MLA_NOPE), F32)
    zeros_tail = jnp.zeros((t, LANE - MLA_NOPE - MLA_ROPE), F32)
    cos_t = jnp.concatenate([ones, jnp.cos(ang), jnp.cos(ang), zeros_tail], axis=1)
    sin_t = jnp.concatenate([jnp.zeros((t, MLA_NOPE), F32), jnp.sin(ang), jnp.sin(ang), zeros_tail], axis=1)
    return cos_t, sin_t


def _rope_swap(w):
    half = MLA_ROPE // 2
    return jnp.concatenate([-w[..., half:], w[..., :half]], axis=-1)


def _prep_layer_weights(w_in_l, w_uq_l, w_ukv_l, w_out_l, hw, q_lora, kv_lora, n_heads):
    d = w_in_l.shape[0]
    base = 5 * hw + q_lora + kv_lora
    kpe_w = w_in_l[:, base:base + MLA_ROPE]
    z_nope = jnp.zeros((d, MLA_NOPE), F32)
    z_tail = jnp.zeros((d, LANE - MLA_NOPE - MLA_ROPE), F32)
    w_ext = jnp.concatenate([w_in_l[:, :base], z_nope, kpe_w, z_tail, z_nope, _rope_swap(kpe_w), z_tail],
                            axis=1).astype(BF16)
    wq = w_uq_l.reshape(q_lora, n_heads, MLA_NOPE + MLA_ROPE)
    zq = jnp.zeros((q_lora, n_heads, LANE - MLA_NOPE - MLA_ROPE), F32)
    wqa = jnp.concatenate([wq, zq], axis=-1).reshape(q_lora, n_heads * LANE).astype(BF16)
    wqb = jnp.concatenate([jnp.zeros((q_lora, n_heads, MLA_NOPE), F32), _rope_swap(wq[..., MLA_NOPE:]), zq],
                          axis=-1).reshape(q_lora, n_heads * LANE).astype(BF16)
    wkv = w_ukv_l.reshape(kv_lora, n_heads, MLA_NOPE + MLA_V)
    wk = jnp.concatenate([wkv[..., :MLA_NOPE], jnp.zeros((kv_lora, n_heads, LANE - MLA_NOPE), F32)],
                         axis=-1).reshape(kv_lora, n_heads * LANE).astype(BF16)
    wv = jnp.concatenate([wkv[..., MLA_NOPE:], jnp.zeros((kv_lora, n_heads, LANE - MLA_V), F32)],
                         axis=-1).reshape(kv_lora, n_heads * LANE).astype(BF16)
    wohg = w_out_l[:hw].astype(BF16)
    wom = w_out_l[hw:].reshape(n_heads, MLA_V, d)
    womla = jnp.concatenate([wom, jnp.zeros((n_heads, LANE - MLA_V, d), F32)], axis=1)
    womla = womla.reshape(n_heads * LANE, d).astype(BF16)
    return w_ext, wqa, wqb, wk, wv, wohg, womla


def kernel(x, c, ctx, c_ctx, w_mod, b_mod, w_in, hg_lb, hg_norm_w, q_norm_w, w_uq, kv_norm_w, w_ukv, w_out,
           ln1_w, ln1_b, router_w, router_bias, moe_w1, moe_w3, moe_w2, shared_w1, shared_w3, shared_w2,
           ln2_w, ln2_b):
    bsz, n_lat, d = x.shape
    n_ctx = ctx.shape[1]
    depth = w_mod.shape[0]
    t = n_ctx + n_lat
    hw = d // 2
    q_lora = w_uq.shape[1]
    kv_lora = w_ukv.shape[1]
    n_heads = (d - hw) // MLA_V
    n_experts = router_w.shape[2]
    alpha = float((2 * depth) ** 0.25)
    assert n_ctx % TOK_TILE == 0 and n_lat % ATT_TK == 0 and bsz < 8
    assert n_ctx % ATT_TQ == 0 and (bsz * t) % DISPATCH_TILE == 0 and t % COMBINE_TILE == 0

    lb_all = jnp.cumsum(jax.nn.softmax(hg_lb.astype(F32), axis=0), axis=0)
    lb_all = lb_all - lb_all[:1]

    c_rows = jnp.concatenate([c, c_ctx[None, :], jnp.zeros((8 - bsz - 1, d), F32)], axis=0)
    mod = _modulation(c_rows, w_mod, b_mod)
    mod3 = mod.reshape(depth * 8, 1, N_MOD * d)

    cos_t, sin_t = _rope_tables(n_ctx, n_lat)
    xc = jnp.concatenate([ctx, x], axis=1)
    nct = n_ctx // TOK_TILE

    n_tok = bsz * t
    n_assign = n_tok * TOP_K
    n_blocks = -(-(n_assign + n_experts * (ROW_BLOCK - 1)) // ROW_BLOCK)
    n_slots = n_blocks * ROW_BLOCK

    for l in range(depth):
        w_ext, wqa, wqb, wk, wv, wohg, womla = _prep_layer_weights(
            w_in[l], w_uq[l], w_ukv[l], w_out[l], hw, q_lora, kv_lora, n_heads)
        (lf_fw, lf_bw, k_fw, k_bw, q_hg, v_hg, sg, q_rot, q_unrot, kk, vv) = _inproj(
            xc, mod3, l, bsz, nct, w_ext, lb_all[l][None, :], q_norm_w[l][None, :], kv_norm_w[l][None, :],
            wqa, wqb, wk, wv, cos_t, sin_t, n_heads)
        o_fw = _gla(lf_fw, k_fw, q_hg, v_hg, nct, reverse=False)
        o_hg = _gla(lf_bw, k_bw, q_hg, v_hg, nct, reverse=True, extra=(o_fw, sg, hg_norm_w[l][None, :]))
        o_mla = _attention(q_unrot, q_rot, kk, vv, n_ctx, n_heads)
        x1, h2p, shared, idx, wts, pos, counts = _postmix(
            xc, o_hg, o_mla, mod3, l, bsz, nct, wohg, womla, ln1_w[l][None, :], ln1_b[l][None, :],
            router_w[l], router_bias[l][None, :], shared_w1[l].astype(BF16), shared_w3[l].astype(BF16),
            shared_w2[l].astype(BF16), alpha)

        counts = counts[:, 0]
        idx = jnp.swapaxes(idx, 1, 2)
        pos = jnp.swapaxes(pos, 1, 2)
        wts = jnp.swapaxes(wts, 1, 2)
        padded = (counts + ROW_BLOCK - 1) // ROW_BLOCK * ROW_BLOCK
        pad_end = jnp.cumsum(padded)
        pad_start = pad_end - padded
        dest = (pad_start[idx.reshape(n_tok, TOP_K)] + pos.reshape(n_tok, TOP_K)).astype(I32)
        block_first_row = jnp.arange(n_blocks, dtype=I32) * ROW_BLOCK
        block_expert = jnp.minimum(
            jnp.sum((pad_end[None, :] <= block_first_row[:, None]).astype(I32), axis=1), n_experts - 1)
        n_used = (pad_end[-1:] // ROW_BLOCK).astype(I32)

        xs = _dispatch(dest, h2p.reshape(n_tok, hw), n_slots, (pad_start + counts).astype(I32),
                       (padded - counts).astype(I32))
        ys = _experts(xs, block_expert, n_used, moe_w1, moe_w3, moe_w2, l)
        xc = _combine(dest, wts, ys, shared, x1, mod3, l, bsz, n_ctx // COMBINE_TILE,
                      ln2_w[l][None, :], ln2_b[l][None, :], alpha, latent_only=(l == depth - 1))
    return xc
```

```python
import functools

import jax
import jax.numpy as jnp
from jax import lax
from jax.experimental import pallas as pl
from jax.experimental.pallas import tpu as pltpu

F32 = jnp.float32
BF16 = jnp.bfloat16
U32 = jnp.uint32
I32 = jnp.int32
HIGHEST = lax.Precision.HIGHEST

HG_HEAD_DIM = 128
MLA_V = 64
MLA_NOPE = 64
MLA_ROPE = 32
GRID_W = 64
ROPE_BASE = 10000.0
TOP_K = 8
ROUTED_SCALE = 2.5
N_MOD = 6
LN_EPS = 1e-6
RMS_EPS = 1e-6

LANE = 128
TOK_TILE = 256
GLA_CHUNK = 128
GLA_SUB = 32
GLA_SAFE_EXPONENT = 80.0
ATT_TQ = 256
ATT_TK = 1024
ATT_HEADS = 2
LOG2_E = 1.4426950408889634
ROW_BLOCK = 512
DISPATCH_TILE = 256
COMBINE_TILE = 256
VMEM_LIMIT = 56 * 1024 * 1024
MASKED = -1e30


def _cparams(sem):
    return pltpu.CompilerParams(dimension_semantics=sem, vmem_limit_bytes=VMEM_LIMIT)


def _sigmoid(z):
    return 1.0 / (1.0 + jnp.exp(-z))


def _silu(z):
    return z * _sigmoid(z)


def _normalize(x):
    mu = jnp.mean(x, axis=-1, keepdims=True)
    xc = x - mu
    var = jnp.mean(xc * xc, axis=-1, keepdims=True)
    return xc * lax.rsqrt(var + LN_EPS)


def _pack_bf16_pair(lo, hi):
    lo_u = lax.bitcast_convert_type(lo.astype(BF16).astype(F32), U32) >> 16
    hi_u = lax.bitcast_convert_type(hi.astype(BF16).astype(F32), U32) & jnp.uint32(0xFFFF0000)
    return hi_u | lo_u


def _unpack_bf16_pair(u):
    lo = lax.bitcast_convert_type(u << 16, F32)
    hi = lax.bitcast_convert_type(u & jnp.uint32(0xFFFF0000), F32)
    return lo, hi


def _mod_kernel(c_ref, w_ref, b_ref, o_ref):
    c = c_ref[...]
    o_ref[0] = jnp.dot(_silu(c), w_ref[0], precision=HIGHEST, preferred_element_type=F32) + b_ref[0]


def _modulation(c_rows, w_mod, b_mod):
    depth, d, n = w_mod.shape
    tn = 1536
    return pl.pallas_call(
        _mod_kernel,
        out_shape=jax.ShapeDtypeStruct((depth, 8, n), F32),
        grid=(depth, n // tn),
        in_specs=[
            pl.BlockSpec((8, d), lambda l, j: (0, 0)),
            pl.BlockSpec((1, d, tn), lambda l, j: (l, 0, j)),
            pl.BlockSpec((1, 1, tn), lambda l, j: (l, 0, j)),
        ],
        out_specs=pl.BlockSpec((1, 8, tn), lambda l, j: (l, 0, j)),
        compiler_params=_cparams(("arbitrary", "arbitrary")),
        name="modulation",
    )(c_rows, w_mod, b_mod.reshape(depth, 1, n))


def _inproj_kernel(x_ref, mod_ref, w_ref, lb_ref, qnw_ref, kvnw_ref, wqa_ref, wqb_ref, wk_ref, wv_ref,
                   cos_ref, sin_ref,
                   lff_ref, lfb_ref, kf_ref, kb_ref, q_ref, v_ref, sg_ref, qr_ref, qu_ref, kk_ref, vv_ref,
                   *, d, hw, q_lora, kv_lora, n_heads):
    x = x_ref[0]
    m = mod_ref[0]
    shift = m[:, 0:d]
    scale = m[:, d:2 * d]
    h = (_normalize(x) * (1.0 + scale) + shift).astype(BF16)
    proj = jnp.dot(h, w_ref[...], preferred_element_type=F32)

    lb = lb_ref[...]

    def forget(z, lbd):
        f = lbd + (1.0 - lbd) * _sigmoid(z)
        return jnp.log(f), 1.0 - f

    lf, kd = forget(proj[:, 0:hw], lb[:, 0:hw])
    lff_ref[0] = lf
    kf_ref[0] = kd.astype(BF16)
    lf, kd = forget(proj[:, hw:2 * hw], lb[:, hw:2 * hw])
    lfb_ref[0] = lf
    kb_ref[0] = kd.astype(BF16)
    v_ref[0] = proj[:, 2 * hw:3 * hw].astype(BF16)
    q_ref[0] = (_silu(proj[:, 3 * hw:4 * hw]) * (HG_HEAD_DIM ** -0.5)).astype(BF16)
    sg_ref[0] = _silu(proj[:, 4 * hw:5 * hw]).astype(BF16)

    o = 5 * hw
    cq = proj[:, o:o + q_lora]
    ckv = proj[:, o + q_lora:o + q_lora + kv_lora]
    o2 = o + q_lora + kv_lora
    kpe_a = proj[:, o2:o2 + LANE]
    kpe_b = proj[:, o2 + LANE:o2 + 2 * LANE]

    cqn = (cq * lax.rsqrt(jnp.mean(cq * cq, axis=-1, keepdims=True) + RMS_EPS) * qnw_ref[...]).astype(BF16)
    ckvn = (ckv * lax.rsqrt(jnp.mean(ckv * ckv, axis=-1, keepdims=True) + RMS_EPS) * kvnw_ref[...]).astype(BF16)

    cos = cos_ref[...]
    sin = sin_ref[...]
    cos_h = jnp.concatenate([cos] * n_heads, axis=1)
    sin_h = jnp.concatenate([sin] * n_heads, axis=1)
    att_scale = (MLA_NOPE + MLA_ROPE) ** -0.5 * LOG2_E
    qa = jnp.dot(cqn, wqa_ref[...], preferred_element_type=F32)
    qb = jnp.dot(cqn, wqb_ref[...], preferred_element_type=F32)
    qu_ref[0] = (qa * att_scale).astype(BF16)
    qr_ref[0] = ((qa * cos_h + qb * sin_h) * att_scale).astype(BF16)

    kr = kpe_a * cos + kpe_b * sin
    kk = jnp.dot(ckvn, wk_ref[...], preferred_element_type=F32) + jnp.concatenate([kr] * n_heads, axis=1)
    kk_ref[0] = kk.astype(BF16)
    vv = jnp.dot(ckvn, wv_ref[...], preferred_element_type=F32)
    lane = lax.broadcasted_iota(I32, vv.shape, 1)
    vv = jnp.where((lane % LANE) == MLA_V, 1.0, vv)
    vv_ref[0] = vv.astype(BF16)


def _inproj(xc, mod3, layer, n_batch, nct, w_ext, lb, qnw, kvnw, wqa, wqb, wk, wv, cos_t, sin_t, n_heads):
    b, t, d = xc.shape
    hw = d // 2
    q_lora = wqa.shape[0]
    kv_lora = wk.shape[0]
    hp = n_heads * LANE
    tm = TOK_TILE
    full = lambda shape: pl.BlockSpec(shape, lambda bi, ti: tuple(0 for _ in shape))
    tok = lambda w: pl.BlockSpec((1, tm, w), lambda bi, ti: (bi, ti, 0))
    out_shape = (
        jax.ShapeDtypeStruct((b, t, hw), F32), jax.ShapeDtypeStruct((b, t, hw), F32),
        jax.ShapeDtypeStruct((b, t, hw), BF16), jax.ShapeDtypeStruct((b, t, hw), BF16),
        jax.ShapeDtypeStruct((b, t, hw), BF16), jax.ShapeDtypeStruct((b, t, hw), BF16),
        jax.ShapeDtypeStruct((b, t, hw), BF16),
        jax.ShapeDtypeStruct((b, t, hp), BF16), jax.ShapeDtypeStruct((b, t, hp), BF16),
        jax.ShapeDtypeStruct((b, t, hp), BF16), jax.ShapeDtypeStruct((b, t, hp), BF16),
    )
    return pl.pallas_call(
        functools.partial(_inproj_kernel, d=d, hw=hw, q_lora=q_lora, kv_lora=kv_lora, n_heads=n_heads),
        out_shape=out_shape,
        grid=(b, t // tm),
        in_specs=[
            tok(d),
            pl.BlockSpec((1, 1, N_MOD * d), lambda bi, ti: (layer * 8 + jnp.where(ti < nct, n_batch, bi), 0, 0)),
            full(w_ext.shape), full(lb.shape), full(qnw.shape), full(kvnw.shape),
            full(wqa.shape), full(wqb.shape), full(wk.shape), full(wv.shape),
            pl.BlockSpec((tm, LANE), lambda bi, ti: (ti, 0)),
            pl.BlockSpec((tm, LANE), lambda bi, ti: (ti, 0)),
        ],
        out_specs=tuple([tok(hw)] * 7 + [tok(hp)] * 4),
        compiler_params=_cparams(("arbitrary", "arbitrary")),
        name="inproj",
    )(xc, mod3, w_ext, lb, qnw, kvnw, wqa, wqb, wk, wv, cos_t, sin_t)


def _gla_kernel(*refs, reverse, n_chunks, n_heads, final):
    if final:
        lf_ref, k_ref, q_ref, v_ref, ofw_ref, sg_ref, nw_ref, o_ref, st_ref = refs
    else:
        lf_ref, k_ref, q_ref, v_ref, o_ref, st_ref = refs
    c, s = GLA_CHUNK, GLA_SUB
    n_sub = c // s

    @pl.when(pl.program_id(1) == 0)
    def _():
        st_ref[...] = jnp.zeros_like(st_ref)

    row = lax.broadcasted_iota(I32, (c, c), 0)
    col = lax.broadcasted_iota(I32, (c, c), 1)
    tri = jnp.where((row <= col) if reverse else (row >= col), 1.0, 0.0).astype(F32)
    sub_row = lax.broadcasted_iota(I32, (s, HG_HEAD_DIM), 0)
    ones = jnp.ones((HG_HEAD_DIM, HG_HEAD_DIM), BF16)

    nt = (((1,), (1,)), ((), ()))

    def sub_geometry(si):
        rs = slice(si * s, (si + 1) * s)
        if reverse:
            return rs, (slice((si + 1) * s, c), (si + 1) * s) if si < n_sub - 1 else (None, None)
        return rs, (slice(0, si * s), si * s - 1) if si > 0 else (None, None)

    def chunk_body(i, carry):
        ci = (n_chunks - 1 - i) if reverse else i
        r0 = pl.multiple_of(ci * c, c)
        rows = pl.ds(r0, c)
        bc_all = jnp.dot(tri, lf_ref[0, rows, :], precision=HIGHEST, preferred_element_type=F32)
        worst = None
        for si in range(n_sub):
            rs, (_, brow) = sub_geometry(si)
            far = bc_all[rs.start:rs.start + 1] if reverse else bc_all[rs.stop - 1:rs.stop]
            d = far if brow is None else far - bc_all[brow:brow + 1]
            worst = d if worst is None else jnp.minimum(worst, d)

        def load(h):
            sl = slice(h * HG_HEAD_DIM, (h + 1) * HG_HEAD_DIM)
            return (sl, bc_all[:, sl], k_ref[0, rows, sl].astype(F32), q_ref[0, rows, sl].astype(F32),
                    v_ref[0, rows, sl].astype(F32))

        def store(sl, o):
            if final:
                o = o + ofw_ref[0, rows, sl]
                o = o * lax.rsqrt(jnp.mean(o * o, axis=-1, keepdims=True) + RMS_EPS) * nw_ref[...]
                o_ref[0, rows, sl] = (o * sg_ref[0, rows, sl].astype(F32)).astype(o_ref.dtype)
            else:
                o_ref[0, rows, sl] = o

        def intra(h, fast):
            sl, bc, k, q, v = load(h)
            qhat = (q * jnp.exp(bc)).astype(BF16)
            o_state = lax.dot_general(qhat, st_ref[h].astype(BF16), nt, preferred_element_type=F32)
            pieces = []
            for si in range(n_sub):
                rs, (src, brow) = sub_geometry(si)
                bs, qs, ks, vs = bc[rs], q[rs], k[rs], v[rs]
                acc = o_state[rs]
                beta = jnp.zeros((1, HG_HEAD_DIM), F32) if brow is None else bc[brow:brow + 1]
                if fast:
                    src = slice(rs.start, c) if reverse else slice(0, rs.stop)
                if src is not None:
                    qi = (qs * jnp.exp(bs - beta)).astype(BF16)
                    ksrc = (k[src] * jnp.exp(beta - bc[src])).astype(BF16)
                    a = lax.dot_general(qi, ksrc, nt, preferred_element_type=F32)
                    if fast:
                        n_src = src.stop - src.start
                        r_i = lax.broadcasted_iota(I32, (s, n_src), 0)
                        c_i = lax.broadcasted_iota(I32, (s, n_src), 1)
                        keep = (c_i >= r_i) if reverse else (c_i <= r_i + rs.start)
                        a = jnp.where(keep, a, 0.0)
                    acc = acc + jnp.dot(a.astype(BF16), v[src].astype(BF16), preferred_element_type=F32)
                if not fast:
                    ws = []
                    for j in range(s):
                        mask = (sub_row <= j) if reverse else (sub_row >= j)
                        dlt = jnp.where(mask, bs - bs[j:j + 1], MASKED)
                        ws.append((jnp.exp(dlt) * qs * ks[j:j + 1]).astype(BF16))
                    sums = jnp.dot(jnp.concatenate(ws, axis=0), ones, preferred_element_type=F32)
                    for j in range(s):
                        acc = acc + sums[j * s:(j + 1) * s] * vs[j:j + 1]
                pieces.append(acc)
            store(sl, jnp.concatenate(pieces, axis=0))

        bounded = jnp.min(worst) > -GLA_SAFE_EXPONENT

        @pl.when(bounded)
        def _():
            for h in range(n_heads):
                intra(h, True)

        @pl.when(jnp.logical_not(bounded))
        def _():
            for h in range(n_heads):
                intra(h, False)

        for h in range(n_heads):
            sl, bc, k, _, v = load(h)
            tot = bc[0:1] if reverse else bc[c - 1:c]
            khat = (k * jnp.exp(tot - bc)).astype(BF16)
            st_ref[h] = st_ref[h] * jnp.exp(tot) + jnp.dot(v.T.astype(BF16), khat, preferred_element_type=F32)
        return carry

    lax.fori_loop(0, n_chunks, chunk_body, 0, unroll=True)


def _gla(lf, k, q, v, nct_blocks, reverse, extra=None):
    b, t, hw = lf.shape
    n_heads = hw // HG_HEAD_DIM
    blk = TOK_TILE
    nb = t // blk

    def blk_index(g):
        if not reverse:
            return g
        return jnp.where(g < nct_blocks, nct_blocks - 1 - g, nb - 1 - (g - nct_blocks))

    spec = pl.BlockSpec((1, blk, hw), lambda bi, g: (bi, blk_index(g), 0))
    in_specs = [spec, spec, spec, spec]
    args = [lf, k, q, v]
    final = extra is not None
    if final:
        ofw, sg, nw = extra
        in_specs += [spec, spec, pl.BlockSpec(nw.shape, lambda bi, g: (0, 0))]
        args += [ofw, sg, nw]
    return pl.pallas_call(
        functools.partial(_gla_kernel, reverse=reverse, n_chunks=blk // GLA_CHUNK, n_heads=n_heads, final=final),
        out_shape=jax.ShapeDtypeStruct((b, t, hw), BF16 if final else F32),
        grid=(b, nb),
        in_specs=in_specs,
        out_specs=spec,
        scratch_shapes=[pltpu.VMEM((n_heads, HG_HEAD_DIM, HG_HEAD_DIM), F32)],
        compiler_params=_cparams(("arbitrary", "arbitrary")),
        name="gla_bwd" if reverse else "gla_fwd",
    )(*args)


def _attn_kernel(qu_ref, qr_ref, k_ref, v_ref, o_ref, s_sc, *, n_ctx, n_lat):
    is_ctx = pl.program_id(2) < n_ctx // ATT_TQ
    nt = (((1,), (1,)), ((), ()))
    n_chunks = n_lat // ATT_TK
    heads = [slice(h * LANE, (h + 1) * LANE) for h in range(ATT_HEADS)]

    def ctx_scores(hs):
        return lax.dot_general(qu_ref[0, :, hs], k_ref[0, 0:n_ctx, hs], nt, preferred_element_type=F32)

    def finish(hs, acc):
        o_ref[0, :, hs] = (acc * (1.0 / acc[:, MLA_V:MLA_V + 1])).astype(o_ref.dtype)

    def lane_tile_max(s, m):
        for c in range(s.shape[1] // LANE):
            t = s[:, c * LANE:(c + 1) * LANE]
            m = t if m is None else jnp.maximum(m, t)
        return m

    @pl.when(is_ctx)
    def _():
        for hs in heads:
            s = ctx_scores(hs)
            p = jnp.exp2(s - jnp.max(s, axis=-1, keepdims=True))
            finish(hs, jnp.dot(p.astype(BF16), v_ref[0, 0:n_ctx, hs], preferred_element_type=F32))

    @pl.when(jnp.logical_not(is_ctx))
    def _():
        ms, accs = [], []
        for hi, hs in enumerate(heads):
            qr = qr_ref[0, :, hs]
            s_c = ctx_scores(hs)
            m_t = lane_tile_max(s_c, None)
            for j in range(n_chunks):
                rows = slice(n_ctx + j * ATT_TK, n_ctx + (j + 1) * ATT_TK)
                s = lax.dot_general(qr, k_ref[0, rows, hs], nt, preferred_element_type=F32)
                s_sc[hi, j] = s
                m_t = lane_tile_max(s, m_t)
            m = jnp.max(m_t, axis=-1, keepdims=True)
            ms.append(m)
            accs.append(jnp.dot(jnp.exp2(s_c - m).astype(BF16), v_ref[0, 0:n_ctx, hs],
                                preferred_element_type=F32))

        for j in range(n_chunks):
            rows = slice(n_ctx + j * ATT_TK, n_ctx + (j + 1) * ATT_TK)
            for hi, hs in enumerate(heads):
                p = jnp.exp2(s_sc[hi, j] - ms[hi]).astype(BF16)
                accs[hi] = accs[hi] + jnp.dot(p, v_ref[0, rows, hs], preferred_element_type=F32)
        for hi, hs in enumerate(heads):
            finish(hs, accs[hi])


def _attention(qu, qr, kk, vv, n_ctx, n_heads):
    b, t, hp = qu.shape
    n_lat = t - n_ctx
    w = ATT_HEADS * LANE
    qspec = pl.BlockSpec((1, ATT_TQ, w), lambda bi, h, qi: (bi, qi, h))
    kspec = pl.BlockSpec((1, t, w), lambda bi, h, qi: (bi, 0, h))
    return pl.pallas_call(
        functools.partial(_attn_kernel, n_ctx=n_ctx, n_lat=n_lat),
        out_shape=jax.ShapeDtypeStruct((b, t, hp), BF16),
        grid=(b, n_heads // ATT_HEADS, t // ATT_TQ),
        in_specs=[qspec, qspec, kspec, kspec],
        out_specs=qspec,
        scratch_shapes=[pltpu.VMEM((ATT_HEADS, n_lat // ATT_TK, ATT_TQ, ATT_TK), F32)],
        compiler_params=_cparams(("arbitrary", "arbitrary", "arbitrary")),
        name="mla_attention",
    )(qu, qr, kk, vv)


def _postmix_kernel(x_ref, ohg_ref, omla_ref, mod_ref, wohg_ref, womla_ref, ln1w_ref, ln1b_ref,
                    rwh_ref, rwl_ref, rb_ref, sw1_ref, sw3_ref, sw2_ref,
                    x1_ref, h2p_ref, sh_ref, idx_ref, wts_ref, pos_ref, cnt_ref, cnt_sc,
                    *, d, alpha, n_experts):
    @pl.when((pl.program_id(0) == 0) & (pl.program_id(1) == 0))
    def _():
        cnt_sc[...] = jnp.zeros_like(cnt_sc)

    m = mod_ref[0]
    g_a = m[:, 2 * d:3 * d]
    sh_f = m[:, 3 * d:4 * d]
    sc_f = m[:, 4 * d:5 * d]
    mix = (jnp.dot(ohg_ref[0], wohg_ref[...], preferred_element_type=F32)
           + jnp.dot(omla_ref[0], womla_ref[...], preferred_element_type=F32))
    x1 = _normalize(alpha * x_ref[0] + g_a * mix) * ln1w_ref[...] + ln1b_ref[...]
    x1_ref[0] = x1
    h2 = _normalize(x1) * (1.0 + sc_f) + sh_f
    hw = d // 2
    h2p_ref[0] = _pack_bf16_pair(h2[:, 0:hw], h2[:, hw:d])
    h2b = h2.astype(BF16)

    a1 = jnp.dot(h2b, sw1_ref[...], preferred_element_type=F32)
    a3 = jnp.dot(h2b, sw3_ref[...], preferred_element_type=F32)
    hid = (_silu(a1) * a3).astype(BF16)
    sh_ref[0] = jnp.dot(hid, sw2_ref[...], preferred_element_type=F32).astype(sh_ref.dtype)

    nt = (((1,), (1,)), ((), ()))
    h2lo = (h2 - h2b.astype(F32)).astype(BF16)
    logits = (lax.dot_general(rwh_ref[...], h2b, nt, preferred_element_type=F32)
              + lax.dot_general(rwh_ref[...], h2lo, nt, preferred_element_type=F32)
              + lax.dot_general(rwl_ref[...], h2b, nt, preferred_element_type=F32))
    scores = _sigmoid(logits)
    sel = scores + rb_ref[...][:, 0:1]
    tm = scores.shape[1]
    erow = lax.broadcasted_iota(I32, (n_experts, tm), 0).astype(F32)
    masks, tops, idxs = [], [], []
    for _ in range(TOP_K):
        mx = jnp.max(sel, axis=0, keepdims=True)
        ik = jnp.min(jnp.where(sel == mx, erow, float(n_experts)), axis=0, keepdims=True)
        oh = erow == ik
        masks.append(oh)
        idxs.append(ik)
        tops.append(jnp.sum(jnp.where(oh, scores, 0.0), axis=0, keepdims=True))
        sel = jnp.where(oh, -jnp.inf, sel)
    tsum = tops[0]
    for tk in tops[1:]:
        tsum = tsum + tk
    inv = ROUTED_SCALE / tsum

    ohf = jnp.zeros((n_experts, tm), F32)
    for oh in masks:
        ohf = jnp.where(oh, 1.0, ohf)
    r = lax.broadcasted_iota(I32, (tm, tm), 0)
    cc = lax.broadcasted_iota(I32, (tm, tm), 1)
    earlier = jnp.where(r < cc, 1.0, 0.0).astype(BF16)
    cnt = cnt_sc[...][:, 0:1]
    before = jnp.dot(ohf.astype(BF16), earlier, preferred_element_type=F32) + cnt
    cnt_new = cnt + jnp.sum(ohf, axis=1, keepdims=True)
    cnt_sc[...] = jnp.broadcast_to(cnt_new, cnt_sc.shape)
    cnt_ref[...] = jnp.broadcast_to(cnt_new, cnt_ref.shape).astype(I32)

    row_k = lax.broadcasted_iota(I32, (TOP_K, tm), 0)
    idx_o = jnp.zeros((TOP_K, tm), I32)
    wts_o = jnp.zeros((TOP_K, tm), F32)
    pos_o = jnp.zeros((TOP_K, tm), I32)
    for kk in range(TOP_K):
        pk = jnp.sum(jnp.where(masks[kk], before, 0.0), axis=0, keepdims=True)
        hit = row_k == kk
        idx_o = jnp.where(hit, idxs[kk].astype(I32), idx_o)
        wts_o = jnp.where(hit, tops[kk] * inv, wts_o)
        pos_o = jnp.where(hit, pk.astype(I32), pos_o)
    idx_ref[0] = idx_o
    wts_ref[0] = wts_o
    pos_ref[0] = pos_o


def _postmix(xc, ohg, omla, mod3, layer, n_batch, nct, wohg, womla, ln1w, ln1b, rw, rb, sw1, sw3, sw2, alpha):
    b, t, d = xc.shape
    tm = TOK_TILE
    n_experts = rw.shape[1]
    rwt = rw.T
    rwh = rwt.astype(BF16)
    rwl = (rwt - rwh.astype(F32)).astype(BF16)
    rb = jnp.broadcast_to(rb.reshape(n_experts, 1), (n_experts, LANE))
    full = lambda a: pl.BlockSpec(a.shape, lambda bi, ti: tuple(0 for _ in a.shape))
    tok = lambda w: pl.BlockSpec((1, tm, w), lambda bi, ti: (bi, ti, 0))
    out_shape = (
        jax.ShapeDtypeStruct((b, t, d), F32),
        jax.ShapeDtypeStruct((b, t, d // 2), U32),
        jax.ShapeDtypeStruct((b, t, d), BF16),
        jax.ShapeDtypeStruct((b, TOP_K, t), I32),
        jax.ShapeDtypeStruct((b, TOP_K, t), F32),
        jax.ShapeDtypeStruct((b, TOP_K, t), I32),
        jax.ShapeDtypeStruct((n_experts, LANE), I32),
    )
    kmaj = pl.BlockSpec((1, TOP_K, tm), lambda bi, ti: (bi, 0, ti))
    return pl.pallas_call(
        functools.partial(_postmix_kernel, d=d, alpha=alpha, n_experts=n_experts),
        out_shape=out_shape,
        grid=(b, t // tm),
        in_specs=[
            tok(d), tok(ohg.shape[2]), tok(omla.shape[2]),
            pl.BlockSpec((1, 1, N_MOD * d), lambda bi, ti: (layer * 8 + jnp.where(ti < nct, n_batch, bi), 0, 0)),
            full(wohg), full(womla), full(ln1w), full(ln1b), full(rwh), full(rwl), full(rb),
            full(sw1), full(sw3), full(sw2),
        ],
        out_specs=(tok(d), tok(d // 2), tok(d), kmaj, kmaj, kmaj,
                   pl.BlockSpec((n_experts, LANE), lambda bi, ti: (0, 0))),
        scratch_shapes=[pltpu.VMEM((n_experts, LANE), F32)],
        compiler_params=_cparams(("arbitrary", "arbitrary")),
        name="postmix_router",
    )(xc, ohg, omla, mod3, wohg, womla, ln1w, ln1b, rwh, rwl, rb, sw1, sw3, sw2)


def _row_copy(src_ref, src_row, dst_ref, dst_row, sem):
    return pltpu.make_async_copy(src_ref.at[pl.ds(src_row, 1)], dst_ref.at[pl.ds(dst_row, 1)], sem)


def _dispatch_kernel(pad_from_ref, pad_len_ref, dest_ref, h_ref, xs_ref, zeros, sem, zsem, *, n_experts):
    tile_rows = zeros.shape[0]

    def for_pad_rows(act):
        def per_expert(e, c):
            first, n = pad_from_ref[e], pad_len_ref[e]
            head = jnp.minimum((-first) & (tile_rows - 1), n)

            def per_row(r, c2):
                act(_row_copy(zeros, 0, xs_ref, first + r, zsem))
                return c2

            def per_tile(i, c2):
                start = pl.multiple_of(first + head + i * tile_rows, tile_rows)
                act(pltpu.make_async_copy(zeros, xs_ref.at[pl.ds(start, tile_rows)], zsem))
                return c2

            c = lax.fori_loop(0, head, per_row, c)
            return lax.fori_loop(0, lax.shift_right_logical(n - head, tile_rows.bit_length() - 1), per_tile, c)
        lax.fori_loop(0, n_experts, per_expert, 0)

    @pl.when(pl.program_id(0) == 0)
    def _():
        zeros[...] = jnp.zeros_like(zeros)
        for_pad_rows(lambda cp: cp.start())

    def issue(t, c):
        for kk in range(TOP_K):
            _row_copy(h_ref, t, xs_ref, dest_ref[0, 0, t * TOP_K + kk], sem).start(priority=kk % 2)
        return c

    lax.fori_loop(0, DISPATCH_TILE, issue, 0, unroll=2)

    def drain(t, c):
        for kk in range(TOP_K):
            _row_copy(h_ref, 0, xs_ref, 0, sem).wait()
        return c

    lax.fori_loop(0, DISPATCH_TILE, drain, 0, unroll=2)

    @pl.when(pl.program_id(0) == 0)
    def _():
        for_pad_rows(lambda cp: cp.wait())


def _dispatch(dest, h2p, n_slots, pad_from, pad_len):
    n, w = h2p.shape
    steps = n // DISPATCH_TILE
    dest3 = dest.reshape(steps, 1, DISPATCH_TILE * TOP_K)
    grid_spec = pltpu.PrefetchScalarGridSpec(
        num_scalar_prefetch=2,
        grid=(steps,),
        in_specs=[
            pl.BlockSpec((1, 1, DISPATCH_TILE * TOP_K), lambda i, pf, pn: (i, 0, 0), memory_space=pltpu.SMEM),
            pl.BlockSpec((DISPATCH_TILE, w), lambda i, pf, pn: (i, 0)),
        ],
        out_specs=pl.BlockSpec(memory_space=pl.ANY),
        scratch_shapes=[pltpu.VMEM((8, w), U32), pltpu.SemaphoreType.DMA, pltpu.SemaphoreType.DMA],
    )
    return pl.pallas_call(
        functools.partial(_dispatch_kernel, n_experts=pad_from.shape[0]),
        out_shape=jax.ShapeDtypeStruct((n_slots, w), U32),
        grid_spec=grid_spec,
        compiler_params=_cparams(("arbitrary",)),
        name="moe_dispatch",
    )(pad_from, pad_len, dest3, h2p)


def _expert_kernel(be_ref, nused_ref, xs_ref, w1_ref, w3_ref, w2_ref, ys_ref, w1b, w3b, w2b):
    i = pl.program_id(0)
    used = i < nused_ref[0]

    @pl.when(used)
    def _():
        changed = (i == 0) | (be_ref[i] != be_ref[jnp.maximum(i - 1, 0)])

        @pl.when(changed)
        def _():
            w1b[...] = w1_ref[0, 0].astype(BF16)
            w3b[...] = w3_ref[0, 0].astype(BF16)
            w2b[...] = w2_ref[0, 0].astype(BF16)

        lo, hi = _unpack_bf16_pair(xs_ref[...])
        x = jnp.concatenate([lo.astype(BF16), hi.astype(BF16)], axis=1)
        h1 = jnp.dot(x, w1b[...], preferred_element_type=F32)
        h3 = jnp.dot(x, w3b[...], preferred_element_type=F32)
        hid = (_silu(h1) * h3).astype(BF16)
        out = jnp.dot(hid, w2b[...], preferred_element_type=F32)
        half = out.shape[1] // 2
        ys_ref[...] = _pack_bf16_pair(out[:, 0:half], out[:, half:])

    @pl.when(jnp.logical_not(used))
    def _():
        ys_ref[...] = jnp.zeros_like(ys_ref)


def _experts(xs, block_expert, n_used, w1, w3, w2, layer):
    n_slots, w = xs.shape
    n_blocks = n_slots // ROW_BLOCK
    _, _, d, hid = w1.shape
    grid_spec = pltpu.PrefetchScalarGridSpec(
        num_scalar_prefetch=2,
        grid=(n_blocks,),
        in_specs=[
            pl.BlockSpec((ROW_BLOCK, w), lambda i, be, nu: (jnp.minimum(i, nu[0] - 1), 0)),
            pl.BlockSpec((1, 1, d, hid), lambda i, be, nu: (layer, be[i], 0, 0)),
            pl.BlockSpec((1, 1, d, hid), lambda i, be, nu: (layer, be[i], 0, 0)),
            pl.BlockSpec((1, 1, hid, d), lambda i, be, nu: (layer, be[i], 0, 0)),
        ],
        out_specs=pl.BlockSpec((ROW_BLOCK, w), lambda i, be, nu: (i, 0)),
        scratch_shapes=[pltpu.VMEM((d, hid), BF16), pltpu.VMEM((d, hid), BF16), pltpu.VMEM((hid, d), BF16)],
    )
    return pl.pallas_call(
        _expert_kernel,
        out_shape=jax.ShapeDtypeStruct((n_slots, w), U32),
        grid_spec=grid_spec,
        compiler_params=_cparams(("arbitrary",)),
        name="moe_experts",
    )(block_expert, n_used, xs, w1, w3, w2)


def _combine_kernel(dest_ref, dnext_ref, wts_ref, ys_ref, sh_ref, x1_ref, mod_ref, lnw_ref, lnb_ref, o_ref,
                    buf, sems, *, d, alpha, n_steps):
    g = pl.program_id(0) * pl.num_programs(1) + pl.program_id(1)
    cur = g % 2
    nxt = 1 - cur

    def issue_tile(tbl_ref, slot):
        def body(t, c):
            for kk in range(TOP_K):
                pltpu.make_async_copy(ys_ref.at[pl.ds(tbl_ref[0, 0, t * TOP_K + kk], 1)],
                                      buf.at[slot, kk, pl.ds(t, 1)], sems.at[slot]).start(priority=kk % 2)
            return c
        lax.fori_loop(0, COMBINE_TILE, body, 0, unroll=2)

    def drain(slot):
        def body(t, c):
            for _ in range(TOP_K):
                pltpu.make_async_copy(ys_ref.at[pl.ds(0, 1)], buf.at[slot, 0, pl.ds(0, 1)], sems.at[slot]).wait()
            return c
        lax.fori_loop(0, COMBINE_TILE, body, 0, unroll=2)

    @pl.when(g == 0)
    def _():
        issue_tile(dest_ref, 0)

    issue_tile(dnext_ref, nxt)
    drain(cur)

    hw = d // 2
    sh = sh_ref[0].astype(F32)
    acc_lo = sh[:, 0:hw]
    acc_hi = sh[:, hw:d]
    wts = wts_ref[0]
    for kk in range(TOP_K):
        lo, hi = _unpack_bf16_pair(buf[cur, kk])
        wk = wts[:, kk:kk + 1]
        acc_lo = acc_lo + wk * lo
        acc_hi = acc_hi + wk * hi
    ff = jnp.concatenate([acc_lo, acc_hi], axis=1)
    g_f = mod_ref[0][:, 5 * d:6 * d]
    o_ref[0] = _normalize(alpha * x1_ref[0] + g_f * ff) * lnw_ref[...] + lnb_ref[...]

    @pl.when(g == n_steps - 1)
    def _():
        drain(nxt)


def _combine(dest, wts, ys, shared, x1, mod3, layer, n_batch, nct_tiles, lnw, lnb, alpha, latent_only):
    b, t, d = x1.shape
    tm = COMBINE_TILE
    tpb = t // tm
    n_steps = b * tpb
    if latent_only:
        out_rows = t - nct_tiles * tm
        out_spec = pl.BlockSpec((1, tm, d), lambda bi, ti: (bi, jnp.maximum(ti - nct_tiles, 0), 0))
    else:
        out_rows = t
        out_spec = pl.BlockSpec((1, tm, d), lambda bi, ti: (bi, ti, 0))
    dest3 = dest.reshape(n_steps, 1, tm * TOP_K)
    tok = lambda w: pl.BlockSpec((1, tm, w), lambda bi, ti: (bi, ti, 0))
    full = lambda a: pl.BlockSpec(a.shape, lambda bi, ti: tuple(0 for _ in a.shape))
    table = lambda shift: pl.BlockSpec(
        (1, 1, tm * TOP_K), lambda bi, ti: (jnp.minimum(bi * tpb + ti + shift, n_steps - 1), 0, 0),
        memory_space=pltpu.SMEM)
    return pl.pallas_call(
        functools.partial(_combine_kernel, d=d, alpha=alpha, n_steps=n_steps),
        out_shape=jax.ShapeDtypeStruct((b, out_rows, d), F32),
        grid=(b, tpb),
        in_specs=[
            table(0), table(1),
            tok(TOP_K),
            pl.BlockSpec(memory_space=pl.ANY),
            tok(d), tok(d),
            pl.BlockSpec((1, 1, N_MOD * d), lambda bi, ti: (layer * 8 + jnp.where(ti < nct_tiles, n_batch, bi), 0, 0)),
            full(lnw), full(lnb),
        ],
        out_specs=out_spec,
        scratch_shapes=[pltpu.VMEM((2, TOP_K, tm, d // 2), U32), pltpu.SemaphoreType.DMA((2,))],
        compiler_params=_cparams(("arbitrary", "arbitrary")),
        name="moe_combine",
    )(dest3, dest3, wts, ys, shared, x1, mod3, lnw, lnb)


def _rope_tables(n_ctx, n_lat):
    pos = jnp.arange(n_lat, dtype=I32)
    rowp = (pos // GRID_W).astype(F32)
    colp = (pos % GRID_W).astype(F32)
    n_freq = MLA_ROPE // 4
    inv = ROPE_BASE ** (-jnp.arange(n_freq, dtype=F32) / n_freq)
    ang = jnp.concatenate([rowp[:, None] * inv, colp[:, None] * inv], axis=-1)
    ang = jnp.concatenate([jnp.zeros((n_ctx, MLA_ROPE // 2), F32), ang], axis=0)
    t = n_ctx + n_lat
    ones = jnp.ones((t, MLA_NOPE), F32)
    zeros_tail = jnp.zeros((t, LANE - MLA_NOPE - MLA_ROPE), F32)
    cos_t = jnp.concatenate([ones, jnp.cos(ang), jnp.cos(ang), zeros_tail], axis=1)
    sin_t = jnp.concatenate([jnp.zeros((t, MLA_NOPE), F32), jnp.sin(ang), jnp.sin(ang), zeros_tail], axis=1)
    return cos_t, sin_t


def _rope_swap(w):
    half = MLA_ROPE // 2
    return jnp.concatenate([-w[..., half:], w[..., :half]], axis=-1)


def _prep_layer_weights(w_in_l, w_uq_l, w_ukv_l, w_out_l, hw, q_lora, kv_lora, n_heads):
    d = w_in_l.shape[0]
    base = 5 * hw + q_lora + kv_lora
    kpe_w = w_in_l[:, base:base + MLA_ROPE]
    z_nope = jnp.zeros((d, MLA_NOPE), F32)
    z_tail = jnp.zeros((d, LANE - MLA_NOPE - MLA_ROPE), F32)
    w_ext = jnp.concatenate([w_in_l[:, :base], z_nope, kpe_w, z_tail, z_nope, _rope_swap(kpe_w), z_tail],
                            axis=1).astype(BF16)
    wq = w_uq_l.reshape(q_lora, n_heads, MLA_NOPE + MLA_ROPE)
    zq = jnp.zeros((q_lora, n_heads, LANE - MLA_NOPE - MLA_ROPE), F32)
    wqa = jnp.concatenate([wq, zq], axis=-1).reshape(q_lora, n_heads * LANE).astype(BF16)
    wqb = jnp.concatenate([jnp.zeros((q_lora, n_heads, MLA_NOPE), F32), _rope_swap(wq[..., MLA_NOPE:]), zq],
                          axis=-1).reshape(q_lora, n_heads * LANE).astype(BF16)
    wkv = w_ukv_l.reshape(kv_lora, n_heads, MLA_NOPE + MLA_V)
    wk = jnp.concatenate([wkv[..., :MLA_NOPE], jnp.zeros((kv_lora, n_heads, LANE - MLA_NOPE), F32)],
                         axis=-1).reshape(kv_lora, n_heads * LANE).astype(BF16)
    wv = jnp.concatenate([wkv[..., MLA_NOPE:], jnp.zeros((kv_lora, n_heads, LANE - MLA_V), F32)],
                         axis=-1).reshape(kv_lora, n_heads * LANE).astype(BF16)
    wohg = w_out_l[:hw].astype(BF16)
    wom = w_out_l[hw:].reshape(n_heads, MLA_V, d)
    womla = jnp.concatenate([wom, jnp.zeros((n_heads, LANE - MLA_V, d), F32)], axis=1)
    womla = womla.reshape(n_heads * LANE, d).astype(BF16)
    return w_ext, wqa, wqb, wk, wv, wohg, womla


def kernel(x, c, ctx, c_ctx, w_mod, b_mod, w_in, hg_lb, hg_norm_w, q_norm_w, w_uq, kv_norm_w, w_ukv, w_out,
           ln1_w, ln1_b, router_w, router_bias, moe_w1, moe_w3, moe_w2, shared_w1, shared_w3, shared_w2,
           ln2_w, ln2_b):
    bsz, n_lat, d = x.shape
    n_ctx = ctx.shape[1]
    depth = w_mod.shape[0]
    t = n_ctx + n_lat
    hw = d // 2
    q_lora = w_uq.shape[1]
    kv_lora = w_ukv.shape[1]
    n_heads = (d - hw) // MLA_V
    n_experts = router_w.shape[2]
    alpha = float((2 * depth) ** 0.25)
    assert n_ctx % TOK_TILE == 0 and n_lat % ATT_TK == 0 and bsz < 8
    assert n_ctx % ATT_TQ == 0 and (bsz * t) % DISPATCH_TILE == 0 and t % COMBINE_TILE == 0

    lb_all = jnp.cumsum(jax.nn.softmax(hg_lb.astype(F32), axis=0), axis=0)
    lb_all = lb_all - lb_all[:1]

    c_rows = jnp.concatenate([c, c_ctx[None, :], jnp.zeros((8 - bsz - 1, d), F32)], axis=0)
    mod = _modulation(c_rows, w_mod, b_mod)
    mod3 = mod.reshape(depth * 8, 1, N_MOD * d)

    cos_t, sin_t = _rope_tables(n_ctx, n_lat)
    xc = jnp.concatenate([ctx, x], axis=1)
    nct = n_ctx // TOK_TILE

    n_tok = bsz * t
    n_assign = n_tok * TOP_K
    n_blocks = -(-(n_assign + n_experts * (ROW_BLOCK - 1)) // ROW_BLOCK)
    n_slots = n_blocks * ROW_BLOCK

    for l in range(depth):
        w_ext, wqa, wqb, wk, wv, wohg, womla = _prep_layer_weights(
            w_in[l], w_uq[l], w_ukv[l], w_out[l], hw, q_lora, kv_lora, n_heads)
        (lf_fw, lf_bw, k_fw, k_bw, q_hg, v_hg, sg, q_rot, q_unrot, kk, vv) = _inproj(
            xc, mod3, l, bsz, nct, w_ext, lb_all[l][None, :], q_norm_w[l][None, :], kv_norm_w[l][None, :],
            wqa, wqb, wk, wv, cos_t, sin_t, n_heads)
        o_fw = _gla(lf_fw, k_fw, q_hg, v_hg, nct, reverse=False)
        o_hg = _gla(lf_bw, k_bw, q_hg, v_hg, nct, reverse=True, extra=(o_fw, sg, hg_norm_w[l][None, :]))
        o_mla = _attention(q_unrot, q_rot, kk, vv, n_ctx, n_heads)
        x1, h2p, shared, idx, wts, pos, counts = _postmix(
            xc, o_hg, o_mla, mod3, l, bsz, nct, wohg, womla, ln1_w[l][None, :], ln1_b[l][None, :],
            router_w[l], router_bias[l][None, :], shared_w1[l].astype(BF16), shared_w3[l].astype(BF16),
            shared_w2[l].astype(BF16), alpha)

        counts = counts[:, 0]
        idx = jnp.swapaxes(idx, 1, 2)
        pos = jnp.swapaxes(pos, 1, 2)
        wts = jnp.swapaxes(wts, 1, 2)
        padded = (counts + ROW_BLOCK - 1) // ROW_BLOCK * ROW_BLOCK
        pad_end = jnp.cumsum(padded)
        pad_start = pad_end - padded
        dest = (pad_start[idx.reshape(n_tok, TOP_K)] + pos.reshape(n_tok, TOP_K)).astype(I32)
        block_first_row = jnp.arange(n_blocks, dtype=I32) * ROW_BLOCK
        block_expert = jnp.minimum(
            jnp.sum((pad_end[None, :] <= block_first_row[:, None]).astype(I32), axis=1), n_experts - 1)
        n_used = (pad_end[-1:] // ROW_BLOCK).astype(I32)

        xs = _dispatch(dest, h2p.reshape(n_tok, hw), n_slots, (pad_start + counts).astype(I32),
                       (padded - counts).astype(I32))
        ys = _experts(xs, block_expert, n_used, moe_w1, moe_w3, moe_w2, l)
        xc = _combine(dest, wts, ys, shared, x1, mod3, l, bsz, n_ctx // COMBINE_TILE,
                      ln2_w[l][None, :], ln2_b[l][None, :], alpha, latent_only=(l == depth - 1))
    return xc
```

```python
import functools

import jax
import jax.numpy as jnp
from jax import lax
from jax.experimental import pallas as pl
from jax.experimental.pallas import tpu as pltpu

F32 = jnp.float32
BF16 = jnp.bfloat16
U32 = jnp.uint32
I32 = jnp.int32
HIGHEST = lax.Precision.HIGHEST

HG_HEAD_DIM = 128
MLA_V = 64
MLA_NOPE = 64
MLA_ROPE = 32
GRID_W = 64
ROPE_BASE = 10000.0
TOP_K = 8
ROUTED_SCALE = 2.5
N_MOD = 6
LN_EPS = 1e-6
RMS_EPS = 1e-6

LANE = 128
TOK_TILE = 256
GLA_CHUNK = 128
GLA_SUB = 32
GLA_SAFE_EXPONENT = 80.0
ATT_TQ = 256
ATT_TK = 1024
ATT_HEADS = 2
LOG2_E = 1.4426950408889634
ROW_BLOCK = 512
DISPATCH_TILE = 256
COMBINE_TILE = 256
VMEM_LIMIT = 56 * 1024 * 1024
MASKED = -1e30


def _cparams(sem):
    return pltpu.CompilerParams(dimension_semantics=sem, vmem_limit_bytes=VMEM_LIMIT)


def _sigmoid(z):
    return 1.0 / (1.0 + jnp.exp(-z))


def _silu(z):
    return z * _sigmoid(z)


def _normalize(x):
    mu = jnp.mean(x, axis=-1, keepdims=True)
    xc = x - mu
    var = jnp.mean(xc * xc, axis=-1, keepdims=True)
    return xc * lax.rsqrt(var + LN_EPS)


def _pack_bf16_pair(lo, hi):
    lo_u = lax.bitcast_convert_type(lo.astype(BF16).astype(F32), U32) >> 16
    hi_u = lax.bitcast_convert_type(hi.astype(BF16).astype(F32), U32) & jnp.uint32(0xFFFF0000)
    return hi_u | lo_u


def _unpack_bf16_pair(u):
    lo = lax.bitcast_convert_type(u << 16, F32)
    hi = lax.bitcast_convert_type(u & jnp.uint32(0xFFFF0000), F32)
    return lo, hi


def _mod_kernel(c_ref, w_ref, b_ref, o_ref):
    c = c_ref[...]
    o_ref[0] = jnp.dot(_silu(c), w_ref[0], precision=HIGHEST, preferred_element_type=F32) + b_ref[0]


def _modulation(c_rows, w_mod, b_mod):
    depth, d, n = w_mod.shape
    tn = 1536
    return pl.pallas_call(
        _mod_kernel,
        out_shape=jax.ShapeDtypeStruct((depth, 8, n), F32),
        grid=(depth, n // tn),
        in_specs=[
            pl.BlockSpec((8, d), lambda l, j: (0, 0)),
            pl.BlockSpec((1, d, tn), lambda l, j: (l, 0, j)),
            pl.BlockSpec((1, 1, tn), lambda l, j: (l, 0, j)),
        ],
        out_specs=pl.BlockSpec((1, 8, tn), lambda l, j: (l, 0, j)),
        compiler_params=_cparams(("arbitrary", "arbitrary")),
        name="modulation",
    )(c_rows, w_mod, b_mod.reshape(depth, 1, n))


def _inproj_kernel(x_ref, mod_ref, w_ref, lb_ref, qnw_ref, kvnw_ref, wqa_ref, wqb_ref, wk_ref, wv_ref,
                   cos_ref, sin_ref,
                   lff_ref, lfb_ref, kf_ref, kb_ref, q_ref, v_ref, sg_ref, qr_ref, qu_ref, kk_ref, vv_ref,
                   *, d, hw, q_lora, kv_lora, n_heads):
    x = x_ref[0]
    m = mod_ref[0]
    shift = m[:, 0:d]
    scale = m[:, d:2 * d]
    h = (_normalize(x) * (1.0 + scale) + shift).astype(BF16)
    proj = jnp.dot(h, w_ref[...], preferred_element_type=F32)

    lb = lb_ref[...]

    def forget(z, lbd):
        f = lbd + (1.0 - lbd) * _sigmoid(z)
        return jnp.log(f), 1.0 - f

    lf, kd = forget(proj[:, 0:hw], lb[:, 0:hw])
    lff_ref[0] = lf
    kf_ref[0] = kd.astype(BF16)
    lf, kd = forget(proj[:, hw:2 * hw], lb[:, hw:2 * hw])
    lfb_ref[0] = lf
    kb_ref[0] = kd.astype(BF16)
    v_ref[0] = proj[:, 2 * hw:3 * hw].astype(BF16)
    q_ref[0] = (_silu(proj[:, 3 * hw:4 * hw]) * (HG_HEAD_DIM ** -0.5)).astype(BF16)
    sg_ref[0] = _silu(proj[:, 4 * hw:5 * hw]).astype(BF16)

    o = 5 * hw
    cq = proj[:, o:o + q_lora]
    ckv = proj[:, o + q_lora:o + q_lora + kv_lora]
    o2 = o + q_lora + kv_lora
    kpe_a = proj[:, o2:o2 + LANE]
    kpe_b = proj[:, o2 + LANE:o2 + 2 * LANE]

    cqn = (cq * lax.rsqrt(jnp.mean(cq * cq, axis=-1, keepdims=True) + RMS_EPS) * qnw_ref[...]).astype(BF16)
    ckvn = (ckv * lax.rsqrt(jnp.mean(ckv * ckv, axis=-1, keepdims=True) + RMS_EPS) * kvnw_ref[...]).astype(BF16)

    cos = cos_ref[...]
    sin = sin_ref[...]
    cos_h = jnp.concatenate([cos] * n_heads, axis=1)
    sin_h = jnp.concatenate([sin] * n_heads, axis=1)
    att_scale = (MLA_NOPE + MLA_ROPE) ** -0.5 * LOG2_E
    qa = jnp.dot(cqn, wqa_ref[...], preferred_element_type=F32)
    qb = jnp.dot(cqn, wqb_ref[...], preferred_element_type=F32)
    qu_ref[0] = (qa * att_scale).astype(BF16)
    qr_ref[0] = ((qa * cos_h + qb * sin_h) * att_scale).astype(BF16)

    kr = kpe_a * cos + kpe_b * sin
    kk = jnp.dot(ckvn, wk_ref[...], preferred_element_type=F32) + jnp.concatenate([kr] * n_heads, axis=1)
    kk_ref[0] = kk.astype(BF16)
    vv = jnp.dot(ckvn, wv_ref[...], preferred_element_type=F32)
    lane = lax.broadcasted_iota(I32, vv.shape, 1)
    vv = jnp.where((lane % LANE) == MLA_V, 1.0, vv)
    vv_ref[0] = vv.astype(BF16)


def _inproj(xc, mod3, layer, n_batch, nct, w_ext, lb, qnw, kvnw, wqa, wqb, wk, wv, cos_t, sin_t, n_heads):
    b, t, d = xc.shape
    hw = d // 2
    q_lora = wqa.shape[0]
    kv_lora = wk.shape[0]
    hp = n_heads * LANE
    tm = TOK_TILE
    full = lambda shape: pl.BlockSpec(shape, lambda bi, ti: tuple(0 for _ in shape))
    tok = lambda w: pl.BlockSpec((1, tm, w), lambda bi, ti: (bi, ti, 0))
    out_shape = (
        jax.ShapeDtypeStruct((b, t, hw), F32), jax.ShapeDtypeStruct((b, t, hw), F32),
        jax.ShapeDtypeStruct((b, t, hw), BF16), jax.ShapeDtypeStruct((b, t, hw), BF16),
        jax.ShapeDtypeStruct((b, t, hw), BF16), jax.ShapeDtypeStruct((b, t, hw), BF16),
        jax.ShapeDtypeStruct((b, t, hw), BF16),
        jax.ShapeDtypeStruct((b, t, hp), BF16), jax.ShapeDtypeStruct((b, t, hp), BF16),
        jax.ShapeDtypeStruct((b, t, hp), BF16), jax.ShapeDtypeStruct((b, t, hp), BF16),
    )
    return pl.pallas_call(
        functools.partial(_inproj_kernel, d=d, hw=hw, q_lora=q_lora, kv_lora=kv_lora, n_heads=n_heads),
        out_shape=out_shape,
        grid=(b, t // tm),
        in_specs=[
            tok(d),
            pl.BlockSpec((1, 1, N_MOD * d), lambda bi, ti: (layer * 8 + jnp.where(ti < nct, n_batch, bi), 0, 0)),
            full(w_ext.shape), full(lb.shape), full(qnw.shape), full(kvnw.shape),
            full(wqa.shape), full(wqb.shape), full(wk.shape), full(wv.shape),
            pl.BlockSpec((tm, LANE), lambda bi, ti: (ti, 0)),
            pl.BlockSpec((tm, LANE), lambda bi, ti: (ti, 0)),
        ],
        out_specs=tuple([tok(hw)] * 7 + [tok(hp)] * 4),
        compiler_params=_cparams(("arbitrary", "arbitrary")),
        name="inproj",
    )(xc, mod3, w_ext, lb, qnw, kvnw, wqa, wqb, wk, wv, cos_t, sin_t)


def _gla_kernel(*refs, reverse, n_chunks, n_heads, final):
    if final:
        lf_ref, k_ref, q_ref, v_ref, ofw_ref, sg_ref, nw_ref, o_ref, st_ref = refs
    else:
        lf_ref, k_ref, q_ref, v_ref, o_ref, st_ref = refs
    c, s = GLA_CHUNK, GLA_SUB
    n_sub = c // s

    @pl.when(pl.program_id(1) == 0)
    def _():
        st_ref[...] = jnp.zeros_like(st_ref)

    row = lax.broadcasted_iota(I32, (c, c), 0)
    col = lax.broadcasted_iota(I32, (c, c), 1)
    tri_b = jnp.where((row <= col) if reverse else (row >= col), 1.0, 0.0).astype(BF16)
    sub_row = lax.broadcasted_iota(I32, (s, HG_HEAD_DIM), 0)
    ones = jnp.ones((HG_HEAD_DIM, HG_HEAD_DIM), BF16)

    nt = (((1,), (1,)), ((), ()))

    def sub_geometry(si):
        rs = slice(si * s, (si + 1) * s)
        if reverse:
            return rs, (slice((si + 1) * s, c), (si + 1) * s) if si < n_sub - 1 else (None, None)
        return rs, (slice(0, si * s), si * s - 1) if si > 0 else (None, None)

    def chunk_body(i, carry):
        ci = (n_chunks - 1 - i) if reverse else i
        r0 = pl.multiple_of(ci * c, c)
        rows = pl.ds(r0, c)
        lf_all = lf_ref[0, rows, :]
        lf_hi = lf_all.astype(BF16)
        lf_r = lf_all - lf_hi.astype(F32)
        lf_mid = lf_r.astype(BF16)
        lf_lo = (lf_r - lf_mid.astype(F32)).astype(BF16)
        bc_all = (jnp.dot(tri_b, lf_hi, preferred_element_type=F32)
                  + jnp.dot(tri_b, lf_mid, preferred_element_type=F32)
                  + jnp.dot(tri_b, lf_lo, preferred_element_type=F32))
        worst = None
        for si in range(n_sub):
            rs, (_, brow) = sub_geometry(si)
            far = bc_all[rs.start:rs.start + 1] if reverse else bc_all[rs.stop - 1:rs.stop]
            d = far if brow is None else far - bc_all[brow:brow + 1]
            worst = d if worst is None else jnp.minimum(worst, d)

        def load(h):
            sl = slice(h * HG_HEAD_DIM, (h + 1) * HG_HEAD_DIM)
            return (sl, bc_all[:, sl], k_ref[0, rows, sl].astype(F32), q_ref[0, rows, sl].astype(F32),
                    v_ref[0, rows, sl].astype(F32))

        def store(sl, o):
            if final:
                o = o + ofw_ref[0, rows, sl]
                o = o * lax.rsqrt(jnp.mean(o * o, axis=-1, keepdims=True) + RMS_EPS) * nw_ref[...]
                o_ref[0, rows, sl] = (o * sg_ref[0, rows, sl].astype(F32)).astype(o_ref.dtype)
            else:
                o_ref[0, rows, sl] = o

        def intra(h, fast):
            sl, bc, k, q, v = load(h)
            qhat = (q * jnp.exp(bc)).astype(BF16)
            o_state = lax.dot_general(qhat, st_ref[h].astype(BF16), nt, preferred_element_type=F32)
            pieces = []
            for si in range(n_sub):
                rs, (src, brow) = sub_geometry(si)
                bs, qs, ks, vs = bc[rs], q[rs], k[rs], v[rs]
                acc = o_state[rs]
                beta = jnp.zeros((1, HG_HEAD_DIM), F32) if brow is None else bc[brow:brow + 1]
                if fast:
                    src = slice(rs.start, c) if reverse else slice(0, rs.stop)
                if src is not None:
                    qi = (qs * jnp.exp(bs - beta)).astype(BF16)
                    ksrc = (k[src] * jnp.exp(beta - bc[src])).astype(BF16)
                    a = lax.dot_general(qi, ksrc, nt, preferred_element_type=F32)
                    if fast:
                        n_src = src.stop - src.start
                        r_i = lax.broadcasted_iota(I32, (s, n_src), 0)
                        c_i = lax.broadcasted_iota(I32, (s, n_src), 1)
                        keep = (c_i >= r_i) if reverse else (c_i <= r_i + rs.start)
                        a = jnp.where(keep, a, 0.0)
                    acc = acc + jnp.dot(a.astype(BF16), v[src].astype(BF16), preferred_element_type=F32)
                if not fast:
                    ws = []
                    for j in range(s):
                        mask = (sub_row <= j) if reverse else (sub_row >= j)
                        dlt = jnp.where(mask, bs - bs[j:j + 1], MASKED)
                        ws.append((jnp.exp(dlt) * qs * ks[j:j + 1]).astype(BF16))
                    sums = jnp.dot(jnp.concatenate(ws, axis=0), ones, preferred_element_type=F32)
                    for j in range(s):
                        acc = acc + sums[j * s:(j + 1) * s] * vs[j:j + 1]
                pieces.append(acc)
            store(sl, jnp.concatenate(pieces, axis=0))

        bounded = jnp.min(worst) > -GLA_SAFE_EXPONENT

        @pl.when(bounded)
        def _():
            for h in range(n_heads):
                intra(h, True)

        @pl.when(jnp.logical_not(bounded))
        def _():
            for h in range(n_heads):
                intra(h, False)

        for h in range(n_heads):
            sl, bc, k, _, v = load(h)
            tot = bc[0:1] if reverse else bc[c - 1:c]
            khat = (k * jnp.exp(tot - bc)).astype(BF16)
            st_ref[h] = st_ref[h] * jnp.exp(tot) + jnp.dot(v.T.astype(BF16), khat, preferred_element_type=F32)
        return carry

    lax.fori_loop(0, n_chunks, chunk_body, 0, unroll=True)


def _gla(lf, k, q, v, nct_blocks, reverse, extra=None):
    b, t, hw = lf.shape
    n_heads = hw // HG_HEAD_DIM
    blk = TOK_TILE
    nb = t // blk

    def blk_index(g):
        if not reverse:
            return g
        return jnp.where(g < nct_blocks, nct_blocks - 1 - g, nb - 1 - (g - nct_blocks))

    spec = pl.BlockSpec((1, blk, hw), lambda bi, g: (bi, blk_index(g), 0))
    in_specs = [spec, spec, spec, spec]
    args = [lf, k, q, v]
    final = extra is not None
    if final:
        ofw, sg, nw = extra
        in_specs += [spec, spec, pl.BlockSpec(nw.shape, lambda bi, g: (0, 0))]
        args += [ofw, sg, nw]
    return pl.pallas_call(
        functools.partial(_gla_kernel, reverse=reverse, n_chunks=blk // GLA_CHUNK, n_heads=n_heads, final=final),
        out_shape=jax.ShapeDtypeStruct((b, t, hw), BF16 if final else F32),
        grid=(b, nb),
        in_specs=in_specs,
        out_specs=spec,
        scratch_shapes=[pltpu.VMEM((n_heads, HG_HEAD_DIM, HG_HEAD_DIM), F32)],
        compiler_params=_cparams(("arbitrary", "arbitrary")),
        name="gla_bwd" if reverse else "gla_fwd",
    )(*args)


def _attn_kernel(qu_ref, qr_ref, k_ref, v_ref, o_ref, s_sc, *, n_ctx, n_lat):
    is_ctx = pl.program_id(2) < n_ctx // ATT_TQ
    nt = (((1,), (1,)), ((), ()))
    n_chunks = n_lat // ATT_TK
    heads = [slice(h * LANE, (h + 1) * LANE) for h in range(ATT_HEADS)]

    def ctx_scores(hs):
        return lax.dot_general(qu_ref[0, :, hs], k_ref[0, 0:n_ctx, hs], nt, preferred_element_type=F32)

    def finish(hs, acc):
        o_ref[0, :, hs] = (acc * (1.0 / acc[:, MLA_V:MLA_V + 1])).astype(o_ref.dtype)

    def lane_tile_max(s, m):
        for c in range(s.shape[1] // LANE):
            t = s[:, c * LANE:(c + 1) * LANE]
            m = t if m is None else jnp.maximum(m, t)
        return m

    @pl.when(is_ctx)
    def _():
        for hs in heads:
            s = ctx_scores(hs)
            p = jnp.exp2(s - jnp.max(s, axis=-1, keepdims=True))
            finish(hs, jnp.dot(p.astype(BF16), v_ref[0, 0:n_ctx, hs], preferred_element_type=F32))

    @pl.when(jnp.logical_not(is_ctx))
    def _():
        ms, accs = [], []
        for hi, hs in enumerate(heads):
            qr = qr_ref[0, :, hs]
            s_c = ctx_scores(hs)
            m_t = lane_tile_max(s_c, None)
            for j in range(n_chunks):
                rows = slice(n_ctx + j * ATT_TK, n_ctx + (j + 1) * ATT_TK)
                s = lax.dot_general(qr, k_ref[0, rows, hs], nt, preferred_element_type=F32)
                s_sc[hi, j] = s
                m_t = lane_tile_max(s, m_t)
            m = jnp.max(m_t, axis=-1, keepdims=True)
            ms.append(m)
            accs.append(jnp.dot(jnp.exp2(s_c - m).astype(BF16), v_ref[0, 0:n_ctx, hs],
                                preferred_element_type=F32))

        for j in range(n_chunks):
            rows = slice(n_ctx + j * ATT_TK, n_ctx + (j + 1) * ATT_TK)
            for hi, hs in enumerate(heads):
                p = jnp.exp2(s_sc[hi, j] - ms[hi]).astype(BF16)
                accs[hi] = accs[hi] + jnp.dot(p, v_ref[0, rows, hs], preferred_element_type=F32)
        for hi, hs in enumerate(heads):
            finish(hs, accs[hi])


def _attention(qu, qr, kk, vv, n_ctx, n_heads):
    b, t, hp = qu.shape
    n_lat = t - n_ctx
    w = ATT_HEADS * LANE
    qspec = pl.BlockSpec((1, ATT_TQ, w), lambda bi, h, qi: (bi, qi, h))
    kspec = pl.BlockSpec((1, t, w), lambda bi, h, qi: (bi, 0, h))
    return pl.pallas_call(
        functools.partial(_attn_kernel, n_ctx=n_ctx, n_lat=n_lat),
        out_shape=jax.ShapeDtypeStruct((b, t, hp), BF16),
        grid=(b, n_heads // ATT_HEADS, t // ATT_TQ),
        in_specs=[qspec, qspec, kspec, kspec],
        out_specs=qspec,
        scratch_shapes=[pltpu.VMEM((ATT_HEADS, n_lat // ATT_TK, ATT_TQ, ATT_TK), F32)],
        compiler_params=_cparams(("arbitrary", "arbitrary", "arbitrary")),
        name="mla_attention",
    )(qu, qr, kk, vv)


def _postmix_kernel(x_ref, ohg_ref, omla_ref, mod_ref, wohg_ref, womla_ref, ln1w_ref, ln1b_ref,
                    rwh_ref, rwl_ref, rb_ref, sw1_ref, sw3_ref, sw2_ref,
                    x1_ref, h2p_ref, sh_ref, idx_ref, wts_ref, pos_ref, cnt_ref, cnt_sc,
                    *, d, alpha, n_experts):
    @pl.when((pl.program_id(0) == 0) & (pl.program_id(1) == 0))
    def _():
        cnt_sc[...] = jnp.zeros_like(cnt_sc)

    m = mod_ref[0]
    g_a = m[:, 2 * d:3 * d]
    sh_f = m[:, 3 * d:4 * d]
    sc_f = m[:, 4 * d:5 * d]
    mix = (jnp.dot(ohg_ref[0], wohg_ref[...], preferred_element_type=F32)
           + jnp.dot(omla_ref[0], womla_ref[...], preferred_element_type=F32))
    x1 = _normalize(alpha * x_ref[0] + g_a * mix) * ln1w_ref[...] + ln1b_ref[...]
    x1_ref[0] = x1
    h2 = _normalize(x1) * (1.0 + sc_f) + sh_f
    hw = d // 2
    h2p_ref[0] = _pack_bf16_pair(h2[:, 0:hw], h2[:, hw:d])
    h2b = h2.astype(BF16)

    a1 = jnp.dot(h2b, sw1_ref[...], preferred_element_type=F32)
    a3 = jnp.dot(h2b, sw3_ref[...], preferred_element_type=F32)
    hid = (_silu(a1) * a3).astype(BF16)
    sh_ref[0] = jnp.dot(hid, sw2_ref[...], preferred_element_type=F32).astype(sh_ref.dtype)

    nt = (((1,), (1,)), ((), ()))
    h2lo = (h2 - h2b.astype(F32)).astype(BF16)
    logits = (lax.dot_general(rwh_ref[...], h2b, nt, preferred_element_type=F32)
              + lax.dot_general(rwh_ref[...], h2lo, nt, preferred_element_type=F32)
              + lax.dot_general(rwl_ref[...], h2b, nt, preferred_element_type=F32))
    scores = _sigmoid(logits)
    sel = scores + rb_ref[...][:, 0:1]
    tm = scores.shape[1]
    erow = lax.broadcasted_iota(I32, (n_experts, tm), 0).astype(F32)
    masks, tops, idxs = [], [], []
    for _ in range(TOP_K):
        mx = jnp.max(sel, axis=0, keepdims=True)
        ik = jnp.min(jnp.where(sel == mx, erow, float(n_experts)), axis=0, keepdims=True)
        oh = erow == ik
        masks.append(oh)
        idxs.append(ik)
        tops.append(jnp.sum(jnp.where(oh, scores, 0.0), axis=0, keepdims=True))
        sel = jnp.where(oh, -jnp.inf, sel)
    tsum = tops[0]
    for tk in tops[1:]:
        tsum = tsum + tk
    inv = ROUTED_SCALE / tsum

    ohf = jnp.zeros((n_experts, tm), F32)
    for oh in masks:
        ohf = jnp.where(oh, 1.0, ohf)
    r = lax.broadcasted_iota(I32, (tm, tm), 0)
    cc = lax.broadcasted_iota(I32, (tm, tm), 1)
    earlier = jnp.where(r < cc, 1.0, 0.0).astype(BF16)
    cnt = cnt_sc[...][:, 0:1]
    before = jnp.dot(ohf.astype(BF16), earlier, preferred_element_type=F32) + cnt
    cnt_new = cnt + jnp.sum(ohf, axis=1, keepdims=True)
    cnt_sc[...] = jnp.broadcast_to(cnt_new, cnt_sc.shape)
    cnt_ref[...] = jnp.broadcast_to(cnt_new, cnt_ref.shape).astype(I32)

    row_k = lax.broadcasted_iota(I32, (TOP_K, tm), 0)
    idx_o = jnp.zeros((TOP_K, tm), I32)
    wts_o = jnp.zeros((TOP_K, tm), F32)
    pos_o = jnp.zeros((TOP_K, tm), I32)
    for kk in range(TOP_K):
        pk = jnp.sum(jnp.where(masks[kk], before, 0.0), axis=0, keepdims=True)
        hit = row_k == kk
        idx_o = jnp.where(hit, idxs[kk].astype(I32), idx_o)
        wts_o = jnp.where(hit, tops[kk] * inv, wts_o)
        pos_o = jnp.where(hit, pk.astype(I32), pos_o)
    idx_ref[0] = idx_o
    wts_ref[0] = wts_o
    pos_ref[0] = pos_o


def _postmix(xc, ohg, omla, mod3, layer, n_batch, nct, wohg, womla, ln1w, ln1b, rw, rb, sw1, sw3, sw2, alpha):
    b, t, d = xc.shape
    tm = TOK_TILE
    n_experts = rw.shape[1]
    rwt = rw.T
    rwh = rwt.astype(BF16)
    rwl = (rwt - rwh.astype(F32)).astype(BF16)
    rb = jnp.broadcast_to(rb.reshape(n_experts, 1), (n_experts, LANE))
    full = lambda a: pl.BlockSpec(a.shape, lambda bi, ti: tuple(0 for _ in a.shape))
    tok = lambda w: pl.BlockSpec((1, tm, w), lambda bi, ti: (bi, ti, 0))
    out_shape = (
        jax.ShapeDtypeStruct((b, t, d), F32),
        jax.ShapeDtypeStruct((b, t, d // 2), U32),
        jax.ShapeDtypeStruct((b, t, d), BF16),
        jax.ShapeDtypeStruct((b, TOP_K, t), I32),
        jax.ShapeDtypeStruct((b, TOP_K, t), F32),
        jax.ShapeDtypeStruct((b, TOP_K, t), I32),
        jax.ShapeDtypeStruct((n_experts, LANE), I32),
    )
    kmaj = pl.BlockSpec((1, TOP_K, tm), lambda bi, ti: (bi, 0, ti))
    return pl.pallas_call(
        functools.partial(_postmix_kernel, d=d, alpha=alpha, n_experts=n_experts),
        out_shape=out_shape,
        grid=(b, t // tm),
        in_specs=[
            tok(d), tok(ohg.shape[2]), tok(omla.shape[2]),
            pl.BlockSpec((1, 1, N_MOD * d), lambda bi, ti: (layer * 8 + jnp.where(ti < nct, n_batch, bi), 0, 0)),
            full(wohg), full(womla), full(ln1w), full(ln1b), full(rwh), full(rwl), full(rb),
            full(sw1), full(sw3), full(sw2),
        ],
        out_specs=(tok(d), tok(d // 2), tok(d), kmaj, kmaj, kmaj,
                   pl.BlockSpec((n_experts, LANE), lambda bi, ti: (0, 0))),
        scratch_shapes=[pltpu.VMEM((n_experts, LANE), F32)],
        compiler_params=_cparams(("arbitrary", "arbitrary")),
        name="postmix_router",
    )(xc, ohg, omla, mod3, wohg, womla, ln1w, ln1b, rwh, rwl, rb, sw1, sw3, sw2)


def _row_copy(src_ref, src_row, dst_ref, dst_row, sem):
    return pltpu.make_async_copy(src_ref.at[pl.ds(src_row, 1)], dst_ref.at[pl.ds(dst_row, 1)], sem)


def _dispatch_kernel(pad_from_ref, pad_len_ref, dest_ref, h_ref, xs_ref, zeros, sem, zsem, *, n_experts):
    tile_rows = zeros.shape[0]

    def for_pad_rows(act):
        def per_expert(e, c):
            first, n = pad_from_ref[e], pad_len_ref[e]
            head = jnp.minimum((-first) & (tile_rows - 1), n)

            def per_row(r, c2):
                act(_row_copy(zeros, 0, xs_ref, first + r, zsem))
                return c2

            def per_tile(i, c2):
                start = pl.multiple_of(first + head + i * tile_rows, tile_rows)
                act(pltpu.make_async_copy(zeros, xs_ref.at[pl.ds(start, tile_rows)], zsem))
                return c2

            c = lax.fori_loop(0, head, per_row, c)
            return lax.fori_loop(0, lax.shift_right_logical(n - head, tile_rows.bit_length() - 1), per_tile, c)
        lax.fori_loop(0, n_experts, per_expert, 0)

    @pl.when(pl.program_id(0) == 0)
    def _():
        zeros[...] = jnp.zeros_like(zeros)
        for_pad_rows(lambda cp: cp.start())

    def issue(t, c):
        for kk in range(TOP_K):
            _row_copy(h_ref, t, xs_ref, dest_ref[0, 0, t * TOP_K + kk], sem).start(priority=kk % 2)
        return c

    lax.fori_loop(0, DISPATCH_TILE, issue, 0, unroll=2)

    def drain(t, c):
        for kk in range(TOP_K):
            _row_copy(h_ref, 0, xs_ref, 0, sem).wait()
        return c

    lax.fori_loop(0, DISPATCH_TILE, drain, 0, unroll=2)

    @pl.when(pl.program_id(0) == 0)
    def _():
        for_pad_rows(lambda cp: cp.wait())


def _dispatch(dest, h2p, n_slots, pad_from, pad_len):
    n, w = h2p.shape
    steps = n // DISPATCH_TILE
    dest3 = dest.reshape(steps, 1, DISPATCH_TILE * TOP_K)
    grid_spec = pltpu.PrefetchScalarGridSpec(
        num_scalar_prefetch=2,
        grid=(steps,),
        in_specs=[
            pl.BlockSpec((1, 1, DISPATCH_TILE * TOP_K), lambda i, pf, pn: (i, 0, 0), memory_space=pltpu.SMEM),
            pl.BlockSpec((DISPATCH_TILE, w), lambda i, pf, pn: (i, 0)),
        ],
        out_specs=pl.BlockSpec(memory_space=pl.ANY),
        scratch_shapes=[pltpu.VMEM((8, w), U32), pltpu.SemaphoreType.DMA, pltpu.SemaphoreType.DMA],
    )
    return pl.pallas_call(
        functools.partial(_dispatch_kernel, n_experts=pad_from.shape[0]),
        out_shape=jax.ShapeDtypeStruct((n_slots, w), U32),
        grid_spec=grid_spec,
        compiler_params=_cparams(("arbitrary",)),
        name="moe_dispatch",
    )(pad_from, pad_len, dest3, h2p)


def _expert_kernel(be_ref, nused_ref, xs_ref, w1_ref, w3_ref, w2_ref, ys_ref, w1b, w3b, w2b):
    i = pl.program_id(0)
    used = i < nused_ref[0]

    @pl.when(used)
    def _():
        changed = (i == 0) | (be_ref[i] != be_ref[jnp.maximum(i - 1, 0)])

        @pl.when(changed)
        def _():
            w1b[...] = w1_ref[0, 0].astype(BF16)
            w3b[...] = w3_ref[0, 0].astype(BF16)
            w2b[...] = w2_ref[0, 0].astype(BF16)

        lo, hi = _unpack_bf16_pair(xs_ref[...])
        x = jnp.concatenate([lo.astype(BF16), hi.astype(BF16)], axis=1)
        h1 = jnp.dot(x, w1b[...], preferred_element_type=F32)
        h3 = jnp.dot(x, w3b[...], preferred_element_type=F32)
        hid = (_silu(h1) * h3).astype(BF16)
        out = jnp.dot(hid, w2b[...], preferred_element_type=F32)
        half = out.shape[1] // 2
        ys_ref[...] = _pack_bf16_pair(out[:, 0:half], out[:, half:])

    @pl.when(jnp.logical_not(used))
    def _():
        ys_ref[...] = jnp.zeros_like(ys_ref)


def _experts(xs, block_expert, n_used, w1, w3, w2, layer):
    n_slots, w = xs.shape
    n_blocks = n_slots // ROW_BLOCK
    _, _, d, hid = w1.shape
    grid_spec = pltpu.PrefetchScalarGridSpec(
        num_scalar_prefetch=2,
        grid=(n_blocks,),
        in_specs=[
            pl.BlockSpec((ROW_BLOCK, w), lambda i, be, nu: (jnp.minimum(i, nu[0] - 1), 0)),
            pl.BlockSpec((1, 1, d, hid), lambda i, be, nu: (layer, be[i], 0, 0)),
            pl.BlockSpec((1, 1, d, hid), lambda i, be, nu: (layer, be[i], 0, 0)),
            pl.BlockSpec((1, 1, hid, d), lambda i, be, nu: (layer, be[i], 0, 0)),
        ],
        out_specs=pl.BlockSpec((ROW_BLOCK, w), lambda i, be, nu: (i, 0)),
        scratch_shapes=[pltpu.VMEM((d, hid), BF16), pltpu.VMEM((d, hid), BF16), pltpu.VMEM((hid, d), BF16)],
    )
    return pl.pallas_call(
        _expert_kernel,
        out_shape=jax.ShapeDtypeStruct((n_slots, w), U32),
        grid_spec=grid_spec,
        compiler_params=_cparams(("arbitrary",)),
        name="moe_experts",
    )(block_expert, n_used, xs, w1, w3, w2)


def _combine_kernel(dest_ref, dnext_ref, wts_ref, ys_ref, sh_ref, x1_ref, mod_ref, lnw_ref, lnb_ref, o_ref,
                    buf, sems, *, d, alpha, n_steps):
    g = pl.program_id(0) * pl.num_programs(1) + pl.program_id(1)
    cur = g % 2
    nxt = 1 - cur

    def issue_tile(tbl_ref, slot):
        def body(t, c):
            for kk in range(TOP_K):
                pltpu.make_async_copy(ys_ref.at[pl.ds(tbl_ref[0, 0, t * TOP_K + kk], 1)],
                                      buf.at[slot, kk, pl.ds(t, 1)], sems.at[slot]).start(priority=kk % 2)
            return c
        lax.fori_loop(0, COMBINE_TILE, body, 0, unroll=2)

    def drain(slot):
        def body(t, c):
            for _ in range(TOP_K):
                pltpu.make_async_copy(ys_ref.at[pl.ds(0, 1)], buf.at[slot, 0, pl.ds(0, 1)], sems.at[slot]).wait()
            return c
        lax.fori_loop(0, COMBINE_TILE, body, 0, unroll=2)

    @pl.when(g == 0)
    def _():
        issue_tile(dest_ref, 0)

    issue_tile(dnext_ref, nxt)
    drain(cur)

    hw = d // 2
    sh = sh_ref[0].astype(F32)
    acc_lo = sh[:, 0:hw]
    acc_hi = sh[:, hw:d]
    wts = wts_ref[0]
    for kk in range(TOP_K):
        lo, hi = _unpack_bf16_pair(buf[cur, kk])
        wk = wts[:, kk:kk + 1]
        acc_lo = acc_lo + wk * lo
        acc_hi = acc_hi + wk * hi
    ff = jnp.concatenate([acc_lo, acc_hi], axis=1)
    g_f = mod_ref[0][:, 5 * d:6 * d]
    o_ref[0] = _normalize(alpha * x1_ref[0] + g_f * ff) * lnw_ref[...] + lnb_ref[...]

    @pl.when(g == n_steps - 1)
    def _():
        drain(nxt)


def _combine(dest, wts, ys, shared, x1, mod3, layer, n_batch, nct_tiles, lnw, lnb, alpha, latent_only):
    b, t, d = x1.shape
    tm = COMBINE_TILE
    tpb = t // tm
    n_steps = b * tpb
    if latent_only:
        out_rows = t - nct_tiles * tm
        out_spec = pl.BlockSpec((1, tm, d), lambda bi, ti: (bi, jnp.maximum(ti - nct_tiles, 0), 0))
    else:
        out_rows = t
        out_spec = pl.BlockSpec((1, tm, d), lambda bi, ti: (bi, ti, 0))
    dest3 = dest.reshape(n_steps, 1, tm * TOP_K)
    tok = lambda w: pl.BlockSpec((1, tm, w), lambda bi, ti: (bi, ti, 0))
    full = lambda a: pl.BlockSpec(a.shape, lambda bi, ti: tuple(0 for _ in a.shape))
    table = lambda shift: pl.BlockSpec(
        (1, 1, tm * TOP_K), lambda bi, ti: (jnp.minimum(bi * tpb + ti + shift, n_steps - 1), 0, 0),
        memory_space=pltpu.SMEM)
    return pl.pallas_call(
        functools.partial(_combine_kernel, d=d, alpha=alpha, n_steps=n_steps),
        out_shape=jax.ShapeDtypeStruct((b, out_rows, d), F32),
        grid=(b, tpb),
        in_specs=[
            table(0), table(1),
            tok(TOP_K),
            pl.BlockSpec(memory_space=pl.ANY),
            tok(d), tok(d),
            pl.BlockSpec((1, 1, N_MOD * d), lambda bi, ti: (layer * 8 + jnp.where(ti < nct_tiles, n_batch, bi), 0, 0)),
            full(lnw), full(lnb),
        ],
        out_specs=out_spec,
        scratch_shapes=[pltpu.VMEM((2, TOP_K, tm, d // 2), U32), pltpu.SemaphoreType.DMA((2,))],
        compiler_params=_cparams(("arbitrary", "arbitrary")),
        name="moe_combine",
    )(dest3, dest3, wts, ys, shared, x1, mod3, lnw, lnb)


def _rope_tables(n_ctx, n_lat):
    pos = jnp.arange(n_lat, dtype=I32)
    rowp = (pos // GRID_W).astype(F32)
    colp = (pos % GRID_W).astype(F32)
    n_freq = MLA_ROPE // 4
    inv = ROPE_BASE ** (-jnp.arange(n_freq, dtype=F32) / n_freq)
    ang = jnp.concatenate([rowp[:, None] * inv, colp[:, None] * inv], axis=-1)
    ang = jnp.concatenate([jnp.zeros((n_ctx, MLA_ROPE // 2), F32), ang], axis=0)
    t = n_ctx + n_lat
    ones = jnp.ones((t, MLA_NOPE), F32)
    zeros_tail = jnp.zeros((t, LANE - MLA_NOPE - MLA_ROPE), F32)
    cos_t = jnp.concatenate([ones, jnp.cos(ang), jnp.cos(ang), zeros_tail], axis=1)
    sin_t = jnp.concatenate([jnp.zeros((t, MLA_NOPE), F32), jnp.sin(ang), jnp.sin(ang), zeros_tail], axis=1)
    return cos_t, sin_t


def _rope_swap(w):
    half = MLA_ROPE // 2
    return jnp.concatenate([-w[..., half:], w[..., :half]], axis=-1)


def _prep_layer_weights(w_in_l, w_uq_l, w_ukv_l, w_out_l, hw, q_lora, kv_lora, n_heads):
    d = w_in_l.shape[0]
    base = 5 * hw + q_lora + kv_lora
    kpe_w = w_in_l[:, base:base + MLA_ROPE]
    z_nope = jnp.zeros((d, MLA_NOPE), F32)
    z_tail = jnp.zeros((d, LANE - MLA_NOPE - MLA_ROPE), F32)
    w_ext = jnp.concatenate([w_in_l[:, :base], z_nope, kpe_w, z_tail, z_nope, _rope_swap(kpe_w), z_tail],
                            axis=1).astype(BF16)
    wq = w_uq_l.reshape(q_lora, n_heads, MLA_NOPE + MLA_ROPE)
    zq = jnp.zeros((q_lora, n_heads, LANE - MLA_NOPE - MLA_ROPE), F32)
    wqa = jnp.concatenate([wq, zq], axis=-1).reshape(q_lora, n_heads * LANE).astype(BF16)
    wqb = jnp.concatenate([jnp.zeros((q_lora, n_heads, MLA_NOPE), F32), _rope_swap(wq[..., MLA_NOPE:]), zq],
                          axis=-1).reshape(q_lora, n_heads * LANE).astype(BF16)
    wkv = w_ukv_l.reshape(kv_lora, n_heads, MLA_NOPE + MLA_V)
    wk = jnp.concatenate([wkv[..., :MLA_NOPE], jnp.zeros((kv_lora, n_heads, LANE - MLA_NOPE), F32)],
                         axis=-1).reshape(kv_lora, n_heads * LANE).astype(BF16)
    wv = jnp.concatenate([wkv[..., MLA_NOPE:], jnp.zeros((kv_lora, n_heads, LANE - MLA_V), F32)],
                         axis=-1).reshape(kv_lora, n_heads * LANE).astype(BF16)
    wohg = w_out_l[:hw].astype(BF16)
    wom = w_out_l[hw:].reshape(n_heads, MLA_V, d)
    womla = jnp.concatenate([wom, jnp.zeros((n_heads, LANE - MLA_V, d), F32)], axis=1)
    womla = womla.reshape(n_heads * LANE, d).astype(BF16)
    return w_ext, wqa, wqb, wk, wv, wohg, womla


def kernel(x, c, ctx, c_ctx, w_mod, b_mod, w_in, hg_lb, hg_norm_w, q_norm_w, w_uq, kv_norm_w, w_ukv, w_out,
           ln1_w, ln1_b, router_w, router_bias, moe_w1, moe_w3, moe_w2, shared_w1, shared_w3, shared_w2,
           ln2_w, ln2_b):
    bsz, n_lat, d = x.shape
    n_ctx = ctx.shape[1]
    depth = w_mod.shape[0]
    t = n_ctx + n_lat
    hw = d // 2
    q_lora = w_uq.shape[1]
    kv_lora = w_ukv.shape[1]
    n_heads = (d - hw) // MLA_V
    n_experts = router_w.shape[2]
    alpha = float((2 * depth) ** 0.25)
    assert n_ctx % TOK_TILE == 0 and n_lat % ATT_TK == 0 and bsz < 8
    assert n_ctx % ATT_TQ == 0 and (bsz * t) % DISPATCH_TILE == 0 and t % COMBINE_TILE == 0

    lb_all = jnp.cumsum(jax.nn.softmax(hg_lb.astype(F32), axis=0), axis=0)
    lb_all = lb_all - lb_all[:1]

    c_rows = jnp.concatenate([c, c_ctx[None, :], jnp.zeros((8 - bsz - 1, d), F32)], axis=0)
    mod = _modulation(c_rows, w_mod, b_mod)
    mod3 = mod.reshape(depth * 8, 1, N_MOD * d)

    cos_t, sin_t = _rope_tables(n_ctx, n_lat)
    xc = jnp.concatenate([ctx, x], axis=1)
    nct = n_ctx // TOK_TILE

    n_tok = bsz * t
    n_assign = n_tok * TOP_K
    n_blocks = -(-(n_assign + n_experts * (ROW_BLOCK - 1)) // ROW_BLOCK)
    n_slots = n_blocks * ROW_BLOCK

    for l in range(depth):
        w_ext, wqa, wqb, wk, wv, wohg, womla = _prep_layer_weights(
            w_in[l], w_uq[l], w_ukv[l], w_out[l], hw, q_lora, kv_lora, n_heads)
        (lf_fw, lf_bw, k_fw, k_bw, q_hg, v_hg, sg, q_rot, q_unrot, kk, vv) = _inproj(
            xc, mod3, l, bsz, nct, w_ext, lb_all[l][None, :], q_norm_w[l][None, :], kv_norm_w[l][None, :],
            wqa, wqb, wk, wv, cos_t, sin_t, n_heads)
        o_fw = _gla(lf_fw, k_fw, q_hg, v_hg, nct, reverse=False)
        o_hg = _gla(lf_bw, k_bw, q_hg, v_hg, nct, reverse=True, extra=(o_fw, sg, hg_norm_w[l][None, :]))
        o_mla = _attention(q_unrot, q_rot, kk, vv, n_ctx, n_heads)
        x1, h2p, shared, idx, wts, pos, counts = _postmix(
            xc, o_hg, o_mla, mod3, l, bsz, nct, wohg, womla, ln1_w[l][None, :], ln1_b[l][None, :],
            router_w[l], router_bias[l][None, :], shared_w1[l].astype(BF16), shared_w3[l].astype(BF16),
            shared_w2[l].astype(BF16), alpha)

        counts = counts[:, 0]
        idx = jnp.swapaxes(idx, 1, 2)
        pos = jnp.swapaxes(pos, 1, 2)
        wts = jnp.swapaxes(wts, 1, 2)
        padded = (counts + ROW_BLOCK - 1) // ROW_BLOCK * ROW_BLOCK
        pad_end = jnp.cumsum(padded)
        pad_start = pad_end - padded
        dest = (pad_start[idx.reshape(n_tok, TOP_K)] + pos.reshape(n_tok, TOP_K)).astype(I32)
        block_first_row = jnp.arange(n_blocks, dtype=I32) * ROW_BLOCK
        block_expert = jnp.minimum(
            jnp.sum((pad_end[None, :] <= block_first_row[:, None]).astype(I32), axis=1), n_experts - 1)
        n_used = (pad_end[-1:] // ROW_BLOCK).astype(I32)

        xs = _dispatch(dest, h2p.reshape(n_tok, hw), n_slots, (pad_start + counts).astype(I32),
                       (padded - counts).astype(I32))
        ys = _experts(xs, block_expert, n_used, moe_w1, moe_w3, moe_w2, l)
        xc = _combine(dest, wts, ys, shared, x1, mod3, l, bsz, n_ctx // COMBINE_TILE,
                      ln2_w[l][None, :], ln2_b[l][None, :], alpha, latent_only=(l == depth - 1))
    return xc
```

```python
import functools

import jax
import jax.numpy as jnp
from jax import lax
from jax.experimental import pallas as pl
from jax.experimental.pallas import tpu as pltpu

F32 = jnp.float32
BF16 = jnp.bfloat16
U32 = jnp.uint32
I32 = jnp.int32
HIGHEST = lax.Precision.HIGHEST

HG_HEAD_DIM = 128
MLA_V = 64
MLA_NOPE = 64
MLA_ROPE = 32
GRID_W = 64
ROPE_BASE = 10000.0
TOP_K = 8
ROUTED_SCALE = 2.5
N_MOD = 6
LN_EPS = 1e-6
RMS_EPS = 1e-6

LANE = 128
TOK_TILE = 256
GLA_CHUNK = 128
GLA_SUB = 32
GLA_SAFE_EXPONENT = 80.0
ATT_TQ = 256
ATT_TK = 1024
ATT_HEADS = 4
LOG2_E = 1.4426950408889634
ROW_BLOCK = 512
DISPATCH_TILE = 256
COMBINE_TILE = 256
VMEM_LIMIT = 56 * 1024 * 1024
MASKED = -1e30


def _cparams(sem):
    return pltpu.CompilerParams(dimension_semantics=sem, vmem_limit_bytes=VMEM_LIMIT)


def _sigmoid(z):
    return 1.0 / (1.0 + jnp.exp(-z))


def _silu(z):
    return z * _sigmoid(z)


def _normalize(x):
    mu = jnp.mean(x, axis=-1, keepdims=True)
    xc = x - mu
    var = jnp.mean(xc * xc, axis=-1, keepdims=True)
    return xc * lax.rsqrt(var + LN_EPS)


def _pack_bf16_pair(lo, hi):
    lo_u = lax.bitcast_convert_type(lo.astype(BF16).astype(F32), U32) >> 16
    hi_u = lax.bitcast_convert_type(hi.astype(BF16).astype(F32), U32) & jnp.uint32(0xFFFF0000)
    return hi_u | lo_u


def _unpack_bf16_pair(u):
    lo = lax.bitcast_convert_type(u << 16, F32)
    hi = lax.bitcast_convert_type(u & jnp.uint32(0xFFFF0000), F32)
    return lo, hi


def _mod_kernel(c_ref, w_ref, b_ref, o_ref):
    c = c_ref[...]
    o_ref[0] = jnp.dot(_silu(c), w_ref[0], precision=HIGHEST, preferred_element_type=F32) + b_ref[0]


def _modulation(c_rows, w_mod, b_mod):
    depth, d, n = w_mod.shape
    tn = 1536
    return pl.pallas_call(
        _mod_kernel,
        out_shape=jax.ShapeDtypeStruct((depth, 8, n), F32),
        grid=(depth, n // tn),
        in_specs=[
            pl.BlockSpec((8, d), lambda l, j: (0, 0)),
            pl.BlockSpec((1, d, tn), lambda l, j: (l, 0, j)),
            pl.BlockSpec((1, 1, tn), lambda l, j: (l, 0, j)),
        ],
        out_specs=pl.BlockSpec((1, 8, tn), lambda l, j: (l, 0, j)),
        compiler_params=_cparams(("arbitrary", "arbitrary")),
        name="modulation",
    )(c_rows, w_mod, b_mod.reshape(depth, 1, n))


def _inproj_kernel(x_ref, mod_ref, w_ref, lb_ref, qnw_ref, kvnw_ref, wqa_ref, wqb_ref, wk_ref, wv_ref,
                   cos_ref, sin_ref,
                   lff_ref, lfb_ref, kf_ref, kb_ref, q_ref, v_ref, sg_ref, qr_ref, qu_ref, kk_ref, vv_ref,
                   *, d, hw, q_lora, kv_lora, n_heads):
    x = x_ref[0]
    m = mod_ref[0]
    shift = m[:, 0:d]
    scale = m[:, d:2 * d]
    h = (_normalize(x) * (1.0 + scale) + shift).astype(BF16)
    proj = jnp.dot(h, w_ref[...], preferred_element_type=F32)

    lb = lb_ref[...]

    def forget(z, lbd):
        f = lbd + (1.0 - lbd) * _sigmoid(z)
        return jnp.log(f), 1.0 - f

    lf, kd = forget(proj[:, 0:hw], lb[:, 0:hw])
    lff_ref[0] = lf
    kf_ref[0] = kd.astype(BF16)
    lf, kd = forget(proj[:, hw:2 * hw], lb[:, hw:2 * hw])
    lfb_ref[0] = lf
    kb_ref[0] = kd.astype(BF16)
    v_ref[0] = proj[:, 2 * hw:3 * hw].astype(BF16)
    q_ref[0] = (_silu(proj[:, 3 * hw:4 * hw]) * (HG_HEAD_DIM ** -0.5)).astype(BF16)
    sg_ref[0] = _silu(proj[:, 4 * hw:5 * hw]).astype(BF16)

    o = 5 * hw
    cq = proj[:, o:o + q_lora]
    ckv = proj[:, o + q_lora:o + q_lora + kv_lora]
    o2 = o + q_lora + kv_lora
    kpe_a = proj[:, o2:o2 + LANE]
    kpe_b = proj[:, o2 + LANE:o2 + 2 * LANE]

    cqn = (cq * lax.rsqrt(jnp.mean(cq * cq, axis=-1, keepdims=True) + RMS_EPS) * qnw_ref[...]).astype(BF16)
    ckvn = (ckv * lax.rsqrt(jnp.mean(ckv * ckv, axis=-1, keepdims=True) + RMS_EPS) * kvnw_ref[...]).astype(BF16)

    cos = cos_ref[...]
    sin = sin_ref[...]
    cos_h = jnp.concatenate([cos] * n_heads, axis=1)
    sin_h = jnp.concatenate([sin] * n_heads, axis=1)
    att_scale = (MLA_NOPE + MLA_ROPE) ** -0.5 * LOG2_E
    qa = jnp.dot(cqn, wqa_ref[...], preferred_element_type=F32)
    qb = jnp.dot(cqn, wqb_ref[...], preferred_element_type=F32)
    qu_ref[0] = (qa * att_scale).astype(BF16)
    qr_ref[0] = ((qa * cos_h + qb * sin_h) * att_scale).astype(BF16)

    kr = kpe_a * cos + kpe_b * sin
    kk = jnp.dot(ckvn, wk_ref[...], preferred_element_type=F32) + jnp.concatenate([kr] * n_heads, axis=1)
    kk_ref[0] = kk.astype(BF16)
    vv = jnp.dot(ckvn, wv_ref[...], preferred_element_type=F32)
    lane = lax.broadcasted_iota(I32, vv.shape, 1)
    vv = jnp.where((lane % LANE) == MLA_V, 1.0, vv)
    vv_ref[0] = vv.astype(BF16)


def _inproj(xc, mod3, layer, n_batch, nct, w_ext, lb, qnw, kvnw, wqa, wqb, wk, wv, cos_t, sin_t, n_heads):
    b, t, d = xc.shape
    hw = d // 2
    q_lora = wqa.shape[0]
    kv_lora = wk.shape[0]
    hp = n_heads * LANE
    tm = TOK_TILE
    full = lambda shape: pl.BlockSpec(shape, lambda bi, ti: tuple(0 for _ in shape))
    tok = lambda w: pl.BlockSpec((1, tm, w), lambda bi, ti: (bi, ti, 0))
    out_shape = (
        jax.ShapeDtypeStruct((b, t, hw), F32), jax.ShapeDtypeStruct((b, t, hw), F32),
        jax.ShapeDtypeStruct((b, t, hw), BF16), jax.ShapeDtypeStruct((b, t, hw), BF16),
        jax.ShapeDtypeStruct((b, t, hw), BF16), jax.ShapeDtypeStruct((b, t, hw), BF16),
        jax.ShapeDtypeStruct((b, t, hw), BF16),
        jax.ShapeDtypeStruct((b, t, hp), BF16), jax.ShapeDtypeStruct((b, t, hp), BF16),
        jax.ShapeDtypeStruct((b, t, hp), BF16), jax.ShapeDtypeStruct((b, t, hp), BF16),
    )
    return pl.pallas_call(
        functools.partial(_inproj_kernel, d=d, hw=hw, q_lora=q_lora, kv_lora=kv_lora, n_heads=n_heads),
        out_shape=out_shape,
        grid=(b, t // tm),
        in_specs=[
            tok(d),
            pl.BlockSpec((1, 1, N_MOD * d), lambda bi, ti: (layer * 8 + jnp.where(ti < nct, n_batch, bi), 0, 0)),
            full(w_ext.shape), full(lb.shape), full(qnw.shape), full(kvnw.shape),
            full(wqa.shape), full(wqb.shape), full(wk.shape), full(wv.shape),
            pl.BlockSpec((tm, LANE), lambda bi, ti: (ti, 0)),
            pl.BlockSpec((tm, LANE), lambda bi, ti: (ti, 0)),
        ],
        out_specs=tuple([tok(hw)] * 7 + [tok(hp)] * 4),
        compiler_params=_cparams(("arbitrary", "arbitrary")),
        name="inproj",
    )(xc, mod3, w_ext, lb, qnw, kvnw, wqa, wqb, wk, wv, cos_t, sin_t)


def _gla_kernel(*refs, reverse, n_chunks, n_heads, final):
    if final:
        lf_ref, k_ref, q_ref, v_ref, ofw_ref, sg_ref, nw_ref, o_ref, st_ref = refs
    else:
        lf_ref, k_ref, q_ref, v_ref, o_ref, st_ref = refs
    c, s = GLA_CHUNK, GLA_SUB
    n_sub = c // s

    @pl.when(pl.program_id(1) == 0)
    def _():
        st_ref[...] = jnp.zeros_like(st_ref)

    row = lax.broadcasted_iota(I32, (c, c), 0)
    col = lax.broadcasted_iota(I32, (c, c), 1)
    tri_b = jnp.where((row <= col) if reverse else (row >= col), 1.0, 0.0).astype(BF16)
    sub_row = lax.broadcasted_iota(I32, (s, HG_HEAD_DIM), 0)
    ones = jnp.ones((HG_HEAD_DIM, HG_HEAD_DIM), BF16)

    nt = (((1,), (1,)), ((), ()))

    def sub_geometry(si):
        rs = slice(si * s, (si + 1) * s)
        if reverse:
            return rs, (slice((si + 1) * s, c), (si + 1) * s) if si < n_sub - 1 else (None, None)
        return rs, (slice(0, si * s), si * s - 1) if si > 0 else (None, None)

    def chunk_body(i, carry):
        ci = (n_chunks - 1 - i) if reverse else i
        r0 = pl.multiple_of(ci * c, c)
        rows = pl.ds(r0, c)
        lf_all = lf_ref[0, rows, :]
        lf_hi = lf_all.astype(BF16)
        lf_r = lf_all - lf_hi.astype(F32)
        lf_mid = lf_r.astype(BF16)
        lf_lo = (lf_r - lf_mid.astype(F32)).astype(BF16)
        bc_all = (jnp.dot(tri_b, lf_hi, preferred_element_type=F32)
                  + jnp.dot(tri_b, lf_mid, preferred_element_type=F32)
                  + jnp.dot(tri_b, lf_lo, preferred_element_type=F32))
        worst = None
        for si in range(n_sub):
            rs, (_, brow) = sub_geometry(si)
            far = bc_all[rs.start:rs.start + 1] if reverse else bc_all[rs.stop - 1:rs.stop]
            d = far if brow is None else far - bc_all[brow:brow + 1]
            worst = d if worst is None else jnp.minimum(worst, d)

        def load(h):
            sl = slice(h * HG_HEAD_DIM, (h + 1) * HG_HEAD_DIM)
            return (sl, bc_all[:, sl], k_ref[0, rows, sl].astype(F32), q_ref[0, rows, sl].astype(F32),
                    v_ref[0, rows, sl].astype(F32))

        def store(sl, o):
            if final:
                o = o + ofw_ref[0, rows, sl]
                o = o * lax.rsqrt(jnp.mean(o * o, axis=-1, keepdims=True) + RMS_EPS) * nw_ref[...]
                o_ref[0, rows, sl] = (o * sg_ref[0, rows, sl].astype(F32)).astype(o_ref.dtype)
            else:
                o_ref[0, rows, sl] = o

        def intra(h, fast):
            sl, bc, k, q, v = load(h)
            qhat = (q * jnp.exp(bc)).astype(BF16)
            o_state = lax.dot_general(qhat, st_ref[h].astype(BF16), nt, preferred_element_type=F32)
            pieces = []
            for si in range(n_sub):
                rs, (src, brow) = sub_geometry(si)
                bs, qs, ks, vs = bc[rs], q[rs], k[rs], v[rs]
                acc = o_state[rs]
                beta = jnp.zeros((1, HG_HEAD_DIM), F32) if brow is None else bc[brow:brow + 1]
                if fast:
                    src = slice(rs.start, c) if reverse else slice(0, rs.stop)
                if src is not None:
                    qi = (qs * jnp.exp(bs - beta)).astype(BF16)
                    ksrc = (k[src] * jnp.exp(beta - bc[src])).astype(BF16)
                    a = lax.dot_general(qi, ksrc, nt, preferred_element_type=F32)
                    if fast:
                        n_src = src.stop - src.start
                        r_i = lax.broadcasted_iota(I32, (s, n_src), 0)
                        c_i = lax.broadcasted_iota(I32, (s, n_src), 1)
                        keep = (c_i >= r_i) if reverse else (c_i <= r_i + rs.start)
                        a = jnp.where(keep, a, 0.0)
                    acc = acc + jnp.dot(a.astype(BF16), v[src].astype(BF16), preferred_element_type=F32)
                if not fast:
                    ws = []
                    for j in range(s):
                        mask = (sub_row <= j) if reverse else (sub_row >= j)
                        dlt = jnp.where(mask, bs - bs[j:j + 1], MASKED)
                        ws.append((jnp.exp(dlt) * qs * ks[j:j + 1]).astype(BF16))
                    sums = jnp.dot(jnp.concatenate(ws, axis=0), ones, preferred_element_type=F32)
                    for j in range(s):
                        acc = acc + sums[j * s:(j + 1) * s] * vs[j:j + 1]
                pieces.append(acc)
            store(sl, jnp.concatenate(pieces, axis=0))

        bounded = jnp.min(worst) > -GLA_SAFE_EXPONENT

        @pl.when(bounded)
        def _():
            for h in range(n_heads):
                intra(h, True)

        @pl.when(jnp.logical_not(bounded))
        def _():
            for h in range(n_heads):
                intra(h, False)

        for h in range(n_heads):
            sl, bc, k, _, v = load(h)
            tot = bc[0:1] if reverse else bc[c - 1:c]
            khat = (k * jnp.exp(tot - bc)).astype(BF16)
            st_ref[h] = st_ref[h] * jnp.exp(tot) + jnp.dot(v.T.astype(BF16), khat, preferred_element_type=F32)
        return carry

    lax.fori_loop(0, n_chunks, chunk_body, 0, unroll=True)


def _gla(lf, k, q, v, nct_blocks, reverse, extra=None):
    b, t, hw = lf.shape
    n_heads = hw // HG_HEAD_DIM
    blk = TOK_TILE
    nb = t // blk

    def blk_index(g):
        if not reverse:
            return g
        return jnp.where(g < nct_blocks, nct_blocks - 1 - g, nb - 1 - (g - nct_blocks))

    spec = pl.BlockSpec((1, blk, hw), lambda bi, g: (bi, blk_index(g), 0))
    in_specs = [spec, spec, spec, spec]
    args = [lf, k, q, v]
    final = extra is not None
    if final:
        ofw, sg, nw = extra
        in_specs += [spec, spec, pl.BlockSpec(nw.shape, lambda bi, g: (0, 0))]
        args += [ofw, sg, nw]
    return pl.pallas_call(
        functools.partial(_gla_kernel, reverse=reverse, n_chunks=blk // GLA_CHUNK, n_heads=n_heads, final=final),
        out_shape=jax.ShapeDtypeStruct((b, t, hw), BF16 if final else F32),
        grid=(b, nb),
        in_specs=in_specs,
        out_specs=spec,
        scratch_shapes=[pltpu.VMEM((n_heads, HG_HEAD_DIM, HG_HEAD_DIM), F32)],
        compiler_params=_cparams(("arbitrary", "arbitrary")),
        name="gla_bwd" if reverse else "gla_fwd",
    )(*args)


def _attn_kernel(qu_ref, qr_ref, k_ref, v_ref, o_ref, s_sc, *, n_ctx, n_lat):
    is_ctx = pl.program_id(2) < n_ctx // ATT_TQ
    nt = (((1,), (1,)), ((), ()))
    n_chunks = n_lat // ATT_TK
    heads = [slice(h * LANE, (h + 1) * LANE) for h in range(ATT_HEADS)]

    def ctx_scores(hs):
        return lax.dot_general(qu_ref[0, :, hs], k_ref[0, 0:n_ctx, hs], nt, preferred_element_type=F32)

    def finish(hs, acc):
        o_ref[0, :, hs] = (acc * (1.0 / acc[:, MLA_V:MLA_V + 1])).astype(o_ref.dtype)

    def lane_tile_max(s, m):
        for c in range(s.shape[1] // LANE):
            t = s[:, c * LANE:(c + 1) * LANE]
            m = t if m is None else jnp.maximum(m, t)
        return m

    @pl.when(is_ctx)
    def _():
        for hs in heads:
            s = ctx_scores(hs)
            p = jnp.exp2(s - jnp.max(s, axis=-1, keepdims=True))
            finish(hs, jnp.dot(p.astype(BF16), v_ref[0, 0:n_ctx, hs], preferred_element_type=F32))

    @pl.when(jnp.logical_not(is_ctx))
    def _():
        ms, accs = [], []
        for hi, hs in enumerate(heads):
            qr = qr_ref[0, :, hs]
            s_c = ctx_scores(hs)
            m_t = lane_tile_max(s_c, None)
            for j in range(n_chunks):
                rows = slice(n_ctx + j * ATT_TK, n_ctx + (j + 1) * ATT_TK)
                s = lax.dot_general(qr, k_ref[0, rows, hs], nt, preferred_element_type=F32)
                s_sc[hi, j] = s
                m_t = lane_tile_max(s, m_t)
            m = jnp.max(m_t, axis=-1, keepdims=True)
            ms.append(m)
            accs.append(jnp.dot(jnp.exp2(s_c - m).astype(BF16), v_ref[0, 0:n_ctx, hs],
                                preferred_element_type=F32))

        for j in range(n_chunks):
            rows = slice(n_ctx + j * ATT_TK, n_ctx + (j + 1) * ATT_TK)
            for hi, hs in enumerate(heads):
                p = jnp.exp2(s_sc[hi, j] - ms[hi]).astype(BF16)
                accs[hi] = accs[hi] + jnp.dot(p, v_ref[0, rows, hs], preferred_element_type=F32)
        for hi, hs in enumerate(heads):
            finish(hs, accs[hi])


def _attention(qu, qr, kk, vv, n_ctx, n_heads):
    b, t, hp = qu.shape
    n_lat = t - n_ctx
    w = ATT_HEADS * LANE
    qspec = pl.BlockSpec((1, ATT_TQ, w), lambda bi, h, qi: (bi, qi, h))
    kspec = pl.BlockSpec((1, t, w), lambda bi, h, qi: (bi, 0, h), pipeline_mode=pl.Buffered(1))
    return pl.pallas_call(
        functools.partial(_attn_kernel, n_ctx=n_ctx, n_lat=n_lat),
        out_shape=jax.ShapeDtypeStruct((b, t, hp), BF16),
        grid=(b, n_heads // ATT_HEADS, t // ATT_TQ),
        in_specs=[qspec, qspec, kspec, kspec],
        out_specs=qspec,
        scratch_shapes=[pltpu.VMEM((ATT_HEADS, n_lat // ATT_TK, ATT_TQ, ATT_TK), F32)],
        compiler_params=_cparams(("arbitrary", "arbitrary", "arbitrary")),
        name="mla_attention",
    )(qu, qr, kk, vv)


def _postmix_kernel(x_ref, ohg_ref, omla_ref, mod_ref, wohg_ref, womla_ref, ln1w_ref, ln1b_ref,
                    rwh_ref, rwl_ref, rb_ref, sw1_ref, sw3_ref, sw2_ref,
                    x1_ref, h2p_ref, sh_ref, idx_ref, wts_ref, pos_ref, cnt_ref, cnt_sc,
                    *, d, alpha, n_experts):
    @pl.when((pl.program_id(0) == 0) & (pl.program_id(1) == 0))
    def _():
        cnt_sc[...] = jnp.zeros_like(cnt_sc)

    m = mod_ref[0]
    g_a = m[:, 2 * d:3 * d]
    sh_f = m[:, 3 * d:4 * d]
    sc_f = m[:, 4 * d:5 * d]
    mix = (jnp.dot(ohg_ref[0], wohg_ref[...], preferred_element_type=F32)
           + jnp.dot(omla_ref[0], womla_ref[...], preferred_element_type=F32))
    x1 = _normalize(alpha * x_ref[0] + g_a * mix) * ln1w_ref[...] + ln1b_ref[...]
    x1_ref[0] = x1
    h2 = _normalize(x1) * (1.0 + sc_f) + sh_f
    hw = d // 2
    h2p_ref[0] = _pack_bf16_pair(h2[:, 0:hw], h2[:, hw:d])
    h2b = h2.astype(BF16)

    a1 = jnp.dot(h2b, sw1_ref[...], preferred_element_type=F32)
    a3 = jnp.dot(h2b, sw3_ref[...], preferred_element_type=F32)
    hid = (_silu(a1) * a3).astype(BF16)
    sh_ref[0] = jnp.dot(hid, sw2_ref[...], preferred_element_type=F32).astype(sh_ref.dtype)

    nt = (((1,), (1,)), ((), ()))
    h2lo = (h2 - h2b.astype(F32)).astype(BF16)
    logits = (lax.dot_general(rwh_ref[...], h2b, nt, preferred_element_type=F32)
              + lax.dot_general(rwh_ref[...], h2lo, nt, preferred_element_type=F32)
              + lax.dot_general(rwl_ref[...], h2b, nt, preferred_element_type=F32))
    scores = _sigmoid(logits)
    sel = scores + rb_ref[...][:, 0:1]
    tm = scores.shape[1]
    erow = lax.broadcasted_iota(I32, (n_experts, tm), 0).astype(F32)
    masks, tops, idxs = [], [], []
    for _ in range(TOP_K):
        mx = jnp.max(sel, axis=0, keepdims=True)
        ik = jnp.min(jnp.where(sel == mx, erow, float(n_experts)), axis=0, keepdims=True)
        oh = erow == ik
        masks.append(oh)
        idxs.append(ik)
        tops.append(jnp.sum(jnp.where(oh, scores, 0.0), axis=0, keepdims=True))
        sel = jnp.where(oh, -jnp.inf, sel)
    tsum = tops[0]
    for tk in tops[1:]:
        tsum = tsum + tk
    inv = ROUTED_SCALE / tsum

    ohf = jnp.zeros((n_experts, tm), F32)
    for oh in masks:
        ohf = jnp.where(oh, 1.0, ohf)
    r = lax.broadcasted_iota(I32, (tm, tm), 0)
    cc = lax.broadcasted_iota(I32, (tm, tm), 1)
    earlier = jnp.where(r < cc, 1.0, 0.0).astype(BF16)
    cnt = cnt_sc[...][:, 0:1]
    before = jnp.dot(ohf.astype(BF16), earlier, preferred_element_type=F32) + cnt
    cnt_new = cnt + jnp.sum(ohf, axis=1, keepdims=True)
    cnt_sc[...] = jnp.broadcast_to(cnt_new, cnt_sc.shape)
    cnt_ref[...] = jnp.broadcast_to(cnt_new, cnt_ref.shape).astype(I32)

    row_k = lax.broadcasted_iota(I32, (TOP_K, tm), 0)
    idx_o = jnp.zeros((TOP_K, tm), I32)
    wts_o = jnp.zeros((TOP_K, tm), F32)
    pos_o = jnp.zeros((TOP_K, tm), I32)
    for kk in range(TOP_K):
        pk = jnp.sum(jnp.where(masks[kk], before, 0.0), axis=0, keepdims=True)
        hit = row_k == kk
        idx_o = jnp.where(hit, idxs[kk].astype(I32), idx_o)
        wts_o = jnp.where(hit, tops[kk] * inv, wts_o)
        pos_o = jnp.where(hit, pk.astype(I32), pos_o)
    idx_ref[0] = idx_o
    wts_ref[0] = wts_o
    pos_ref[0] = pos_o


def _postmix(xc, ohg, omla, mod3, layer, n_batch, nct, wohg, womla, ln1w, ln1b, rw, rb, sw1, sw3, sw2, alpha):
    b, t, d = xc.shape
    tm = TOK_TILE
    n_experts = rw.shape[1]
    rwt = rw.T
    rwh = rwt.astype(BF16)
    rwl = (rwt - rwh.astype(F32)).astype(BF16)
    rb = jnp.broadcast_to(rb.reshape(n_experts, 1), (n_experts, LANE))
    full = lambda a: pl.BlockSpec(a.shape, lambda bi, ti: tuple(0 for _ in a.shape))
    tok = lambda w: pl.BlockSpec((1, tm, w), lambda bi, ti: (bi, ti, 0))
    out_shape = (
        jax.ShapeDtypeStruct((b, t, d), F32),
        jax.ShapeDtypeStruct((b, t, d // 2), U32),
        jax.ShapeDtypeStruct((b, t, d), BF16),
        jax.ShapeDtypeStruct((b, TOP_K, t), I32),
        jax.ShapeDtypeStruct((b, TOP_K, t), F32),
        jax.ShapeDtypeStruct((b, TOP_K, t), I32),
        jax.ShapeDtypeStruct((n_experts, LANE), I32),
    )
    kmaj = pl.BlockSpec((1, TOP_K, tm), lambda bi, ti: (bi, 0, ti))
    return pl.pallas_call(
        functools.partial(_postmix_kernel, d=d, alpha=alpha, n_experts=n_experts),
        out_shape=out_shape,
        grid=(b, t // tm),
        in_specs=[
            tok(d), tok(ohg.shape[2]), tok(omla.shape[2]),
            pl.BlockSpec((1, 1, N_MOD * d), lambda bi, ti: (layer * 8 + jnp.where(ti < nct, n_batch, bi), 0, 0)),
            full(wohg), full(womla), full(ln1w), full(ln1b), full(rwh), full(rwl), full(rb),
            full(sw1), full(sw3), full(sw2),
        ],
        out_specs=(tok(d), tok(d // 2), tok(d), kmaj, kmaj, kmaj,
                   pl.BlockSpec((n_experts, LANE), lambda bi, ti: (0, 0))),
        scratch_shapes=[pltpu.VMEM((n_experts, LANE), F32)],
        compiler_params=_cparams(("arbitrary", "arbitrary")),
        name="postmix_router",
    )(xc, ohg, omla, mod3, wohg, womla, ln1w, ln1b, rwh, rwl, rb, sw1, sw3, sw2)


def _row_copy(src_ref, src_row, dst_ref, dst_row, sem):
    return pltpu.make_async_copy(src_ref.at[pl.ds(src_row, 1)], dst_ref.at[pl.ds(dst_row, 1)], sem)


def _dispatch_kernel(pad_from_ref, pad_len_ref, dest_ref, h_ref, xs_ref, zeros, sem, zsem, *, n_experts):
    tile_rows = zeros.shape[0]

    def for_pad_rows(act):
        def per_expert(e, c):
            first, n = pad_from_ref[e], pad_len_ref[e]
            head = jnp.minimum((-first) & (tile_rows - 1), n)

            def per_row(r, c2):
                act(_row_copy(zeros, 0, xs_ref, first + r, zsem))
                return c2

            def per_tile(i, c2):
                start = pl.multiple_of(first + head + i * tile_rows, tile_rows)
                act(pltpu.make_async_copy(zeros, xs_ref.at[pl.ds(start, tile_rows)], zsem))
                return c2

            c = lax.fori_loop(0, head, per_row, c)
            return lax.fori_loop(0, lax.shift_right_logical(n - head, tile_rows.bit_length() - 1), per_tile, c)
        lax.fori_loop(0, n_experts, per_expert, 0)

    @pl.when(pl.program_id(0) == 0)
    def _():
        zeros[...] = jnp.zeros_like(zeros)
        for_pad_rows(lambda cp: cp.start())

    def issue(t, c):
        for kk in range(TOP_K):
            _row_copy(h_ref, t, xs_ref, dest_ref[0, 0, t * TOP_K + kk], sem).start(priority=kk % 2)
        return c

    lax.fori_loop(0, DISPATCH_TILE, issue, 0, unroll=2)

    def drain(t, c):
        for kk in range(TOP_K):
            _row_copy(h_ref, 0, xs_ref, 0, sem).wait()
        return c

    lax.fori_loop(0, DISPATCH_TILE, drain, 0, unroll=2)

    @pl.when(pl.program_id(0) == 0)
    def _():
        for_pad_rows(lambda cp: cp.wait())


def _dispatch(dest, h2p, n_slots, pad_from, pad_len):
    n, w = h2p.shape
    steps = n // DISPATCH_TILE
    dest3 = dest.reshape(steps, 1, DISPATCH_TILE * TOP_K)
    grid_spec = pltpu.PrefetchScalarGridSpec(
        num_scalar_prefetch=2,
        grid=(steps,),
        in_specs=[
            pl.BlockSpec((1, 1, DISPATCH_TILE * TOP_K), lambda i, pf, pn: (i, 0, 0), memory_space=pltpu.SMEM),
            pl.BlockSpec((DISPATCH_TILE, w), lambda i, pf, pn: (i, 0)),
        ],
        out_specs=pl.BlockSpec(memory_space=pl.ANY),
        scratch_shapes=[pltpu.VMEM((8, w), U32), pltpu.SemaphoreType.DMA, pltpu.SemaphoreType.DMA],
    )
    return pl.pallas_call(
        functools.partial(_dispatch_kernel, n_experts=pad_from.shape[0]),
        out_shape=jax.ShapeDtypeStruct((n_slots, w), U32),
        grid_spec=grid_spec,
        compiler_params=_cparams(("arbitrary",)),
        name="moe_dispatch",
    )(pad_from, pad_len, dest3, h2p)


def _expert_kernel(be_ref, nused_ref, xs_ref, w1_ref, w3_ref, w2_ref, ys_ref, w1b, w3b, w2b):
    i = pl.program_id(0)
    used = i < nused_ref[0]

    @pl.when(used)
    def _():
        changed = (i == 0) | (be_ref[i] != be_ref[jnp.maximum(i - 1, 0)])

        @pl.when(changed)
        def _():
            w1b[...] = w1_ref[0, 0].astype(BF16)
            w3b[...] = w3_ref[0, 0].astype(BF16)
            w2b[...] = w2_ref[0, 0].astype(BF16)

        lo, hi = _unpack_bf16_pair(xs_ref[...])
        x = jnp.concatenate([lo.astype(BF16), hi.astype(BF16)], axis=1)
        h1 = jnp.dot(x, w1b[...], preferred_element_type=F32)
        h3 = jnp.dot(x, w3b[...], preferred_element_type=F32)
        hid = (_silu(h1) * h3).astype(BF16)
        out = jnp.dot(hid, w2b[...], preferred_element_type=F32)
        half = out.shape[1] // 2
        ys_ref[...] = _pack_bf16_pair(out[:, 0:half], out[:, half:])

    @pl.when(jnp.logical_not(used))
    def _():
        ys_ref[...] = jnp.zeros_like(ys_ref)


def _experts(xs, block_expert, n_used, w1, w3, w2, layer):
    n_slots, w = xs.shape
    n_blocks = n_slots // ROW_BLOCK
    _, _, d, hid = w1.shape
    grid_spec = pltpu.PrefetchScalarGridSpec(
        num_scalar_prefetch=2,
        grid=(n_blocks,),
        in_specs=[
            pl.BlockSpec((ROW_BLOCK, w), lambda i, be, nu: (jnp.minimum(i, nu[0] - 1), 0)),
            pl.BlockSpec((1, 1, d, hid), lambda i, be, nu: (layer, be[i], 0, 0)),
            pl.BlockSpec((1, 1, d, hid), lambda i, be, nu: (layer, be[i], 0, 0)),
            pl.BlockSpec((1, 1, hid, d), lambda i, be, nu: (layer, be[i], 0, 0)),
        ],
        out_specs=pl.BlockSpec((ROW_BLOCK, w), lambda i, be, nu: (i, 0)),
        scratch_shapes=[pltpu.VMEM((d, hid), BF16), pltpu.VMEM((d, hid), BF16), pltpu.VMEM((hid, d), BF16)],
    )
    return pl.pallas_call(
        _expert_kernel,
        out_shape=jax.ShapeDtypeStruct((n_slots, w), U32),
        grid_spec=grid_spec,
        compiler_params=_cparams(("arbitrary",)),
        name="moe_experts",
    )(block_expert, n_used, xs, w1, w3, w2)


def _combine_kernel(dest_ref, dnext_ref, wts_ref, ys_ref, sh_ref, x1_ref, mod_ref, lnw_ref, lnb_ref, o_ref,
                    buf, sems, *, d, alpha, n_steps):
    g = pl.program_id(0) * pl.num_programs(1) + pl.program_id(1)
    cur = g % 2
    nxt = 1 - cur

    def issue_tile(tbl_ref, slot):
        def body(t, c):
            for kk in range(TOP_K):
                pltpu.make_async_copy(ys_ref.at[pl.ds(tbl_ref[0, 0, t * TOP_K + kk], 1)],
                                      buf.at[slot, kk, pl.ds(t, 1)], sems.at[slot]).start(priority=kk % 2)
            return c
        lax.fori_loop(0, COMBINE_TILE, body, 0, unroll=2)

    def drain(slot):
        def body(t, c):
            for _ in range(TOP_K):
                pltpu.make_async_copy(ys_ref.at[pl.ds(0, 1)], buf.at[slot, 0, pl.ds(0, 1)], sems.at[slot]).wait()
            return c
        lax.fori_loop(0, COMBINE_TILE, body, 0, unroll=2)

    @pl.when(g == 0)
    def _():
        issue_tile(dest_ref, 0)

    issue_tile(dnext_ref, nxt)
    drain(cur)

    hw = d // 2
    sh = sh_ref[0].astype(F32)
    acc_lo = sh[:, 0:hw]
    acc_hi = sh[:, hw:d]
    wts = wts_ref[0]
    for kk in range(TOP_K):
        lo, hi = _unpack_bf16_pair(buf[cur, kk])
        wk = wts[:, kk:kk + 1]
        acc_lo = acc_lo + wk * lo
        acc_hi = acc_hi + wk * hi
    ff = jnp.concatenate([acc_lo, acc_hi], axis=1)
    g_f = mod_ref[0][:, 5 * d:6 * d]
    o_ref[0] = _normalize(alpha * x1_ref[0] + g_f * ff) * lnw_ref[...] + lnb_ref[...]

    @pl.when(g == n_steps - 1)
    def _():
        drain(nxt)


def _combine(dest, wts, ys, shared, x1, mod3, layer, n_batch, nct_tiles, lnw, lnb, alpha, latent_only):
    b, t, d = x1.shape
    tm = COMBINE_TILE
    tpb = t // tm
    n_steps = b * tpb
    if latent_only:
        out_rows = t - nct_tiles * tm
        out_spec = pl.BlockSpec((1, tm, d), lambda bi, ti: (bi, jnp.maximum(ti - nct_tiles, 0), 0))
    else:
        out_rows = t
        out_spec = pl.BlockSpec((1, tm, d), lambda bi, ti: (bi, ti, 0))
    dest3 = dest.reshape(n_steps, 1, tm * TOP_K)
    tok = lambda w: pl.BlockSpec((1, tm, w), lambda bi, ti: (bi, ti, 0))
    full = lambda a: pl.BlockSpec(a.shape, lambda bi, ti: tuple(0 for _ in a.shape))
    table = lambda shift: pl.BlockSpec(
        (1, 1, tm * TOP_K), lambda bi, ti: (jnp.minimum(bi * tpb + ti + shift, n_steps - 1), 0, 0),
        memory_space=pltpu.SMEM)
    return pl.pallas_call(
        functools.partial(_combine_kernel, d=d, alpha=alpha, n_steps=n_steps),
        out_shape=jax.ShapeDtypeStruct((b, out_rows, d), F32),
        grid=(b, tpb),
        in_specs=[
            table(0), table(1),
            tok(TOP_K),
            pl.BlockSpec(memory_space=pl.ANY),
            tok(d), tok(d),
            pl.BlockSpec((1, 1, N_MOD * d), lambda bi, ti: (layer * 8 + jnp.where(ti < nct_tiles, n_batch, bi), 0, 0)),
            full(lnw), full(lnb),
        ],
        out_specs=out_spec,
        scratch_shapes=[pltpu.VMEM((2, TOP_K, tm, d // 2), U32), pltpu.SemaphoreType.DMA((2,))],
        compiler_params=_cparams(("arbitrary", "arbitrary")),
        name="moe_combine",
    )(dest3, dest3, wts, ys, shared, x1, mod3, lnw, lnb)


def _rope_tables(n_ctx, n_lat):
    pos = jnp.arange(n_lat, dtype=I32)
    rowp = (pos // GRID_W).astype(F32)
    colp = (pos % GRID_W).astype(F32)
    n_freq = MLA_ROPE // 4
    inv = ROPE_BASE ** (-jnp.arange(n_freq, dtype=F32) / n_freq)
    ang = jnp.concatenate([rowp[:, None] * inv, colp[:, None] * inv], axis=-1)
    ang = jnp.concatenate([jnp.zeros((n_ctx, MLA_ROPE // 2), F32), ang], axis=0)
    t = n_ctx + n_lat
    ones = jnp.ones((t, MLA_NOPE), F32)
    zeros_tail = jnp.zeros((t, LANE - MLA_NOPE - MLA_ROPE), F32)
    cos_t = jnp.concatenate([ones, jnp.cos(ang), jnp.cos(ang), zeros_tail], axis=1)
    sin_t = jnp.concatenate([jnp.zeros((t, MLA_NOPE), F32), jnp.sin(ang), jnp.sin(ang), zeros_tail], axis=1)
    return cos_t, sin_t


def _rope_swap(w):
    half = MLA_ROPE // 2
    return jnp.concatenate([-w[..., half:], w[..., :half]], axis=-1)


def _prep_layer_weights(w_in_l, w_uq_l, w_ukv_l, w_out_l, hw, q_lora, kv_lora, n_heads):
    d = w_in_l.shape[0]
    base = 5 * hw + q_lora + kv_lora
    kpe_w = w_in_l[:, base:base + MLA_ROPE]
    z_nope = jnp.zeros((d, MLA_NOPE), F32)
    z_tail = jnp.zeros((d, LANE - MLA_NOPE - MLA_ROPE), F32)
    w_ext = jnp.concatenate([w_in_l[:, :base], z_nope, kpe_w, z_tail, z_nope, _rope_swap(kpe_w), z_tail],
                            axis=1).astype(BF16)
    wq = w_uq_l.reshape(q_lora, n_heads, MLA_NOPE + MLA_ROPE)
    zq = jnp.zeros((q_lora, n_heads, LANE - MLA_NOPE - MLA_ROPE), F32)
    wqa = jnp.concatenate([wq, zq], axis=-1).reshape(q_lora, n_heads * LANE).astype(BF16)
    wqb = jnp.concatenate([jnp.zeros((q_lora, n_heads, MLA_NOPE), F32), _rope_swap(wq[..., MLA_NOPE:]), zq],
                          axis=-1).reshape(q_lora, n_heads * LANE).astype(BF16)
    wkv = w_ukv_l.reshape(kv_lora, n_heads, MLA_NOPE + MLA_V)
    wk = jnp.concatenate([wkv[..., :MLA_NOPE], jnp.zeros((kv_lora, n_heads, LANE - MLA_NOPE), F32)],
                         axis=-1).reshape(kv_lora, n_heads * LANE).astype(BF16)
    wv = jnp.concatenate([wkv[..., MLA_NOPE:], jnp.zeros((kv_lora, n_heads, LANE - MLA_V), F32)],
                         axis=-1).reshape(kv_lora, n_heads * LANE).astype(BF16)
    wohg = w_out_l[:hw].astype(BF16)
    wom = w_out_l[hw:].reshape(n_heads, MLA_V, d)
    womla = jnp.concatenate([wom, jnp.zeros((n_heads, LANE - MLA_V, d), F32)], axis=1)
    womla = womla.reshape(n_heads * LANE, d).astype(BF16)
    return w_ext, wqa, wqb, wk, wv, wohg, womla


def kernel(x, c, ctx, c_ctx, w_mod, b_mod, w_in, hg_lb, hg_norm_w, q_norm_w, w_uq, kv_norm_w, w_ukv, w_out,
           ln1_w, ln1_b, router_w, router_bias, moe_w1, moe_w3, moe_w2, shared_w1, shared_w3, shared_w2,
           ln2_w, ln2_b):
    bsz, n_lat, d = x.shape
    n_ctx = ctx.shape[1]
    depth = w_mod.shape[0]
    t = n_ctx + n_lat
    hw = d // 2
    q_lora = w_uq.shape[1]
    kv_lora = w_ukv.shape[1]
    n_heads = (d - hw) // MLA_V
    n_experts = router_w.shape[2]
    alpha = float((2 * depth) ** 0.25)
    assert n_ctx % TOK_TILE == 0 and n_lat % ATT_TK == 0 and bsz < 8
    assert n_ctx % ATT_TQ == 0 and (bsz * t) % DISPATCH_TILE == 0 and t % COMBINE_TILE == 0

    lb_all = jnp.cumsum(jax.nn.softmax(hg_lb.astype(F32), axis=0), axis=0)
    lb_all = lb_all - lb_all[:1]

    c_rows = jnp.concatenate([c, c_ctx[None, :], jnp.zeros((8 - bsz - 1, d), F32)], axis=0)
    mod = _modulation(c_rows, w_mod, b_mod)
    mod3 = mod.reshape(depth * 8, 1, N_MOD * d)

    cos_t, sin_t = _rope_tables(n_ctx, n_lat)
    xc = jnp.concatenate([ctx, x], axis=1)
    nct = n_ctx // TOK_TILE

    n_tok = bsz * t
    n_assign = n_tok * TOP_K
    n_blocks = -(-(n_assign + n_experts * (ROW_BLOCK - 1)) // ROW_BLOCK)
    n_slots = n_blocks * ROW_BLOCK

    for l in range(depth):
        w_ext, wqa, wqb, wk, wv, wohg, womla = _prep_layer_weights(
            w_in[l], w_uq[l], w_ukv[l], w_out[l], hw, q_lora, kv_lora, n_heads)
        (lf_fw, lf_bw, k_fw, k_bw, q_hg, v_hg, sg, q_rot, q_unrot, kk, vv) = _inproj(
            xc, mod3, l, bsz, nct, w_ext, lb_all[l][None, :], q_norm_w[l][None, :], kv_norm_w[l][None, :],
            wqa, wqb, wk, wv, cos_t, sin_t, n_heads)
        o_fw = _gla(lf_fw, k_fw, q_hg, v_hg, nct, reverse=False)
        o_hg = _gla(lf_bw, k_bw, q_hg, v_hg, nct, reverse=True, extra=(o_fw, sg, hg_norm_w[l][None, :]))
        o_mla = _attention(q_unrot, q_rot, kk, vv, n_ctx, n_heads)
        x1, h2p, shared, idx, wts, pos, counts = _postmix(
            xc, o_hg, o_mla, mod3, l, bsz, nct, wohg, womla, ln1_w[l][None, :], ln1_b[l][None, :],
            router_w[l], router_bias[l][None, :], shared_w1[l].astype(BF16), shared_w3[l].astype(BF16),
            shared_w2[l].astype(BF16), alpha)

        counts = counts[:, 0]
        idx = jnp.swapaxes(idx, 1, 2)
        pos = jnp.swapaxes(pos, 1, 2)
        wts = jnp.swapaxes(wts, 1, 2)
        padded = (counts + ROW_BLOCK - 1) // ROW_BLOCK * ROW_BLOCK
        pad_end = jnp.cumsum(padded)
        pad_start = pad_end - padded
        dest = (pad_start[idx.reshape(n_tok, TOP_K)] + pos.reshape(n_tok, TOP_K)).astype(I32)
        block_first_row = jnp.arange(n_blocks, dtype=I32) * ROW_BLOCK
        block_expert = jnp.minimum(
            jnp.sum((pad_end[None, :] <= block_first_row[:, None]).astype(I32), axis=1), n_experts - 1)
        n_used = (pad_end[-1:] // ROW_BLOCK).astype(I32)

        xs = _dispatch(dest, h2p.reshape(n_tok, hw), n_slots, (pad_start + counts).astype(I32),
                       (padded - counts).astype(I32))
        ys = _experts(xs, block_expert, n_used, moe_w1, moe_w3, moe_w2, l)
        xc = _combine(dest, wts, ys, shared, x1, mod3, l, bsz, n_ctx // COMBINE_TILE,
                      ln2_w[l][None, :], ln2_b[l][None, :], alpha, latent_only=(l == depth - 1))
    return xc
```
